```python
import math
import jax, jax.numpy as jnp
from jax import lax
import numpy as np

D_MODEL = 1024
BATCH = 8
SEQ = 2048
DEPTH = 1

N_META = 16
MLA_HEADS = 4
QK_NOPE_DIM = 128
QK_ROPE_DIM = 64
QK_HEAD_DIM = QK_NOPE_DIM + QK_ROPE_DIM
V_HEAD_DIM = 128
Q_LORA_RANK = 256
KV_LORA_RANK = 256
ROPE_THETA = 10000.0
Q_BLOCK = 128
DN_HEADS = 4
DN_HEAD_DIM = 128
DN_WIDTH = DN_HEADS * DN_HEAD_DIM
DN_CONV_WIDTH = 4
DN_CHUNK = 64
MIX_WIDTH = MLA_HEADS * V_HEAD_DIM + DN_WIDTH
IN_COLS = Q_LORA_RANK + KV_LORA_RANK + QK_ROPE_DIM + 3 * DN_WIDTH + DN_WIDTH + 2 * DN_HEADS
D_FF = 2816
FFN_CONV_WIDTH = 3
NORM_EPS = 1e-6

kernel_name = "hymba_mla_gdn_convglu_layer"


def _rmsnorm(x, w):
    xf = x.astype(jnp.float32)
    y = xf * lax.rsqrt(jnp.mean(xf * xf, axis=-1, keepdims=True) + NORM_EPS)
    return (y * w.astype(jnp.float32)).astype(x.dtype)


def _l2norm(x):
    return x * lax.rsqrt(jnp.sum(x * x, axis=-1, keepdims=True) + NORM_EPS)


def _causal_dwconv(x, w):
    width, channels = w.shape
    return lax.conv_general_dilated(
        x, w[:, None, :].astype(x.dtype), window_strides=(1,),
        padding=[(width - 1, 0)], dimension_numbers=("NWC", "WIO", "NWC"),
        feature_group_count=channels)


def _rope(x, pos):
    half = x.shape[-1] // 2
    inv_freq = ROPE_THETA ** (-jnp.arange(half, dtype=jnp.float32) / half)
    ang = pos.astype(jnp.float32)[:, None] * inv_freq[None, :]
    cos = jnp.cos(ang)[None, :, None, :]
    sin = jnp.sin(ang)[None, :, None, :]
    xf = x.astype(jnp.float32)
    x1, x2 = xf[..., :half], xf[..., half:]
    return jnp.concatenate([x1 * cos - x2 * sin, x1 * sin + x2 * cos], axis=-1).astype(x.dtype)


def _causal_attention(q, k, v):
    B, H, L, dq = q.shape
    dv = v.shape[-1]
    nb = -(-L // Q_BLOCK)
    Lp = nb * Q_BLOCK
    pad = ((0, 0), (0, 0), (0, Lp - L), (0, 0))
    q, k, v = jnp.pad(q, pad), jnp.pad(k, pad), jnp.pad(v, pad)
    scale = 1.0 / math.sqrt(dq)
    q_blocks = q.reshape(B, H, nb, Q_BLOCK, dq).transpose(2, 0, 1, 3, 4)
    key_pos = jnp.arange(Lp)

    def one_block(args):
        qb, start = args
        s = jnp.einsum("bhqd,bhkd->bhqk", qb, k).astype(jnp.float32) * scale
        q_pos = start + jnp.arange(Q_BLOCK)
        mask = key_pos[None, :] <= q_pos[:, None]
        p = jax.nn.softmax(jnp.where(mask, s, -jnp.inf), axis=-1)
        return jnp.einsum("bhqk,bhkd->bhqd", p.astype(v.dtype), v)

    out = lax.map(one_block, (q_blocks, jnp.arange(nb) * Q_BLOCK))
    return out.transpose(1, 2, 0, 3, 4).reshape(B, H, Lp, dv)[:, :, :L]


def _mla(q_lat, kv_lat, k_pe, pos, q_a_norm_w, w_q_b, kv_a_norm_w, w_kv_b,
         q_norm_w, k_norm_w, mla_out_norm_w):
    B, L, _ = q_lat.shape
    q = (_rmsnorm(q_lat, q_a_norm_w) @ w_q_b).reshape(B, L, MLA_HEADS, QK_HEAD_DIM)
    kv = (_rmsnorm(kv_lat, kv_a_norm_w) @ w_kv_b).reshape(B, L, MLA_HEADS, QK_NOPE_DIM + V_HEAD_DIM)
    k_nope, v = kv[..., :QK_NOPE_DIM], kv[..., QK_NOPE_DIM:]
    k = jnp.concatenate(
        [k_nope, jnp.broadcast_to(k_pe[:, :, None, :], (B, L, MLA_HEADS, QK_ROPE_DIM))], axis=-1)
    q = _rmsnorm(q, q_norm_w)
    k = _rmsnorm(k, k_norm_w)
    q = jnp.concatenate([q[..., :QK_NOPE_DIM], _rope(q[..., QK_NOPE_DIM:], pos)], axis=-1)
    k = jnp.concatenate([k[..., :QK_NOPE_DIM], _rope(k[..., QK_NOPE_DIM:], pos)], axis=-1)
    o = _causal_attention(q.transpose(0, 2, 1, 3), k.transpose(0, 2, 1, 3), v.transpose(0, 2, 1, 3))
    o = _rmsnorm(o.transpose(0, 2, 1, 3), mla_out_norm_w)
    return o.reshape(B, L, MLA_HEADS * V_HEAD_DIM)


def _chunk_gated_delta_rule(q, k, v, g, beta):
    B, H, T, dk = q.shape
    dv = v.shape[-1]
    C = DN_CHUNK
    N = T // C
    q = q * (1.0 / math.sqrt(dk))
    q = q.reshape(B, H, N, C, dk)
    k = k.reshape(B, H, N, C, dk)
    v = v.reshape(B, H, N, C, dv)
    g = jnp.cumsum(g.reshape(B, H, N, C), axis=-1)
    beta = beta.reshape(B, H, N, C)
    tri = jnp.tril(jnp.ones((C, C), dtype=bool))
    strict = jnp.tril(jnp.ones((C, C), dtype=jnp.float32), -1)
    decay = jnp.exp(jnp.where(tri, g[..., :, None] - g[..., None, :], -jnp.inf))
    k_beta = k * beta[..., None]
    v_beta = v * beta[..., None]
    a_strict = jnp.einsum("bhnid,bhnjd->bhnij", k_beta, k) * decay * strict
    eye = jnp.eye(C, dtype=jnp.float32)
    t_inv = lax.linalg.triangular_solve(
        eye + a_strict, jnp.broadcast_to(eye, a_strict.shape),
        left_side=True, lower=True, unit_diagonal=True)
    u = jnp.einsum("bhnij,bhnjd->bhnid", t_inv, v_beta)
    w = jnp.einsum("bhnij,bhnjd->bhnid", t_inv, k_beta * jnp.exp(g)[..., None])
    qk = jnp.einsum("bhnid,bhnjd->bhnij", q, k) * decay

    def step(S, inp):
        q_c, k_c, u_c, w_c, g_c, qk_c = inp
        v_new = u_c - jnp.einsum("bhcd,bhde->bhce", w_c, S)
        o = (jnp.einsum("bhcd,bhde->bhce", q_c * jnp.exp(g_c)[..., None], S)
             + jnp.einsum("bhij,bhje->bhie", qk_c, v_new))
        g_last = g_c[..., -1]
        S = (S * jnp.exp(g_last)[..., None, None]
             + jnp.einsum("bhcd,bhce->bhde", k_c * jnp.exp(g_last[..., None] - g_c)[..., None], v_new))
        return S, o

    to_scan = lambda t: jnp.moveaxis(t, 2, 0)
    S0 = jnp.zeros((B, H, dk, dv), jnp.float32)
    _, o = lax.scan(step, S0, (to_scan(q), to_scan(k), to_scan(u), to_scan(w), to_scan(g), to_scan(qk)))
    return jnp.moveaxis(o, 0, 2).reshape(B, H, T, dv)


def _gated_deltanet(qkv, z, a, b, dn_conv_w, dn_A_log, dn_dt_bias, dn_out_norm_w):
    B, L, _ = qkv.shape
    qkv = jax.nn.silu(_causal_dwconv(qkv, dn_conv_w)).astype(jnp.float32)
    heads = lambda t: t.reshape(B, L, DN_HEADS, DN_HEAD_DIM).transpose(0, 2, 1, 3)
    q = _l2norm(heads(qkv[..., :DN_WIDTH]))
    k = _l2norm(heads(qkv[..., DN_WIDTH:2 * DN_WIDTH]))
    v = heads(qkv[..., 2 * DN_WIDTH:])
    beta = jax.nn.sigmoid(b.astype(jnp.float32)).transpose(0, 2, 1)
    g = (-jnp.exp(dn_A_log.astype(jnp.float32))
         * jax.nn.softplus(a.astype(jnp.float32) + dn_dt_bias.astype(jnp.float32))).transpose(0, 2, 1)
    pad = (-N_META) % DN_CHUNK
    p4 = ((0, 0), (0, 0), (pad, 0), (0, 0))
    p3 = ((0, 0), (0, 0), (pad, 0))
    o = _chunk_gated_delta_rule(jnp.pad(q, p4), jnp.pad(k, p4), jnp.pad(v, p4),
                                jnp.pad(g, p3), jnp.pad(beta, p3))[:, :, pad:]
    o = _rmsnorm(o.transpose(0, 2, 1, 3), dn_out_norm_w)
    o = o * jax.nn.silu(z.reshape(B, L, DN_HEADS, DN_HEAD_DIM).astype(jnp.float32))
    return o.reshape(B, L, DN_WIDTH).astype(qkv.dtype if False else z.dtype)


def _conv_glu(h, w_gate, w_up, ffn_conv_w, ffn_conv_b, w_down):
    gate = _causal_dwconv(h @ w_gate, ffn_conv_w) + ffn_conv_b
    return (jax.nn.silu(gate) * (h @ w_up)) @ w_down


def setup_inputs(seed: int = 0) -> dict:
    key = jax.random.key(seed)
    ks = jax.random.split(key, 26)
    f32 = jnp.float32
    nrm = lambda k, shape, scale: jax.random.normal(k, shape, f32) * scale
    gain = lambda k, shape: 1.0 + 0.02 * jax.random.normal(k, shape, f32)
    Ld = DEPTH
    dt = jnp.exp(jax.random.uniform(ks[14], (Ld, DN_HEADS), f32, math.log(1e-3), math.log(1e-1)))
    return {
        "x": nrm(ks[0], (BATCH, SEQ, D_MODEL), 1.0),
        "meta_tokens": nrm(ks[1], (N_META, D_MODEL), 1.0),
        "attn_norm_w": gain(ks[2], (Ld, D_MODEL)),
        "w_in": nrm(ks[3], (Ld, D_MODEL, IN_COLS), D_MODEL ** -0.5),
        "q_a_norm_w": gain(ks[4], (Ld, Q_LORA_RANK)),
        "w_q_b": nrm(ks[5], (Ld, Q_LORA_RANK, MLA_HEADS * QK_HEAD_DIM), Q_LORA_RANK ** -0.5),
        "kv_a_norm_w": gain(ks[6], (Ld, KV_LORA_RANK)),
        "w_kv_b": nrm(ks[7], (Ld, KV_LORA_RANK, MLA_HEADS * (QK_NOPE_DIM + V_HEAD_DIM)), KV_LORA_RANK ** -0.5),
        "q_norm_w": gain(ks[8], (Ld, QK_HEAD_DIM)),
        "k_norm_w": gain(ks[9], (Ld, QK_HEAD_DIM)),
        "mla_out_norm_w": gain(ks[10], (Ld, V_HEAD_DIM)),
        "dn_conv_w": nrm(ks[11], (Ld, DN_CONV_WIDTH, 3 * DN_WIDTH), DN_CONV_WIDTH ** -0.5),
        "dn_A_log": jnp.log(jax.random.uniform(ks[12], (Ld, DN_HEADS), f32, 1.0, 16.0)),
        "dn_dt_bias": dt + jnp.log(-jnp.expm1(-dt)),
        "dn_out_norm_w": gain(ks[13], (Ld, DN_HEAD_DIM)),
        "w_out": nrm(ks[15], (Ld, MIX_WIDTH, D_MODEL), MIX_WIDTH ** -0.5),
        "ffn_norm_w": gain(ks[16], (Ld, D_MODEL)),
        "w_gate": nrm(ks[17], (Ld, D_MODEL, D_FF), D_MODEL ** -0.5),
        "w_up": nrm(ks[18], (Ld, D_MODEL, D_FF), D_MODEL ** -0.5),
        "ffn_conv_w": nrm(ks[19], (Ld, FFN_CONV_WIDTH, D_FF), FFN_CONV_WIDTH ** -0.5),
        "ffn_conv_b": nrm(ks[20], (Ld, D_FF), 0.01),
        "w_down": nrm(ks[21], (Ld, D_FF, D_MODEL), D_FF ** -0.5),
    }


def reference(x, meta_tokens, attn_norm_w, w_in, q_a_norm_w, w_q_b, kv_a_norm_w, w_kv_b,
              q_norm_w, k_norm_w, mla_out_norm_w, dn_conv_w, dn_A_log, dn_dt_bias,
              dn_out_norm_w, w_out, ffn_norm_w, w_gate, w_up, ffn_conv_w, ffn_conv_b, w_down):
    B = x.shape[0]
    meta = jnp.broadcast_to(meta_tokens[None].astype(x.dtype), (B, N_META, D_MODEL))
    h = jnp.concatenate([meta, x], axis=1)
    L = h.shape[1]
    pos = jnp.arange(L, dtype=jnp.int32)
    c1 = Q_LORA_RANK
    c2 = c1 + KV_LORA_RANK
    c3 = c2 + QK_ROPE_DIM
    c4 = c3 + 3 * DN_WIDTH
    c5 = c4 + DN_WIDTH
    c6 = c5 + DN_HEADS
    for l in range(DEPTH):
        u = _rmsnorm(h, attn_norm_w[l])
        proj = u @ w_in[l]
        q_lat, kv_lat, k_pe = proj[..., :c1], proj[..., c1:c2], proj[..., c2:c3]
        dn_qkv, dn_z = proj[..., c3:c4], proj[..., c4:c5]
        dn_a, dn_b = proj[..., c5:c6], proj[..., c6:]
        y_mla = _mla(q_lat, kv_lat, k_pe, pos, q_a_norm_w[l], w_q_b[l], kv_a_norm_w[l], w_kv_b[l],
                     q_norm_w[l], k_norm_w[l], mla_out_norm_w[l])
        y_dn = _gated_deltanet(dn_qkv, dn_z, dn_a, dn_b, dn_conv_w[l], dn_A_log[l], dn_dt_bias[l],
                               dn_out_norm_w[l])
        mixed = jnp.concatenate([y_mla, y_dn], axis=-1)
        h = h + mixed @ w_out[l]
        h = h + _conv_glu(_rmsnorm(h, ffn_norm_w[l]), w_gate[l], w_up[l], ffn_conv_w[l],
                          ffn_conv_b[l], w_down[l])
    return h[:, N_META:]
```

```python
import functools
import math

import jax
import jax.numpy as jnp
from jax import lax
from jax.experimental import pallas as pl
from jax.experimental.pallas import tpu as pltpu

F32 = jnp.float32
BF16 = jnp.bfloat16

D_MODEL = 1024
SEQ = 2048
N_META = 16
META_ROWS = 64
META_PAD = META_ROWS - N_META
L_PAD = META_ROWS + SEQ

MLA_HEADS = 4
QK_NOPE = 128
QK_ROPE = 64
QK_HEAD = QK_NOPE + QK_ROPE
V_HEAD = 128
Q_LORA = 256
KV_LORA = 256
ROPE_THETA = 10000.0
HEAD_SLOT = 256

DN_HEADS = 4
DN_DIM = 128
DN_WIDTH = DN_HEADS * DN_DIM
DN_CHUNK = 64
DN_STACK = DN_HEADS * DN_CHUNK
DN_CONV = 4

D_FF = 2816
FF_BLK = 256
NORM_EPS = 1e-6

C_LAT = 0
N_LAT = Q_LORA + KV_LORA + 2 * QK_ROPE
C_DN = N_LAT
N_DN = 4 * DN_WIDTH
C_AB = C_DN + N_DN
N_AB = 128
N_PROJ = C_AB + N_AB

VMEM_LIMIT = 56 * 1024 * 1024


def _cparams(sem):
    return pltpu.CompilerParams(dimension_semantics=sem, vmem_limit_bytes=VMEM_LIMIT)


def _rms(x, w):
    return x * lax.rsqrt(jnp.mean(x * x, axis=-1, keepdims=True) + NORM_EPS) * w


def _dot(a, b):
    return jnp.dot(a, b, preferred_element_type=F32)


def _dot_nt(a, b):
    return lax.dot_general(a, b, (((1,), (1,)), ((), ())), preferred_element_type=F32)


def _dot_tn(a, b):
    return lax.dot_general(a, b, (((0,), (0,)), ((), ())), preferred_element_type=F32)


def _silu(x):
    return x * jax.nn.sigmoid(x)


def _inproj_kernel(x_ref, nw_ref, w_ref, lat_ref, dn_ref, ab_ref):
    u = _rms(x_ref[...], nw_ref[...]).astype(BF16)
    lat_ref[...] = _dot(u, w_ref[:, C_LAT:C_LAT + N_LAT]).astype(BF16)
    dn_ref[...] = _dot(u, w_ref[:, C_DN:C_DN + N_DN]).astype(BF16)
    ab_ref[...] = _dot(u, w_ref[:, C_AB:C_AB + N_AB])


def _inproj(x2d, nw, w, row_tile):
    rows = x2d.shape[0]
    grid = (rows // row_tile,)
    return pl.pallas_call(
        _inproj_kernel,
        grid=grid,
        in_specs=[
            pl.BlockSpec((row_tile, D_MODEL), lambda i: (i, 0)),
            pl.BlockSpec((1, D_MODEL), lambda i: (0, 0)),
            pl.BlockSpec((D_MODEL, N_PROJ), lambda i: (0, 0)),
        ],
        out_specs=[
            pl.BlockSpec((row_tile, N_LAT), lambda i: (i, 0)),
            pl.BlockSpec((row_tile, N_DN), lambda i: (i, 0)),
            pl.BlockSpec((row_tile, N_AB), lambda i: (i, 0)),
        ],
        out_shape=[
            jax.ShapeDtypeStruct((rows, N_LAT), BF16),
            jax.ShapeDtypeStruct((rows, N_DN), BF16),
            jax.ShapeDtypeStruct((rows, N_AB), F32),
        ],
        compiler_params=_cparams(("arbitrary",)),
        name="inproj",
    )(x2d, nw, w)


ATT_TQ = 256
ATT_PROJ_ROWS = 512
NEG_INF = float("-inf")


def _mla_kernel(latx_ref, latm_ref, cs_ref, qaw_ref, wq_ref, kvaw_ref, wk_ref, wv_ref,
                qnn_ref, qrw_ref, knn_ref, krw_ref, onw_ref,
                yx_ref, ym_ref, q_s, k_s, v_s):
    low = lax.broadcasted_iota(jnp.int32, (1, 128), 1) < QK_ROPE

    def rope(rr, cs, rw):
        a = rr * (cs * rw)
        return jnp.where(low, a + pltpu.roll(a, QK_ROPE, 1), 0.0)

    def project(lat, cs, r0, nrows):
        qlat = lat[:, 0:Q_LORA]
        kvlat = lat[:, Q_LORA:Q_LORA + KV_LORA]
        pe = lat[:, Q_LORA + KV_LORA:N_LAT]
        qf = _dot(_rms(qlat, qaw_ref[...]).astype(BF16), wq_ref[...])
        kvn = _rms(kvlat, kvaw_ref[...]).astype(BF16)
        kn = _dot(kvn, wk_ref[...])
        v_s[r0:r0 + nrows, :] = _dot(kvn, wv_ref[...]).astype(BF16)
        k_rope = rope(pe, cs, krw_ref[...])
        pe_ss = jnp.sum(jnp.where(low, pe * pe, 0.0), axis=-1, keepdims=True)
        for h in range(MLA_HEADS):
            c0 = h * HEAD_SLOT
            nope = kn[:, h * QK_NOPE:(h + 1) * QK_NOPE]
            ss = jnp.sum(nope * nope, axis=-1, keepdims=True) + pe_ss
            rs = lax.rsqrt(ss * (1.0 / QK_HEAD) + NORM_EPS)
            k_s[r0:r0 + nrows, c0:c0 + 128] = (nope * rs * knn_ref[...]).astype(BF16)
            k_s[r0:r0 + nrows, c0 + 128:c0 + 256] = (k_rope * rs).astype(BF16)
            qnope = qf[:, c0:c0 + 128]
            qrr = qf[:, c0 + 128:c0 + 256]
            ssq = (jnp.sum(qnope * qnope, axis=-1, keepdims=True)
                   + jnp.sum(jnp.where(low, qrr * qrr, 0.0), axis=-1, keepdims=True))
            rsq = lax.rsqrt(ssq * (1.0 / QK_HEAD) + NORM_EPS) * (1.0 / math.sqrt(QK_HEAD))
            q_s[r0:r0 + nrows, c0:c0 + 128] = (qnope * rsq * qnn_ref[...]).astype(BF16)
            q_s[r0:r0 + nrows, c0 + 128:c0 + 256] = (rope(qrr, cs, qrw_ref[...]) * rsq).astype(BF16)

    project(latm_ref[...].astype(F32), cs_ref[0:META_ROWS, :], 0, META_ROWS)
    for c in range(SEQ // ATT_PROJ_ROWS):
        r0 = META_ROWS + c * ATT_PROJ_ROWS
        project(latx_ref[0, c * ATT_PROJ_ROWS:(c + 1) * ATT_PROJ_ROWS, :].astype(F32),
                cs_ref[r0:r0 + ATT_PROJ_ROWS, :], r0, ATT_PROJ_ROWS)

    def finish(acc, l, h):
        o = acc / l
        return _rms(o, onw_ref[...]).astype(BF16)

    mrow = lax.broadcasted_iota(jnp.int32, (META_ROWS, META_ROWS), 0)
    mcol = lax.broadcasted_iota(jnp.int32, (META_ROWS, META_ROWS), 1)
    meta_mask = (mcol <= mrow) & ((mcol >= META_PAD) | (mcol == mrow))
    meta_key_ok = lax.broadcasted_iota(jnp.int32, (ATT_TQ, META_ROWS), 1) >= META_PAD
    drow = lax.broadcasted_iota(jnp.int32, (ATT_TQ, ATT_TQ), 0)
    dcol = lax.broadcasted_iota(jnp.int32, (ATT_TQ, ATT_TQ), 1)
    diag_mask = dcol <= drow

    for h in range(MLA_HEADS):
        c0 = h * HEAD_SLOT
        v0 = h * V_HEAD
        km = k_s[0:META_ROWS, c0:c0 + HEAD_SLOT]
        vm = v_s[0:META_ROWS, v0:v0 + V_HEAD]

        s = jnp.where(meta_mask, _dot_nt(q_s[0:META_ROWS, c0:c0 + HEAD_SLOT], km), NEG_INF)
        m = jnp.max(s, axis=-1, keepdims=True)
        p = jnp.exp(s - m)
        l = jnp.sum(p, axis=-1, keepdims=True)
        ym_ref[0, :, v0:v0 + V_HEAD] = finish(_dot(p.astype(BF16), vm), l, h)

        def q_tile(i, carry, c0=c0, v0=v0, km=km, vm=vm):
            q0 = pl.multiple_of(META_ROWS + i * ATT_TQ, 64)
            q = q_s[pl.ds(q0, ATT_TQ), c0:c0 + HEAD_SLOT]
            s = jnp.where(meta_key_ok, _dot_nt(q, km), NEG_INF)
            m = jnp.max(s, axis=-1, keepdims=True)
            p = jnp.exp(s - m)
            l = jnp.sum(p, axis=-1, keepdims=True)
            acc = _dot(p.astype(BF16), vm)

            def update(s, m, l, acc, vj):
                m_new = jnp.maximum(m, jnp.max(s, axis=-1, keepdims=True))
                alpha = jnp.exp(m - m_new)
                p = jnp.exp(s - m_new)
                l = alpha * l + jnp.sum(p, axis=-1, keepdims=True)
                acc = alpha * acc + _dot(p.astype(BF16), vj)
                return m_new, l, acc

            def kv_block(j, st):
                m, l, acc = st
                k0 = pl.multiple_of(META_ROWS + j * ATT_TQ, 64)
                kj = k_s[pl.ds(k0, ATT_TQ), c0:c0 + HEAD_SLOT]
                vj = v_s[pl.ds(k0, ATT_TQ), v0:v0 + V_HEAD]
                return update(_dot_nt(q, kj), m, l, acc, vj)

            m, l, acc = lax.fori_loop(0, i, kv_block, (m, l, acc))
            kd = k_s[pl.ds(q0, ATT_TQ), c0:c0 + HEAD_SLOT]
            vd = v_s[pl.ds(q0, ATT_TQ), v0:v0 + V_HEAD]
            s = jnp.where(diag_mask, _dot_nt(q, kd), NEG_INF)
            m, l, acc = update(s, m, l, acc, vd)
            x0 = pl.multiple_of(i * ATT_TQ, ATT_TQ)
            yx_ref[0, pl.ds(x0, ATT_TQ), v0:v0 + V_HEAD] = finish(acc, l, h)
            return carry

        lax.fori_loop(0, SEQ // ATT_TQ, q_tile, 0)


def _mla(latx, latm, cs, qaw, wq, kvaw, wk, wv, qnn, qrw, knn, krw, onw):
    nb = latx.shape[0]
    const = lambda shape: pl.BlockSpec(shape, lambda b: (0,) * len(shape))
    return pl.pallas_call(
        _mla_kernel,
        grid=(nb,),
        in_specs=[
            pl.BlockSpec((1, SEQ, N_LAT), lambda b: (b, 0, 0)),
            const((META_ROWS, N_LAT)),
            const((L_PAD, 128)),
            const((1, Q_LORA)),
            const((Q_LORA, MLA_HEADS * HEAD_SLOT)),
            const((1, KV_LORA)),
            const((KV_LORA, MLA_HEADS * QK_NOPE)),
            const((KV_LORA, MLA_HEADS * V_HEAD)),
            const((1, 128)), const((1, 128)), const((1, 128)), const((1, 128)), const((1, 128)),
        ],
        out_specs=[
            pl.BlockSpec((1, SEQ, MLA_HEADS * V_HEAD), lambda b: (b, 0, 0)),
            pl.BlockSpec((1, META_ROWS, MLA_HEADS * V_HEAD), lambda b: (b, 0, 0)),
        ],
        out_shape=[
            jax.ShapeDtypeStruct((nb, SEQ, MLA_HEADS * V_HEAD), BF16),
            jax.ShapeDtypeStruct((nb, META_ROWS, MLA_HEADS * V_HEAD), BF16),
        ],
        scratch_shapes=[
            pltpu.VMEM((L_PAD, MLA_HEADS * HEAD_SLOT), BF16),
            pltpu.VMEM((L_PAD, MLA_HEADS * HEAD_SLOT), BF16),
            pltpu.VMEM((L_PAD, MLA_HEADS * V_HEAD), BF16),
        ],
        compiler_params=_cparams(("arbitrary",)),
        name="mla",
    )(latx, latm, cs, qaw, wq, kvaw, wk, wv, qnn, qrw, knn, krw, onw)


def _deltanet_kernel(dnx_ref, dnm_ref, abx_ref, abm_ref, cw_ref, alog_ref, dtb_ref, onw_ref,
                     yx_ref, ym_ref, s_s, cbuf):
    C = DN_CHUNK
    R = DN_STACK
    row = lax.broadcasted_iota(jnp.int32, (R, R), 0)
    col = lax.broadcasted_iota(jnp.int32, (R, R), 1)
    same = lambda sh: jnp.right_shift(row, sh) == jnp.right_shift(col, sh)
    m_incl = same(6) & (col <= row)
    m_strict = same(6) & (col < row)
    m_d16 = m_strict & same(4)
    m_l32 = m_strict & same(5) & jnp.logical_not(same(4))
    m_l64 = m_strict & jnp.logical_not(same(5))
    tri_c = (lax.broadcasted_iota(jnp.int32, (C, C), 1)
             <= lax.broadcasted_iota(jnp.int32, (C, C), 0)).astype(F32)
    eye = (row == col).astype(F32)
    neg_a = -jnp.exp(alog_ref[...])

    s_s[...] = jnp.zeros_like(s_s)
    cbuf[0:8, :] = jnp.zeros((8, 3 * DN_WIDTH), F32)

    def stack(x):
        return jnp.concatenate([x[:, h * DN_DIM:(h + 1) * DN_DIM] for h in range(DN_HEADS)], axis=0)

    def stack_col(x, c0):
        return jnp.concatenate(
            [jnp.broadcast_to(x[:, c0 + h:c0 + h + 1], (C, DN_DIM)) for h in range(DN_HEADS)], axis=0)

    def bdot(a, b):
        return _dot(a.astype(BF16), b.astype(BF16))

    def chunk(dn, ab, row_ok):
        pre = dn[:, 0:3 * DN_WIDTH].astype(F32)
        z = dn[:, 3 * DN_WIDTH:].astype(F32)
        cbuf[8:8 + C, :] = pre
        conv = (cw_ref[3:4, :] * pre + cw_ref[2:3, :] * cbuf[7:7 + C, :]
                + cw_ref[1:2, :] * cbuf[6:6 + C, :] + cw_ref[0:1, :] * cbuf[5:5 + C, :])
        cbuf[0:8, :] = cbuf[C:C + 8, :]
        act = _silu(conv)
        q = stack(act[:, 0:DN_WIDTH])
        k = stack(act[:, DN_WIDTH:2 * DN_WIDTH])
        v = stack(act[:, 2 * DN_WIDTH:3 * DN_WIDTH])
        q = q * lax.rsqrt(jnp.sum(q * q, axis=-1, keepdims=True) + NORM_EPS) * (1.0 / math.sqrt(DN_DIM))
        k = k * lax.rsqrt(jnp.sum(k * k, axis=-1, keepdims=True) + NORM_EPS)

        xa = ab + dtb_ref[...]
        softplus = jnp.maximum(xa, 0.0) + jnp.log(1.0 + jnp.exp(-jnp.abs(xa)))
        g = neg_a * softplus
        beta = jax.nn.sigmoid(ab)
        if row_ok is not None:
            g = jnp.where(row_ok, g, 0.0)
            beta = jnp.where(row_ok, beta, 0.0)
        gc = jnp.dot(tri_c, g, preferred_element_type=F32, precision=lax.Precision.HIGHEST)
        gcs = stack_col(gc, 0)
        gls = stack_col(jnp.broadcast_to(gc[C - 1:C, :], (C, 128)), 0)
        bs = stack_col(beta, DN_HEADS)
        grow = gcs.T[0:1, :]
        dlog = gcs[:, 0:1] - grow
        dec = jnp.exp(jnp.where(m_incl, dlog, NEG_INF))

        kb = k * bs
        vb = v * bs
        kbf = k.astype(BF16)
        a_full = _dot_nt(kb.astype(BF16), kbf) * dec
        b1 = jnp.where(m_d16, a_full, 0.0)
        b2 = bdot(b1, b1)
        t = eye - b1
        t = t + bdot(t, b2)
        b4 = bdot(b2, b2)
        t = t + bdot(t, b4)
        b8 = bdot(b4, b4)
        t = t + bdot(t, b8)
        t = t - bdot(t, bdot(jnp.where(m_l32, a_full, 0.0), t))
        t = t - bdot(t, bdot(jnp.where(m_l64, a_full, 0.0), t))

        eg = jnp.exp(gcs)
        uw = bdot(t, jnp.concatenate([vb, kb * eg], axis=1))
        qk = jnp.where(m_incl, _dot_nt(q.astype(BF16), kbf) * dec, 0.0)
        qg = q * eg
        kd = k * jnp.exp(gls - gcs)
        el = jnp.exp(gls)

        vnew = []
        o_inter = []
        for h in range(DN_HEADS):
            r0 = h * C
            s_h = s_s[h]
            vn = uw[r0:r0 + C, 0:DN_DIM] - bdot(uw[r0:r0 + C, DN_DIM:], s_h)
            o_inter.append(bdot(qg[r0:r0 + C, :], s_h))
            s_s[h] = s_h * el[r0:r0 + 1, :] + _dot_tn(kd[r0:r0 + C, :].astype(BF16), vn.astype(BF16))
            vnew.append(vn)
        o = jnp.concatenate(o_inter, axis=0) + bdot(qk, jnp.concatenate(vnew, axis=0))
        o = _rms(o, onw_ref[...])
        out = jnp.concatenate([o[h * C:(h + 1) * C, :] for h in range(DN_HEADS)], axis=1) * _silu(z)
        return out.astype(BF16)

    meta_ok = lax.broadcasted_iota(jnp.int32, (C, 128), 0) >= META_PAD
    ym_ref[0] = chunk(dnm_ref[...], abm_ref[...], meta_ok)

    def body(c, carry):
        r0 = pl.multiple_of(c * C, C)
        yx_ref[0, pl.ds(r0, C), :] = chunk(dnx_ref[0, pl.ds(r0, C), :], abx_ref[0, pl.ds(r0, C), :], None)
        return carry

    lax.fori_loop(0, SEQ // C, body, 0)


def _deltanet(dnx, dnm, abx, abm, cw, alog, dtb, onw):
    nb = dnx.shape[0]
    const = lambda shape: pl.BlockSpec(shape, lambda b: (0,) * len(shape))
    return pl.pallas_call(
        _deltanet_kernel,
        grid=(nb,),
        in_specs=[
            pl.BlockSpec((1, SEQ, N_DN), lambda b: (b, 0, 0)),
            const((META_ROWS, N_DN)),
            pl.BlockSpec((1, SEQ, N_AB), lambda b: (b, 0, 0)),
            const((META_ROWS, N_AB)),
            const((DN_CONV, 3 * DN_WIDTH)),
            const((1, 128)), const((1, 128)), const((1, 128)),
        ],
        out_specs=[
            pl.BlockSpec((1, SEQ, DN_WIDTH), lambda b: (b, 0, 0)),
            pl.BlockSpec((1, META_ROWS, DN_WIDTH), lambda b: (b, 0, 0)),
        ],
        out_shape=[
            jax.ShapeDtypeStruct((nb, SEQ, DN_WIDTH), BF16),
            jax.ShapeDtypeStruct((nb, META_ROWS, DN_WIDTH), BF16),
        ],
        scratch_shapes=[
            pltpu.VMEM((DN_HEADS, DN_DIM, DN_DIM), F32),
            pltpu.VMEM((8 + DN_CHUNK, 3 * DN_WIDTH), F32),
        ],
        compiler_params=_cparams(("arbitrary",)),
        name="deltanet",
    )(dnx, dnm, abx, abm, cw, alog, dtb, onw)


FFN_ROWS = 1024
FFN_HALO = 16


def _ffn_kernel(x_ref, xh_ref, mh_ref, ya_ref, yah_ref, yam_ref, yd_ref, ydh_ref, ydm_ref,
                wo_ref, nw_ref, wg_ref, wu_ref, cw_ref, cb_ref, wd_ref,
                o_ref, u_s, g_s):
    r = pl.program_id(1)
    f = pl.program_id(2)

    @pl.when(f == 0)
    def _():
        mixed = jnp.concatenate([ya_ref[0], yd_ref[0]], axis=1)
        h_mid = x_ref[0] + _dot(mixed, wo_ref[...])
        o_ref[0] = h_mid
        u_s[FFN_HALO:, :] = _rms(h_mid, nw_ref[...]).astype(BF16)
        first = r == 0
        mixed_h = jnp.concatenate([jnp.where(first, yam_ref[0], yah_ref[0]),
                                   jnp.where(first, ydm_ref[0], ydh_ref[0])], axis=1)
        h_halo = jnp.where(first, mh_ref[...], xh_ref[0]) + _dot(mixed_h, wo_ref[...])
        u_s[0:FFN_HALO, :] = _rms(h_halo, nw_ref[...]).astype(BF16)

    g_s[...] = _dot(u_s[...], wg_ref[...])
    up = _dot(u_s[FFN_HALO:, :], wu_ref[...])
    gate = (cw_ref[2:3, :] * g_s[FFN_HALO:, :]
            + cw_ref[1:2, :] * g_s[FFN_HALO - 1:FFN_HALO - 1 + FFN_ROWS, :]
            + cw_ref[0:1, :] * g_s[FFN_HALO - 2:FFN_HALO - 2 + FFN_ROWS, :]
            + cb_ref[...])
    act = (_silu(gate) * up).astype(BF16)
    o_ref[0] += _dot(act, wd_ref[...])


def _ffn(x, hp_meta, yax, yam, ydx, ydm, wo, nw, wg, wu, cw, cb, wd):
    nb = x.shape[0]
    nr = SEQ // FFN_ROWS
    hb = FFN_ROWS // FFN_HALO
    halo_idx = lambda b, r, f: (b, jnp.maximum(r * hb - 1, 0), 0)
    meta_idx = lambda b, r, f: (b, META_ROWS // FFN_HALO - 1, 0)
    main_idx = lambda b, r, f: (b, r, 0)
    const2 = lambda shape: pl.BlockSpec(shape, lambda b, r, f: (0, 0))
    return pl.pallas_call(
        _ffn_kernel,
        grid=(nb, nr, D_FF // FF_BLK),
        in_specs=[
            pl.BlockSpec((1, FFN_ROWS, D_MODEL), main_idx),
            pl.BlockSpec((1, FFN_HALO, D_MODEL), halo_idx),
            pl.BlockSpec((FFN_HALO, D_MODEL), lambda b, r, f: (META_ROWS // FFN_HALO - 1, 0)),
            pl.BlockSpec((1, FFN_ROWS, MLA_HEADS * V_HEAD), main_idx),
            pl.BlockSpec((1, FFN_HALO, MLA_HEADS * V_HEAD), halo_idx),
            pl.BlockSpec((1, FFN_HALO, MLA_HEADS * V_HEAD), meta_idx),
            pl.BlockSpec((1, FFN_ROWS, DN_WIDTH), main_idx),
            pl.BlockSpec((1, FFN_HALO, DN_WIDTH), halo_idx),
            pl.BlockSpec((1, FFN_HALO, DN_WIDTH), meta_idx),
            const2((D_MODEL, D_MODEL)),
            const2((1, D_MODEL)),
            pl.BlockSpec((D_MODEL, FF_BLK), lambda b, r, f: (0, f)),
            pl.BlockSpec((D_MODEL, FF_BLK), lambda b, r, f: (0, f)),
            pl.BlockSpec((3, FF_BLK), lambda b, r, f: (0, f)),
            pl.BlockSpec((1, FF_BLK), lambda b, r, f: (0, f)),
            pl.BlockSpec((FF_BLK, D_MODEL), lambda b, r, f: (f, 0)),
        ],
        out_specs=pl.BlockSpec((1, FFN_ROWS, D_MODEL), main_idx),
        out_shape=jax.ShapeDtypeStruct((nb, SEQ, D_MODEL), F32),
        scratch_shapes=[
            pltpu.VMEM((FFN_HALO + FFN_ROWS, D_MODEL), BF16),
            pltpu.VMEM((FFN_HALO + FFN_ROWS, FF_BLK), F32),
        ],
        compiler_params=_cparams(("arbitrary", "arbitrary", "arbitrary")),
        name="outproj_ffn",
    )(x, x, hp_meta, yax, yax, yam, ydx, ydx, ydm, wo, nw, wg, wu, cw, cb, wd)


def _rot_cols(w):
    half = QK_ROPE // 2
    return jnp.concatenate([-w[..., half:], w[..., :half]], axis=-1)


def _swap_halves(w):
    half = QK_ROPE // 2
    return jnp.concatenate([w[..., half:], w[..., :half]], axis=-1)


def _pad_lanes(v, n=128):
    return jnp.pad(v.astype(F32), (0, n - v.shape[0])).reshape(1, n)


def _layer(x, hp_meta, l, attn_norm_w, w_in, q_a_norm_w, w_q_b, kv_a_norm_w, w_kv_b, q_norm_w,
           k_norm_w, mla_out_norm_w, dn_conv_w, dn_A_log, dn_dt_bias, dn_out_norm_w, w_out,
           ffn_norm_w, w_gate, w_up, ffn_conv_w, ffn_conv_b, w_down):
    nb = x.shape[0]
    c1 = Q_LORA
    c2 = c1 + KV_LORA
    c3 = c2 + QK_ROPE
    c4 = c3 + 3 * DN_WIDTH
    c5 = c4 + DN_WIDTH
    win = w_in[l]
    k_pe_w = win[:, c2:c3]
    w1 = jnp.concatenate(
        [win[:, :c2], k_pe_w, _rot_cols(k_pe_w), win[:, c3:c5], win[:, c5:],
         jnp.zeros((D_MODEL, N_AB - 2 * DN_HEADS), F32)], axis=1).astype(BF16)

    wqb = w_q_b[l].reshape(Q_LORA, MLA_HEADS, QK_HEAD)
    wq = jnp.concatenate([wqb[..., :QK_NOPE], wqb[..., QK_NOPE:], _rot_cols(wqb[..., QK_NOPE:])],
                         axis=-1).reshape(Q_LORA, MLA_HEADS * HEAD_SLOT).astype(BF16)
    wkvb = w_kv_b[l].reshape(KV_LORA, MLA_HEADS, QK_NOPE + V_HEAD)
    wk = wkvb[..., :QK_NOPE].reshape(KV_LORA, MLA_HEADS * QK_NOPE).astype(BF16)
    wv = wkvb[..., QK_NOPE:].reshape(KV_LORA, MLA_HEADS * V_HEAD).astype(BF16)
    qn = q_norm_w[l].astype(F32)
    kn = k_norm_w[l].astype(F32)
    qnn = qn[:QK_NOPE].reshape(1, 128)
    knn = kn[:QK_NOPE].reshape(1, 128)
    qrw = jnp.concatenate([qn[QK_NOPE:], _swap_halves(qn[QK_NOPE:])]).reshape(1, 128)
    krw = jnp.concatenate([kn[QK_NOPE:], _swap_halves(kn[QK_NOPE:])]).reshape(1, 128)

    half = QK_ROPE // 2
    inv_freq = ROPE_THETA ** (-jnp.arange(half, dtype=F32) / half)
    pos = (jnp.arange(L_PAD, dtype=jnp.int32) - META_PAD).astype(F32)
    ang = pos[:, None] * inv_freq[None, :]
    cs = jnp.concatenate([jnp.cos(ang), jnp.cos(ang), jnp.sin(ang), jnp.sin(ang)], axis=1)

    nw1 = attn_norm_w[l].astype(F32).reshape(1, D_MODEL)
    latm, dnm, abm = _inproj(hp_meta, nw1, w1, META_ROWS)
    latx, dnx, abx = _inproj(x.reshape(nb * SEQ, D_MODEL), nw1, w1, 512)
    latx = latx.reshape(nb, SEQ, N_LAT)
    dnx = dnx.reshape(nb, SEQ, N_DN)
    abx = abx.reshape(nb, SEQ, N_AB)

    yax, yam = _mla(latx, latm, cs, q_a_norm_w[l].astype(F32).reshape(1, Q_LORA), wq,
                    kv_a_norm_w[l].astype(F32).reshape(1, KV_LORA), wk, wv,
                    qnn, qrw, knn, krw, mla_out_norm_w[l].astype(F32).reshape(1, V_HEAD))
    ydx, ydm = _deltanet(dnx, dnm, abx, abm, dn_conv_w[l].astype(F32),
                         _pad_lanes(dn_A_log[l]), _pad_lanes(dn_dt_bias[l]),
                         dn_out_norm_w[l].astype(F32).reshape(1, DN_DIM))
    return _ffn(x, hp_meta, yax, yam, ydx, ydm, w_out[l].astype(BF16),
                ffn_norm_w[l].astype(F32).reshape(1, D_MODEL), w_gate[l].astype(BF16),
                w_up[l].astype(BF16), ffn_conv_w[l].astype(F32),
                ffn_conv_b[l].astype(F32).reshape(1, D_FF), w_down[l].astype(BF16))


def kernel(x, meta_tokens, attn_norm_w, w_in, q_a_norm_w, w_q_b, kv_a_norm_w, w_kv_b, q_norm_w, k_norm_w, mla_out_norm_w, dn_conv_w, dn_A_log, dn_dt_bias, dn_out_norm_w, w_out, ffn_norm_w, w_gate, w_up, ffn_conv_w, ffn_conv_b, w_down):
    assert x.shape[1:] == (SEQ, D_MODEL) and w_in.shape[0] == 1
    hp_meta = jnp.concatenate([jnp.zeros((META_PAD, D_MODEL), x.dtype), meta_tokens.astype(x.dtype)], axis=0)
    return _layer(x, hp_meta, 0, attn_norm_w, w_in, q_a_norm_w, w_q_b, kv_a_norm_w, w_kv_b, q_norm_w,
                  k_norm_w, mla_out_norm_w, dn_conv_w, dn_A_log, dn_dt_bias, dn_out_norm_w, w_out,
                  ffn_norm_w, w_gate, w_up, ffn_conv_w, ffn_conv_b, w_down)
```

```python
import math

import jax
import jax.numpy as jnp
from jax import lax
from jax.experimental import pallas as pl
from jax.experimental.pallas import tpu as pltpu

F32 = jnp.float32
BF16 = jnp.bfloat16

D_MODEL = 1024
SEQ = 2048
N_META = 16
META_ROWS = 64
META_PAD = META_ROWS - N_META

MLA_HEADS = 4
QK_NOPE = 128
QK_ROPE = 64
QK_HEAD = QK_NOPE + QK_ROPE
V_HEAD = 128
Q_LORA = 256
KV_LORA = 256
ROPE_THETA = 10000.0
HEAD_SLOT = 256

DN_HEADS = 4
DN_DIM = 128
DN_WIDTH = DN_HEADS * DN_DIM
DN_CHUNK = 64
DN_STACK = DN_HEADS * DN_CHUNK
DN_CONV = 4

D_FF = 2816
FF_BLK = 256
NORM_EPS = 1e-6

C_LAT = 0
N_LAT = Q_LORA + KV_LORA + 2 * QK_ROPE
C_DN = N_LAT
N_DN = 4 * DN_WIDTH
C_AB = C_DN + N_DN
N_AB = 128
N_PROJ = C_AB + N_AB

VMEM_LIMIT = 56 * 1024 * 1024


def _cparams(sem):
    return pltpu.CompilerParams(dimension_semantics=sem, vmem_limit_bytes=VMEM_LIMIT)


def _rms(x, w):
    return x * lax.rsqrt(jnp.mean(x * x, axis=-1, keepdims=True) + NORM_EPS) * w


def _dot(a, b):
    return jnp.dot(a, b, preferred_element_type=F32)


def _dot_nt(a, b):
    return lax.dot_general(a, b, (((1,), (1,)), ((), ())), preferred_element_type=F32)


def _dot_tn(a, b):
    return lax.dot_general(a, b, (((0,), (0,)), ((), ())), preferred_element_type=F32)


def _silu(x):
    return x * jax.nn.sigmoid(x)


def _inproj_kernel(x_ref, nw_ref, w_ref, lat_ref, dn_ref, ab_ref):
    u = _rms(x_ref[...], nw_ref[...]).astype(BF16)
    lat_ref[...] = _dot(u, w_ref[:, C_LAT:C_LAT + N_LAT]).astype(BF16)
    dn_ref[...] = _dot(u, w_ref[:, C_DN:C_DN + N_DN]).astype(BF16)
    ab_ref[...] = _dot(u, w_ref[:, C_AB:C_AB + N_AB])


def _inproj(x2d, nw, w, row_tile):
    rows = x2d.shape[0]
    grid = (rows // row_tile,)
    return pl.pallas_call(
        _inproj_kernel,
        grid=grid,
        in_specs=[
            pl.BlockSpec((row_tile, D_MODEL), lambda i: (i, 0)),
            pl.BlockSpec((1, D_MODEL), lambda i: (0, 0)),
            pl.BlockSpec((D_MODEL, N_PROJ), lambda i: (0, 0)),
        ],
        out_specs=[
            pl.BlockSpec((row_tile, N_LAT), lambda i: (i, 0)),
            pl.BlockSpec((row_tile, N_DN), lambda i: (i, 0)),
            pl.BlockSpec((row_tile, N_AB), lambda i: (i, 0)),
        ],
        out_shape=[
            jax.ShapeDtypeStruct((rows, N_LAT), BF16),
            jax.ShapeDtypeStruct((rows, N_DN), BF16),
            jax.ShapeDtypeStruct((rows, N_AB), F32),
        ],
        compiler_params=_cparams(("arbitrary",)),
        name="inproj",
    )(x2d, nw, w)


ATT_TQ = 256
ATT_PROJ_ROWS = 512
ATT_META = 2 * META_ROWS
ATT_META_VALID = ATT_META - N_META
ATT_ROWS = ATT_META + SEQ
NEG_INF = float("-inf")


def _mla_kernel(latx_ref, latm_ref, cs_ref, cst_ref, qaw_ref, wqt_ref, kvaw_ref, wk_ref, wvt_ref,
                qnn_ref, qrw_ref, knn_ref, krw_ref, onw_ref,
                yx_ref, ym_ref, km_s, qm_s, vm_s, kx_s, qx_s, vx_s, acc_s, st_s, p_s):
    low = lax.broadcasted_iota(jnp.int32, (1, 128), 1) < QK_ROPE
    scale = 1.0 / math.sqrt(QK_HEAD)

    def project(lat, cs, cst):
        nrows = lat.shape[0]
        qn = _rms(lat[:, 0:Q_LORA], qaw_ref[...])
        kvn = _rms(lat[:, Q_LORA:Q_LORA + KV_LORA], kvaw_ref[...])
        pe = lat[:, Q_LORA + KV_LORA:N_LAT]
        qt = _dot(wqt_ref[...], qn.T.astype(BF16))
        vt = _dot(wvt_ref[...], kvn.T.astype(BF16)).astype(BF16)
        kn = _dot(kvn.astype(BF16), wk_ref[...])
        a = pe * (cs * krw_ref[...])
        k_rope = jnp.where(low, a + pltpu.roll(a, QK_ROPE, 1), 0.0)
        pe_ss = jnp.sum(jnp.where(low, pe * pe, 0.0), axis=-1, keepdims=True)
        cos_t = cst[0:QK_ROPE, :]
        sin_t = cst[QK_ROPE:2 * QK_ROPE, :]
        k_parts = []
        q_parts = []
        for h in range(MLA_HEADS):
            nope = kn[:, h * QK_NOPE:(h + 1) * QK_NOPE]
            rs = lax.rsqrt((jnp.sum(nope * nope, axis=-1, keepdims=True) + pe_ss) * (1.0 / QK_HEAD) + NORM_EPS)
            k_parts += [(nope * rs * knn_ref[...]).astype(BF16), (k_rope * rs).astype(BF16)]
            r0 = h * HEAD_SLOT
            qnope = qt[r0:r0 + QK_NOPE, :]
            qrope = qt[r0 + QK_NOPE:r0 + QK_HEAD, :]
            qrot = qt[r0 + QK_HEAD:r0 + HEAD_SLOT, :]
            ssq = (jnp.sum(qnope * qnope, axis=0, keepdims=True)
                   + jnp.sum(qrope * qrope, axis=0, keepdims=True))
            rsq = lax.rsqrt(ssq * (1.0 / QK_HEAD) + NORM_EPS) * scale
            roped = (qrope * (qrw_ref[0:QK_ROPE, :] * cos_t)
                     + qrot * (qrw_ref[QK_ROPE:2 * QK_ROPE, :] * sin_t))
            q_parts += [(qnope * qnn_ref[...] * rsq).astype(BF16), (roped * rsq).astype(BF16),
                        jnp.zeros((HEAD_SLOT - QK_HEAD, nrows), BF16)]
        return jnp.concatenate(k_parts, axis=1), jnp.concatenate(q_parts, axis=0), vt

    latm = jnp.concatenate([jnp.zeros((META_ROWS, N_LAT), F32), latm_ref[...].astype(F32)], axis=0)
    km_s[...], qm_s[...], vm_s[...] = project(latm, cs_ref[0:ATT_META, :], cst_ref[:, 0:ATT_META])
    for c in range(SEQ // ATT_PROJ_ROWS):
        r0 = c * ATT_PROJ_ROWS
        k, qt, vt = project(latx_ref[0, r0:r0 + ATT_PROJ_ROWS, :].astype(F32),
                            cs_ref[ATT_META + r0:ATT_META + r0 + ATT_PROJ_ROWS, :],
                            cst_ref[:, ATT_META + r0:ATT_META + r0 + ATT_PROJ_ROWS])
        kx_s[r0:r0 + ATT_PROJ_ROWS, :] = k
        for t in range(ATT_PROJ_ROWS // ATT_TQ):
            qx_s[c * (ATT_PROJ_ROWS // ATT_TQ) + t] = qt[:, t * ATT_TQ:(t + 1) * ATT_TQ]
            vx_s[c * (ATT_PROJ_ROWS // ATT_TQ) + t] = vt[:, t * ATT_TQ:(t + 1) * ATT_TQ]

    def first_block(st):
        m = jnp.max(st, axis=0, keepdims=True)
        p = jnp.exp(st - m)
        return p.astype(BF16), m, jnp.sum(p, axis=0, keepdims=True)

    def next_block(st, m, l):
        m_new = jnp.maximum(m, jnp.max(st, axis=0, keepdims=True))
        alpha = jnp.exp(m - m_new)
        p = jnp.exp(st - m_new)
        return p.astype(BF16), m_new, alpha * l + jnp.sum(p, axis=0, keepdims=True), alpha

    def finish(acc_t, l):
        o = acc_t / l
        o = o * lax.rsqrt(jnp.mean(o * o, axis=0, keepdims=True) + NORM_EPS) * onw_ref[...]
        return o.T.astype(BF16)

    hs = lambda h: slice(h * HEAD_SLOT, (h + 1) * HEAD_SLOT)
    vs = lambda h: slice(h * V_HEAD, (h + 1) * V_HEAD)

    mkey = lax.broadcasted_iota(jnp.int32, (ATT_META, ATT_META), 0)
    mqry = lax.broadcasted_iota(jnp.int32, (ATT_META, ATT_META), 1)
    meta_mask = (mkey <= mqry) & ((mkey >= ATT_META_VALID) | (mkey == mqry))
    for h in range(MLA_HEADS):
        st = jnp.where(meta_mask, _dot(km_s[:, hs(h)], qm_s[hs(h), :]), NEG_INF)
        p, m, l = first_block(st)
        o = finish(_dot(vm_s[vs(h), :], p), l)
        ym_ref[0, :, vs(h)] = o[META_ROWS:, :]

    meta_key_ok = lax.broadcasted_iota(jnp.int32, (ATT_META, ATT_TQ), 0) >= ATT_META_VALID
    diag_mask = (lax.broadcasted_iota(jnp.int32, (ATT_TQ, ATT_TQ), 0)
                 <= lax.broadcasted_iota(jnp.int32, (ATT_TQ, ATT_TQ), 1))

    def q_tile(i, carry):
        def step(k_of, v_of, nk, mask, ms, ls):
            for h in range(MLA_HEADS):
                st_s[h, 0:nk, :] = _dot(k_of(h), qx_s[i, hs(h), :])
            ms2, ls2, alphas = [], [], []
            for h in range(MLA_HEADS):
                st = st_s[h, 0:nk, :]
                if mask is not None:
                    st = jnp.where(mask, st, NEG_INF)
                p, m, l, alpha = next_block(st, ms[h], ls[h])
                p_s[h, 0:nk, :] = p
                ms2.append(m)
                ls2.append(l)
                alphas.append(alpha)
            for h in range(MLA_HEADS):
                acc_s[h] = acc_s[h] * alphas[h] + _dot(v_of(h), p_s[h, 0:nk, :])
            return tuple(ms2), tuple(ls2)

        acc_s[...] = jnp.zeros_like(acc_s)
        ms = (jnp.full((1, ATT_TQ), NEG_INF, F32),) * MLA_HEADS
        ls = (jnp.zeros((1, ATT_TQ), F32),) * MLA_HEADS
        ms, ls = step(lambda h: km_s[:, hs(h)], lambda h: vm_s[vs(h), :], ATT_META, meta_key_ok, ms, ls)

        def kv_block(j, st_):
            k0 = pl.multiple_of(j * ATT_TQ, ATT_TQ)
            return step(lambda h: kx_s[pl.ds(k0, ATT_TQ), hs(h)], lambda h: vx_s[j, vs(h), :],
                        ATT_TQ, None, *st_)

        ms, ls = lax.fori_loop(0, i, kv_block, (ms, ls))
        q0 = pl.multiple_of(i * ATT_TQ, ATT_TQ)
        ms, ls = step(lambda h: kx_s[pl.ds(q0, ATT_TQ), hs(h)], lambda h: vx_s[i, vs(h), :],
                      ATT_TQ, diag_mask, ms, ls)
        for h in range(MLA_HEADS):
            yx_ref[0, pl.ds(q0, ATT_TQ), vs(h)] = finish(acc_s[h], ls[h])
        return carry

    lax.fori_loop(0, SEQ // ATT_TQ, q_tile, 0)


def _mla(latx, latm, cs, cst, qaw, wqt, kvaw, wk, wvt, qnn, qrw, knn, krw, onw):
    nb = latx.shape[0]
    const = lambda shape: pl.BlockSpec(shape, lambda b: (0,) * len(shape))
    nq = SEQ // ATT_TQ
    return pl.pallas_call(
        _mla_kernel,
        grid=(nb,),
        in_specs=[
            pl.BlockSpec((1, SEQ, N_LAT), lambda b: (b, 0, 0)),
            const((META_ROWS, N_LAT)),
            const((ATT_ROWS, 128)),
            const((128, ATT_ROWS)),
            const((1, Q_LORA)),
            const((MLA_HEADS * HEAD_SLOT, Q_LORA)),
            const((1, KV_LORA)),
            const((KV_LORA, MLA_HEADS * QK_NOPE)),
            const((MLA_HEADS * V_HEAD, KV_LORA)),
            const((QK_NOPE, 1)), const((2 * QK_ROPE, 1)), const((1, 128)), const((1, 128)),
            const((V_HEAD, 1)),
        ],
        out_specs=[
            pl.BlockSpec((1, SEQ, MLA_HEADS * V_HEAD), lambda b: (b, 0, 0)),
            pl.BlockSpec((1, META_ROWS, MLA_HEADS * V_HEAD), lambda b: (b, 0, 0)),
        ],
        out_shape=[
            jax.ShapeDtypeStruct((nb, SEQ, MLA_HEADS * V_HEAD), BF16),
            jax.ShapeDtypeStruct((nb, META_ROWS, MLA_HEADS * V_HEAD), BF16),
        ],
        scratch_shapes=[
            pltpu.VMEM((ATT_META, MLA_HEADS * HEAD_SLOT), BF16),
            pltpu.VMEM((MLA_HEADS * HEAD_SLOT, ATT_META), BF16),
            pltpu.VMEM((MLA_HEADS * V_HEAD, ATT_META), BF16),
            pltpu.VMEM((SEQ, MLA_HEADS * HEAD_SLOT), BF16),
            pltpu.VMEM((nq, MLA_HEADS * HEAD_SLOT, ATT_TQ), BF16),
            pltpu.VMEM((nq, MLA_HEADS * V_HEAD, ATT_TQ), BF16),
            pltpu.VMEM((MLA_HEADS, V_HEAD, ATT_TQ), F32),
            pltpu.VMEM((MLA_HEADS, ATT_TQ, ATT_TQ), F32),
            pltpu.VMEM((MLA_HEADS, ATT_TQ, ATT_TQ), BF16),
        ],
        compiler_params=_cparams(("arbitrary",)),
        name="mla",
    )(latx, latm, cs, cst, qaw, wqt, kvaw, wk, wvt, qnn, qrw, knn, krw, onw)


DN_GROUP = 4
DN_GROUP_ROWS = DN_GROUP * DN_CHUNK
DN_HIST = 16


def _deltanet_kernel(dnx_ref, dnm_ref, abx_ref, abm_ref, cw_ref, alog_ref, dtb_ref, onw_ref,
                     yx_ref, ym_ref, s_s, cbuf, af_s, t_s, pa_s, pb_s, rhs_s,
                     uw0_s, qk0_s, qg0_s, kd0_s, el0_s, uw1_s, qk1_s, qg1_s, kd1_s, el1_s):
    C = DN_CHUNK
    R = DN_STACK
    row = lax.broadcasted_iota(jnp.int32, (R, R), 0)
    col = lax.broadcasted_iota(jnp.int32, (R, R), 1)
    same = lambda sh: jnp.right_shift(row, sh) == jnp.right_shift(col, sh)
    m_incl = same(6) & (col <= row)
    m_strict = same(6) & (col < row)
    m_d16 = m_strict & same(4)
    m_l32 = m_strict & same(5) & jnp.logical_not(same(4))
    m_l64 = m_strict & jnp.logical_not(same(5))
    eye = (row == col).astype(F32)
    neg_a = -jnp.exp(alog_ref[...])

    s_s[...] = jnp.zeros_like(s_s)

    def stack(x):
        return jnp.concatenate([x[:, h * DN_DIM:(h + 1) * DN_DIM] for h in range(DN_HEADS)], axis=0)

    def stack_col(x, c0):
        return jnp.concatenate(
            [jnp.broadcast_to(x[:, c0 + h:c0 + h + 1], (C, DN_DIM)) for h in range(DN_HEADS)], axis=0)

    def bdot(a, b):
        return _dot(a.astype(BF16), b.astype(BF16))

    def conv_act(pre, hist):
        n = pre.shape[0]
        cbuf[0:DN_HIST, :] = hist
        cbuf[DN_HIST:DN_HIST + n, :] = pre
        conv = cw_ref[3:4, :] * pre
        for s in range(1, DN_CONV):
            conv = conv + cw_ref[3 - s:4 - s, :] * cbuf[DN_HIST - s:DN_HIST - s + n, :]
        return _silu(conv)

    def gates(ab, row_ok):
        xa = ab + dtb_ref[...]
        softplus = jnp.maximum(xa, 0.0) + jnp.log(1.0 + jnp.exp(-jnp.abs(xa)))
        g = neg_a * softplus
        beta = jax.nn.sigmoid(ab)
        if row_ok is not None:
            g = jnp.where(row_ok, g, 0.0)
            beta = jnp.where(row_ok, beta, 0.0)
        pos = lax.broadcasted_iota(jnp.int32, g.shape, 0) & (C - 1)
        gc = g
        for s in (1, 2, 4, 8, 16, 32):
            gc = gc + jnp.where(pos >= s, pltpu.roll(gc, s, 0), 0.0)
        return gc, beta

    def phase_a(acts, gcl, betal, buf):
        uw_b, qk_b, qg_b, kd_b, el_b = buf
        n = len(acts)
        for c in range(n):
            act, gc, beta = acts[c], gcl[c], betal[c]
            q = stack(act[:, 0:DN_WIDTH])
            k = stack(act[:, DN_WIDTH:2 * DN_WIDTH])
            v = stack(act[:, 2 * DN_WIDTH:3 * DN_WIDTH])
            q = q * lax.rsqrt(jnp.sum(q * q, axis=-1, keepdims=True) + NORM_EPS) * (1.0 / math.sqrt(DN_DIM))
            k = k * lax.rsqrt(jnp.sum(k * k, axis=-1, keepdims=True) + NORM_EPS)
            gcs = stack_col(gc, 0)
            gls = stack_col(jnp.broadcast_to(gc[C - 1:C, :], (C, 128)), 0)
            bs = stack_col(beta, DN_HEADS)
            grow = gcs.T[0:1, :]
            dec = jnp.exp(jnp.where(m_incl, gcs[:, 0:1] - grow, NEG_INF))
            kb = k * bs
            kbf = k.astype(BF16)
            eg = jnp.exp(gcs)
            af_s[c] = _dot_nt(kb.astype(BF16), kbf) * dec
            qk_b[c] = jnp.where(m_incl, _dot_nt(q.astype(BF16), kbf) * dec, 0.0).astype(BF16)
            rhs_s[c] = jnp.concatenate([v * bs, kb * eg], axis=1).astype(BF16)
            qg_b[c] = (q * eg).astype(BF16)
            kd_b[c] = (k * jnp.exp(gls - gcs)).astype(BF16)
            el_b[c] = jnp.exp(gls)
            yield

        for c in range(n):
            b1 = jnp.where(m_d16, af_s[c], 0.0)
            pa_s[c] = b1.astype(BF16)
            t_s[c] = eye - b1
        for c in range(n):
            pb_s[c] = _dot(pa_s[c], pa_s[c]).astype(BF16)
        yield
        for c in range(n):
            t_s[c] = t_s[c] + _dot(t_s[c].astype(BF16), pb_s[c])
            pa_s[c] = _dot(pb_s[c], pb_s[c]).astype(BF16)
        yield
        for c in range(n):
            t_s[c] = t_s[c] + _dot(t_s[c].astype(BF16), pa_s[c])
            pb_s[c] = _dot(pa_s[c], pa_s[c]).astype(BF16)
        yield
        for c in range(n):
            t_s[c] = t_s[c] + _dot(t_s[c].astype(BF16), pb_s[c])
        yield
        for m_low in (m_l32, m_l64):
            for c in range(n):
                pa_s[c] = _dot(jnp.where(m_low, af_s[c], 0.0).astype(BF16), t_s[c].astype(BF16)).astype(BF16)
            yield
            for c in range(n):
                t_s[c] = t_s[c] - _dot(t_s[c].astype(BF16), pa_s[c])
            yield
        for c in range(n):
            uw_b[c] = _dot(t_s[c].astype(BF16), rhs_s[c])

    def phase_b(zs, buf, outs):
        uw_b, qk_b, qg_b, kd_b, el_b = buf
        for c in range(len(zs)):
            vnew = []
            o_inter = []
            for h in range(DN_HEADS):
                r0 = h * C
                s_h = s_s[h]
                sb = s_h.astype(BF16)
                vn = uw_b[c, r0:r0 + C, 0:DN_DIM] - _dot(uw_b[c, r0:r0 + C, DN_DIM:].astype(BF16), sb)
                o_inter.append(_dot(qg_b[c, r0:r0 + C, :], sb))
                s_s[h] = s_h * el_b[c, r0:r0 + 1, :] + _dot_tn(kd_b[c, r0:r0 + C, :], vn.astype(BF16))
                vnew.append(vn)
            o = jnp.concatenate(o_inter, axis=0) + _dot(qk_b[c], jnp.concatenate(vnew, axis=0).astype(BF16))
            o = _rms(o, onw_ref[...])
            out = jnp.concatenate([o[h * C:(h + 1) * C, :] for h in range(DN_HEADS)], axis=1) * _silu(zs[c])
            outs.append(out.astype(BF16))
            yield

    def run(*gens_and_steps):
        gens = [g for g, _ in gens_and_steps]
        lens = [s for _, s in gens_and_steps]
        done = [0] * len(gens)
        alive = [True] * len(gens)
        while any(alive):
            i = min((j for j in range(len(gens)) if alive[j]), key=lambda j: (done[j] + 0.5) / lens[j])
            try:
                next(gens[i])
                done[i] += 1
            except StopIteration:
                alive[i] = False

    bufs = ((uw0_s, qk0_s, qg0_s, kd0_s, el0_s), (uw1_s, qk1_s, qg1_s, kd1_s, el1_s))

    meta_ok = lax.broadcasted_iota(jnp.int32, (C, 128), 0) >= META_PAD
    dn0 = dnm_ref[...]
    act0 = conv_act(dn0[:, 0:3 * DN_WIDTH].astype(F32), jnp.zeros((DN_HIST, 3 * DN_WIDTH), F32))
    gc0, beta0 = gates(abm_ref[...], meta_ok)
    run((phase_a([act0], [gc0], [beta0], bufs[1]), 1))
    out0 = []
    run((phase_b([dn0[:, 3 * DN_WIDTH:].astype(F32)], bufs[1], out0), 1))
    ym_ref[0] = out0[0]

    rows = [slice(c * C, (c + 1) * C) for c in range(DN_GROUP)]
    a_steps = DN_GROUP + 10
    b_steps = DN_GROUP + 1

    def group_a(p, buf):
        r0 = p * DN_GROUP_ROWS
        if isinstance(p, int) and p == 0:
            hist = dnm_ref[META_ROWS - DN_HIST:META_ROWS, 0:3 * DN_WIDTH]
        else:
            hist = dnx_ref[0, pl.ds(pl.multiple_of(r0 - DN_HIST, DN_HIST), DN_HIST), 0:3 * DN_WIDTH]
        if not isinstance(p, int):
            r0 = pl.multiple_of(r0, DN_GROUP_ROWS)
        act = conv_act(dnx_ref[0, pl.ds(r0, DN_GROUP_ROWS), 0:3 * DN_WIDTH].astype(F32), hist.astype(F32))
        gc, beta = gates(abx_ref[0, pl.ds(r0, DN_GROUP_ROWS), :], None)
        yield
        yield from phase_a([act[r, :] for r in rows], [gc[r, :] for r in rows], [beta[r, :] for r in rows], buf)

    def group_b(p, buf):
        r0 = p * DN_GROUP_ROWS
        if not isinstance(p, int):
            r0 = pl.multiple_of(r0, DN_GROUP_ROWS)
        z = dnx_ref[0, pl.ds(r0, DN_GROUP_ROWS), 3 * DN_WIDTH:].astype(F32)
        outs = []
        yield from phase_b([z[r, :] for r in rows], buf, outs)
        yx_ref[0, pl.ds(r0, DN_GROUP_ROWS), :] = jnp.concatenate(outs, axis=0)

    n_groups = SEQ // DN_GROUP_ROWS
    run((group_a(0, bufs[0]), a_steps))

    def body(kk, carry):
        p = 2 * kk
        run((group_a(p + 1, bufs[1]), a_steps), (group_b(p, bufs[0]), b_steps))
        run((group_a(p + 2, bufs[0]), a_steps), (group_b(p + 1, bufs[1]), b_steps))
        return carry

    lax.fori_loop(0, n_groups // 2 - 1, body, 0)
    run((group_a(n_groups - 1, bufs[1]), a_steps), (group_b(n_groups - 2, bufs[0]), b_steps))
    run((group_b(n_groups - 1, bufs[1]), b_steps))


def _deltanet(dnx, dnm, abx, abm, cw, alog, dtb, onw):
    nb = dnx.shape[0]
    const = lambda shape: pl.BlockSpec(shape, lambda b: (0,) * len(shape))
    return pl.pallas_call(
        _deltanet_kernel,
        grid=(nb,),
        in_specs=[
            pl.BlockSpec((1, SEQ, N_DN), lambda b: (b, 0, 0)),
            const((META_ROWS, N_DN)),
            pl.BlockSpec((1, SEQ, N_AB), lambda b: (b, 0, 0)),
            const((META_ROWS, N_AB)),
            const((DN_CONV, 3 * DN_WIDTH)),
            const((1, 128)), const((1, 128)), const((1, 128)),
        ],
        out_specs=[
            pl.BlockSpec((1, SEQ, DN_WIDTH), lambda b: (b, 0, 0)),
            pl.BlockSpec((1, META_ROWS, DN_WIDTH), lambda b: (b, 0, 0)),
        ],
        out_shape=[
            jax.ShapeDtypeStruct((nb, SEQ, DN_WIDTH), BF16),
            jax.ShapeDtypeStruct((nb, META_ROWS, DN_WIDTH), BF16),
        ],
        scratch_shapes=[
            pltpu.VMEM((DN_HEADS, DN_DIM, DN_DIM), F32),
            pltpu.VMEM((DN_HIST + DN_GROUP_ROWS, 3 * DN_WIDTH), F32),
            pltpu.VMEM((DN_GROUP, DN_STACK, DN_STACK), F32),
            pltpu.VMEM((DN_GROUP, DN_STACK, DN_STACK), F32),
            pltpu.VMEM((DN_GROUP, DN_STACK, DN_STACK), BF16),
            pltpu.VMEM((DN_GROUP, DN_STACK, DN_STACK), BF16),
            pltpu.VMEM((DN_GROUP, DN_STACK, 2 * DN_DIM), BF16),
        ] + 2 * [
            pltpu.VMEM((DN_GROUP, DN_STACK, 2 * DN_DIM), F32),
            pltpu.VMEM((DN_GROUP, DN_STACK, DN_STACK), BF16),
            pltpu.VMEM((DN_GROUP, DN_STACK, DN_DIM), BF16),
            pltpu.VMEM((DN_GROUP, DN_STACK, DN_DIM), BF16),
            pltpu.VMEM((DN_GROUP, DN_STACK, DN_DIM), F32),
        ],
        compiler_params=_cparams(("arbitrary",)),
        name="deltanet",
    )(dnx, dnm, abx, abm, cw, alog, dtb, onw)


FFN_ROWS = 1024
FFN_HALO = 16


def _ffn_kernel(x_ref, xh_ref, mh_ref, ya_ref, yah_ref, yam_ref, yd_ref, ydh_ref, ydm_ref,
                wo_ref, nw_ref, wg_ref, wu_ref, cw_ref, cb_ref, wd_ref,
                o_ref, u_s, g_s):
    r = pl.program_id(1)
    f = pl.program_id(2)

    @pl.when(f == 0)
    def _():
        mixed = jnp.concatenate([ya_ref[0], yd_ref[0]], axis=1)
        h_mid = x_ref[0] + _dot(mixed, wo_ref[...])
        o_ref[0] = h_mid
        u_s[FFN_HALO:, :] = _rms(h_mid, nw_ref[...]).astype(BF16)
        first = r == 0
        mixed_h = jnp.concatenate([jnp.where(first, yam_ref[0], yah_ref[0]),
                                   jnp.where(first, ydm_ref[0], ydh_ref[0])], axis=1)
        h_halo = jnp.where(first, mh_ref[...], xh_ref[0]) + _dot(mixed_h, wo_ref[...])
        u_s[0:FFN_HALO, :] = _rms(h_halo, nw_ref[...]).astype(BF16)

    g_s[...] = _dot(u_s[...], wg_ref[...])
    up = _dot(u_s[FFN_HALO:, :], wu_ref[...])
    gate = (cw_ref[2:3, :] * g_s[FFN_HALO:, :]
            + cw_ref[1:2, :] * g_s[FFN_HALO - 1:FFN_HALO - 1 + FFN_ROWS, :]
            + cw_ref[0:1, :] * g_s[FFN_HALO - 2:FFN_HALO - 2 + FFN_ROWS, :]
            + cb_ref[...])
    act = (_silu(gate) * up).astype(BF16)
    o_ref[0] += _dot(act, wd_ref[...])


def _ffn(x, hp_meta, yax, yam, ydx, ydm, wo, nw, wg, wu, cw, cb, wd):
    nb = x.shape[0]
    nr = SEQ // FFN_ROWS
    hb = FFN_ROWS // FFN_HALO
    halo_idx = lambda b, r, f: (b, jnp.maximum(r * hb - 1, 0), 0)
    meta_idx = lambda b, r, f: (b, META_ROWS // FFN_HALO - 1, 0)
    main_idx = lambda b, r, f: (b, r, 0)
    const2 = lambda shape: pl.BlockSpec(shape, lambda b, r, f: (0, 0))
    return pl.pallas_call(
        _ffn_kernel,
        grid=(nb, nr, D_FF // FF_BLK),
        in_specs=[
            pl.BlockSpec((1, FFN_ROWS, D_MODEL), main_idx),
            pl.BlockSpec((1, FFN_HALO, D_MODEL), halo_idx),
            pl.BlockSpec((FFN_HALO, D_MODEL), lambda b, r, f: (META_ROWS // FFN_HALO - 1, 0)),
            pl.BlockSpec((1, FFN_ROWS, MLA_HEADS * V_HEAD), main_idx),
            pl.BlockSpec((1, FFN_HALO, MLA_HEADS * V_HEAD), halo_idx),
            pl.BlockSpec((1, FFN_HALO, MLA_HEADS * V_HEAD), meta_idx),
            pl.BlockSpec((1, FFN_ROWS, DN_WIDTH), main_idx),
            pl.BlockSpec((1, FFN_HALO, DN_WIDTH), halo_idx),
            pl.BlockSpec((1, FFN_HALO, DN_WIDTH), meta_idx),
            const2((D_MODEL, D_MODEL)),
            const2((1, D_MODEL)),
            pl.BlockSpec((D_MODEL, FF_BLK), lambda b, r, f: (0, f)),
            pl.BlockSpec((D_MODEL, FF_BLK), lambda b, r, f: (0, f)),
            pl.BlockSpec((3, FF_BLK), lambda b, r, f: (0, f)),
            pl.BlockSpec((1, FF_BLK), lambda b, r, f: (0, f)),
            pl.BlockSpec((FF_BLK, D_MODEL), lambda b, r, f: (f, 0)),
        ],
        out_specs=pl.BlockSpec((1, FFN_ROWS, D_MODEL), main_idx),
        out_shape=jax.ShapeDtypeStruct((nb, SEQ, D_MODEL), F32),
        scratch_shapes=[
            pltpu.VMEM((FFN_HALO + FFN_ROWS, D_MODEL), BF16),
            pltpu.VMEM((FFN_HALO + FFN_ROWS, FF_BLK), F32),
        ],
        compiler_params=_cparams(("arbitrary", "arbitrary", "arbitrary")),
        name="outproj_ffn",
    )(x, x, hp_meta, yax, yax, yam, ydx, ydx, ydm, wo, nw, wg, wu, cw, cb, wd)


def _rot_cols(w):
    half = QK_ROPE // 2
    return jnp.concatenate([-w[..., half:], w[..., :half]], axis=-1)


def _swap_halves(w):
    half = QK_ROPE // 2
    return jnp.concatenate([w[..., half:], w[..., :half]], axis=-1)


def _pad_lanes(v, n=128):
    return jnp.pad(v.astype(F32), (0, n - v.shape[0])).reshape(1, n)


def _layer(x, hp_meta, l, attn_norm_w, w_in, q_a_norm_w, w_q_b, kv_a_norm_w, w_kv_b, q_norm_w,
           k_norm_w, mla_out_norm_w, dn_conv_w, dn_A_log, dn_dt_bias, dn_out_norm_w, w_out,
           ffn_norm_w, w_gate, w_up, ffn_conv_w, ffn_conv_b, w_down):
    nb = x.shape[0]
    c1 = Q_LORA
    c2 = c1 + KV_LORA
    c3 = c2 + QK_ROPE
    c4 = c3 + 3 * DN_WIDTH
    c5 = c4 + DN_WIDTH
    win = w_in[l]
    k_pe_w = win[:, c2:c3]
    w1 = jnp.concatenate(
        [win[:, :c2], k_pe_w, _rot_cols(k_pe_w), win[:, c3:c5], win[:, c5:],
         jnp.zeros((D_MODEL, N_AB - 2 * DN_HEADS), F32)], axis=1).astype(BF16)

    wqb = w_q_b[l].reshape(Q_LORA, MLA_HEADS, QK_HEAD)
    wqt = jnp.concatenate([wqb[..., :QK_NOPE], wqb[..., QK_NOPE:], _rot_cols(wqb[..., QK_NOPE:])],
                          axis=-1).reshape(Q_LORA, MLA_HEADS * HEAD_SLOT).T.astype(BF16)
    wkvb = w_kv_b[l].reshape(KV_LORA, MLA_HEADS, QK_NOPE + V_HEAD)
    wk = wkvb[..., :QK_NOPE].reshape(KV_LORA, MLA_HEADS * QK_NOPE).astype(BF16)
    wvt = wkvb[..., QK_NOPE:].reshape(KV_LORA, MLA_HEADS * V_HEAD).T.astype(BF16)
    qn = q_norm_w[l].astype(F32)
    kn = k_norm_w[l].astype(F32)
    qnn = qn[:QK_NOPE].reshape(QK_NOPE, 1)
    knn = kn[:QK_NOPE].reshape(1, 128)
    qrw = jnp.concatenate([qn[QK_NOPE:], _swap_halves(qn[QK_NOPE:])]).reshape(2 * QK_ROPE, 1)
    krw = jnp.concatenate([kn[QK_NOPE:], _swap_halves(kn[QK_NOPE:])]).reshape(1, 128)

    half = QK_ROPE // 2
    inv_freq = ROPE_THETA ** (-jnp.arange(half, dtype=F32) / half)
    pos = (jnp.arange(ATT_ROWS, dtype=jnp.int32) - ATT_META_VALID).astype(F32)
    ang = pos[:, None] * inv_freq[None, :]
    cs = jnp.concatenate([jnp.cos(ang), jnp.cos(ang), jnp.sin(ang), jnp.sin(ang)], axis=1)
    cst = cs.T

    nw1 = attn_norm_w[l].astype(F32).reshape(1, D_MODEL)
    latm, dnm, abm = _inproj(hp_meta, nw1, w1, META_ROWS)
    latx, dnx, abx = _inproj(x.reshape(nb * SEQ, D_MODEL), nw1, w1, 512)
    latx = latx.reshape(nb, SEQ, N_LAT)
    dnx = dnx.reshape(nb, SEQ, N_DN)
    abx = abx.reshape(nb, SEQ, N_AB)

    yax, yam = _mla(latx, latm, cs, cst, q_a_norm_w[l].astype(F32).reshape(1, Q_LORA), wqt,
                    kv_a_norm_w[l].astype(F32).reshape(1, KV_LORA), wk, wvt,
                    qnn, qrw, knn, krw, mla_out_norm_w[l].astype(F32).reshape(V_HEAD, 1))
    ydx, ydm = _deltanet(dnx, dnm, abx, abm, dn_conv_w[l].astype(F32),
                         _pad_lanes(dn_A_log[l]), _pad_lanes(dn_dt_bias[l]),
                         dn_out_norm_w[l].astype(F32).reshape(1, DN_DIM))
    return _ffn(x, hp_meta, yax, yam, ydx, ydm, w_out[l].astype(BF16),
                ffn_norm_w[l].astype(F32).reshape(1, D_MODEL), w_gate[l].astype(BF16),
                w_up[l].astype(BF16), ffn_conv_w[l].astype(F32),
                ffn_conv_b[l].astype(F32).reshape(1, D_FF), w_down[l].astype(BF16))


def kernel(x, meta_tokens, attn_norm_w, w_in, q_a_norm_w, w_q_b, kv_a_norm_w, w_kv_b, q_norm_w, k_norm_w, mla_out_norm_w, dn_conv_w, dn_A_log, dn_dt_bias, dn_out_norm_w, w_out, ffn_norm_w, w_gate, w_up, ffn_conv_w, ffn_conv_b, w_down):
    assert x.shape[1:] == (SEQ, D_MODEL) and w_in.shape[0] == 1
    hp_meta = jnp.concatenate([jnp.zeros((META_PAD, D_MODEL), x.dtype), meta_tokens.astype(x.dtype)], axis=0)
    return _layer(x, hp_meta, 0, attn_norm_w, w_in, q_a_norm_w, w_q_b, kv_a_norm_w, w_kv_b, q_norm_w,
                  k_norm_w, mla_out_norm_w, dn_conv_w, dn_A_log, dn_dt_bias, dn_out_norm_w, w_out,
                  ffn_norm_w, w_gate, w_up, ffn_conv_w, ffn_conv_b, w_down)
```

```python
import math

import jax
import jax.numpy as jnp
from jax import lax
from jax.experimental import pallas as pl
from jax.experimental.pallas import tpu as pltpu

F32 = jnp.float32
BF16 = jnp.bfloat16

D_MODEL = 1024
SEQ = 2048
N_META = 16
META_ROWS = 64
META_PAD = META_ROWS - N_META

MLA_HEADS = 4
QK_NOPE = 128
QK_ROPE = 64
QK_HEAD = QK_NOPE + QK_ROPE
V_HEAD = 128
Q_LORA = 256
KV_LORA = 256
ROPE_THETA = 10000.0
HEAD_SLOT = 256

DN_HEADS = 4
DN_DIM = 128
DN_WIDTH = DN_HEADS * DN_DIM
DN_CHUNK = 64
DN_STACK = DN_HEADS * DN_CHUNK
DN_CONV = 4

D_FF = 2816
FF_BLK = 256
NORM_EPS = 1e-6

C_LAT = 0
N_LAT = Q_LORA + KV_LORA + 2 * QK_ROPE
C_DN = N_LAT
N_DN = 4 * DN_WIDTH
C_AB = C_DN + N_DN
N_AB = 128
N_PROJ = C_AB + N_AB

VMEM_LIMIT = 56 * 1024 * 1024


def _cparams(sem):
    return pltpu.CompilerParams(dimension_semantics=sem, vmem_limit_bytes=VMEM_LIMIT)


def _rms(x, w):
    return x * lax.rsqrt(jnp.mean(x * x, axis=-1, keepdims=True) + NORM_EPS) * w


def _dot(a, b):
    return jnp.dot(a, b, preferred_element_type=F32)


def _dot_nt(a, b):
    return lax.dot_general(a, b, (((1,), (1,)), ((), ())), preferred_element_type=F32)


def _dot_tn(a, b):
    return lax.dot_general(a, b, (((0,), (0,)), ((), ())), preferred_element_type=F32)


def _silu(x):
    return x * jax.nn.sigmoid(x)


def _inproj_kernel(x_ref, nw_ref, w_ref, lat_ref, dn_ref, ab_ref):
    u = _rms(x_ref[...], nw_ref[...]).astype(BF16)
    lat_ref[...] = _dot(u, w_ref[:, C_LAT:C_LAT + N_LAT]).astype(BF16)
    dn_ref[...] = _dot(u, w_ref[:, C_DN:C_DN + N_DN]).astype(BF16)
    ab_ref[...] = _dot(u, w_ref[:, C_AB:C_AB + N_AB])


def _inproj(x2d, nw, w, row_tile):
    rows = x2d.shape[0]
    grid = (rows // row_tile,)
    return pl.pallas_call(
        _inproj_kernel,
        grid=grid,
        in_specs=[
            pl.BlockSpec((row_tile, D_MODEL), lambda i: (i, 0)),
            pl.BlockSpec((1, D_MODEL), lambda i: (0, 0)),
            pl.BlockSpec((D_MODEL, N_PROJ), lambda i: (0, 0)),
        ],
        out_specs=[
            pl.BlockSpec((row_tile, N_LAT), lambda i: (i, 0)),
            pl.BlockSpec((row_tile, N_DN), lambda i: (i, 0)),
            pl.BlockSpec((row_tile, N_AB), lambda i: (i, 0)),
        ],
        out_shape=[
            jax.ShapeDtypeStruct((rows, N_LAT), BF16),
            jax.ShapeDtypeStruct((rows, N_DN), BF16),
            jax.ShapeDtypeStruct((rows, N_AB), F32),
        ],
        compiler_params=_cparams(("arbitrary",)),
        name="inproj",
    )(x2d, nw, w)


ATT_TQ = 256
ATT_PROJ_ROWS = 512
ATT_META = 2 * META_ROWS
ATT_META_VALID = ATT_META - N_META
ATT_ROWS = ATT_META + SEQ
NEG_INF = float("-inf")


def _mla_kernel(latx_ref, latm_ref, cs_ref, cst_ref, qaw_ref, wqt_ref, kvaw_ref, wk_ref, wvt_ref,
                qnn_ref, qrw_ref, knn_ref, krw_ref, onw_ref,
                yx_ref, ym_ref, km_s, qm_s, vm_s, kx_s, qx_s, vx_s, acc_s, stm_s, st0_s, st1_s, p_s):
    low = lax.broadcasted_iota(jnp.int32, (1, 128), 1) < QK_ROPE
    scale = 1.0 / math.sqrt(QK_HEAD)

    def project(lat, cs, cst):
        nrows = lat.shape[0]
        qn = _rms(lat[:, 0:Q_LORA], qaw_ref[...])
        kvn = _rms(lat[:, Q_LORA:Q_LORA + KV_LORA], kvaw_ref[...])
        pe = lat[:, Q_LORA + KV_LORA:N_LAT]
        qt = _dot(wqt_ref[...], qn.T.astype(BF16))
        vt = _dot(wvt_ref[...], kvn.T.astype(BF16)).astype(BF16)
        kn = _dot(kvn.astype(BF16), wk_ref[...])
        a = pe * (cs * krw_ref[...])
        k_rope = jnp.where(low, a + pltpu.roll(a, QK_ROPE, 1), 0.0)
        pe_ss = jnp.sum(jnp.where(low, pe * pe, 0.0), axis=-1, keepdims=True)
        cos_t = cst[0:QK_ROPE, :]
        sin_t = cst[QK_ROPE:2 * QK_ROPE, :]
        k_parts = []
        q_parts = []
        for h in range(MLA_HEADS):
            nope = kn[:, h * QK_NOPE:(h + 1) * QK_NOPE]
            rs = lax.rsqrt((jnp.sum(nope * nope, axis=-1, keepdims=True) + pe_ss) * (1.0 / QK_HEAD) + NORM_EPS)
            k_parts += [(nope * rs * knn_ref[...]).astype(BF16), (k_rope * rs).astype(BF16)]
            r0 = h * HEAD_SLOT
            qnope = qt[r0:r0 + QK_NOPE, :]
            qrope = qt[r0 + QK_NOPE:r0 + QK_HEAD, :]
            qrot = qt[r0 + QK_HEAD:r0 + HEAD_SLOT, :]
            ssq = (jnp.sum(qnope * qnope, axis=0, keepdims=True)
                   + jnp.sum(qrope * qrope, axis=0, keepdims=True))
            rsq = lax.rsqrt(ssq * (1.0 / QK_HEAD) + NORM_EPS) * scale
            roped = (qrope * (qrw_ref[0:QK_ROPE, :] * cos_t)
                     + qrot * (qrw_ref[QK_ROPE:2 * QK_ROPE, :] * sin_t))
            q_parts += [(qnope * qnn_ref[...] * rsq).astype(BF16), (roped * rsq).astype(BF16),
                        jnp.zeros((HEAD_SLOT - QK_HEAD, nrows), BF16)]
        return jnp.concatenate(k_parts, axis=1), jnp.concatenate(q_parts, axis=0), vt

    latm = jnp.concatenate([jnp.zeros((META_ROWS, N_LAT), F32), latm_ref[...].astype(F32)], axis=0)
    km_s[...], qm_s[...], vm_s[...] = project(latm, cs_ref[0:ATT_META, :], cst_ref[:, 0:ATT_META])
    for c in range(SEQ // ATT_PROJ_ROWS):
        r0 = c * ATT_PROJ_ROWS
        k, qt, vt = project(latx_ref[0, r0:r0 + ATT_PROJ_ROWS, :].astype(F32),
                            cs_ref[ATT_META + r0:ATT_META + r0 + ATT_PROJ_ROWS, :],
                            cst_ref[:, ATT_META + r0:ATT_META + r0 + ATT_PROJ_ROWS])
        kx_s[r0:r0 + ATT_PROJ_ROWS, :] = k
        for t in range(ATT_PROJ_ROWS // ATT_TQ):
            qx_s[c * (ATT_PROJ_ROWS // ATT_TQ) + t] = qt[:, t * ATT_TQ:(t + 1) * ATT_TQ]
            vx_s[c * (ATT_PROJ_ROWS // ATT_TQ) + t] = vt[:, t * ATT_TQ:(t + 1) * ATT_TQ]

    def first_block(st):
        m = jnp.max(st, axis=0, keepdims=True)
        p = jnp.exp(st - m)
        return p.astype(BF16), m, jnp.sum(p, axis=0, keepdims=True)

    def next_block(st, m, l):
        m_new = jnp.maximum(m, jnp.max(st, axis=0, keepdims=True))
        alpha = jnp.exp(m - m_new)
        p = jnp.exp(st - m_new)
        return p.astype(BF16), m_new, alpha * l + jnp.sum(p, axis=0, keepdims=True), alpha

    def finish(acc_t, l):
        o = acc_t / l
        o = o * lax.rsqrt(jnp.mean(o * o, axis=0, keepdims=True) + NORM_EPS) * onw_ref[...]
        return o.T.astype(BF16)

    hs = lambda h: slice(h * HEAD_SLOT, (h + 1) * HEAD_SLOT)
    vs = lambda h: slice(h * V_HEAD, (h + 1) * V_HEAD)

    mkey = lax.broadcasted_iota(jnp.int32, (ATT_META, ATT_META), 0)
    mqry = lax.broadcasted_iota(jnp.int32, (ATT_META, ATT_META), 1)
    meta_mask = (mkey <= mqry) & ((mkey >= ATT_META_VALID) | (mkey == mqry))
    for h in range(MLA_HEADS):
        st = jnp.where(meta_mask, _dot(km_s[:, hs(h)], qm_s[hs(h), :]), NEG_INF)
        p, m, l = first_block(st)
        o = finish(_dot(vm_s[vs(h), :], p), l)
        ym_ref[0, :, vs(h)] = o[META_ROWS:, :]

    meta_key_ok = lax.broadcasted_iota(jnp.int32, (ATT_META, ATT_TQ), 0) >= ATT_META_VALID
    diag_mask = (lax.broadcasted_iota(jnp.int32, (ATT_TQ, ATT_TQ), 0)
                 <= lax.broadcasted_iota(jnp.int32, (ATT_TQ, ATT_TQ), 1))

    st_bufs = (st0_s, st1_s)

    for i in range(SEQ // ATT_TQ):
        def scores(buf, k_of, nk):
            for h in range(MLA_HEADS):
                buf[h, 0:nk, :] = _dot(k_of(h), qx_s[i, hs(h), :])

        def absorb(buf, v_of, nk, mask, ms, ls):
            ms2, ls2, alphas = [], [], []
            for h in range(MLA_HEADS):
                st = buf[h, 0:nk, :]
                if mask is not None:
                    st = jnp.where(mask, st, NEG_INF)
                p, m, l, alpha = next_block(st, ms[h], ls[h])
                p_s[h, 0:nk, :] = p
                ms2.append(m)
                ls2.append(l)
                alphas.append(alpha)
            for h in range(MLA_HEADS):
                acc_s[h] = acc_s[h] * alphas[h] + _dot(v_of(h), p_s[h, 0:nk, :])
            return tuple(ms2), tuple(ls2)

        def k_blk(j):
            k0 = j * ATT_TQ
            if not isinstance(j, int):
                k0 = pl.multiple_of(k0, ATT_TQ)
            return lambda h: kx_s[pl.ds(k0, ATT_TQ), hs(h)]

        v_blk = lambda j: (lambda h: vx_s[j, vs(h), :])

        acc_s[...] = jnp.zeros_like(acc_s)
        ms = (jnp.full((1, ATT_TQ), NEG_INF, F32),) * MLA_HEADS
        ls = (jnp.zeros((1, ATT_TQ), F32),) * MLA_HEADS
        scores(stm_s, lambda h: km_s[:, hs(h)], ATT_META)
        scores(st_bufs[0], k_blk(0), ATT_TQ)
        ms, ls = absorb(stm_s, lambda h: vm_s[vs(h), :], ATT_META, meta_key_ok, ms, ls)

        def pair(t, st_):
            ms, ls = st_
            scores(st_bufs[1], k_blk(2 * t + 1), ATT_TQ)
            ms, ls = absorb(st_bufs[0], v_blk(2 * t), ATT_TQ, None, ms, ls)
            scores(st_bufs[0], k_blk(2 * t + 2), ATT_TQ)
            return absorb(st_bufs[1], v_blk(2 * t + 1), ATT_TQ, None, ms, ls)

        if i // 2 > 0:
            ms, ls = lax.fori_loop(0, i // 2, pair, (ms, ls))
        if i % 2 == 1:
            scores(st_bufs[1], k_blk(i), ATT_TQ)
            ms, ls = absorb(st_bufs[0], v_blk(i - 1), ATT_TQ, None, ms, ls)
        ms, ls = absorb(st_bufs[i % 2], v_blk(i), ATT_TQ, diag_mask, ms, ls)
        for h in range(MLA_HEADS):
            yx_ref[0, i * ATT_TQ:(i + 1) * ATT_TQ, vs(h)] = finish(acc_s[h], ls[h])


def _mla(latx, latm, cs, cst, qaw, wqt, kvaw, wk, wvt, qnn, qrw, knn, krw, onw):
    nb = latx.shape[0]
    const = lambda shape: pl.BlockSpec(shape, lambda b: (0,) * len(shape))
    nq = SEQ // ATT_TQ
    return pl.pallas_call(
        _mla_kernel,
        grid=(nb,),
        in_specs=[
            pl.BlockSpec((1, SEQ, N_LAT), lambda b: (b, 0, 0)),
            const((META_ROWS, N_LAT)),
            const((ATT_ROWS, 128)),
            const((128, ATT_ROWS)),
            const((1, Q_LORA)),
            const((MLA_HEADS * HEAD_SLOT, Q_LORA)),
            const((1, KV_LORA)),
            const((KV_LORA, MLA_HEADS * QK_NOPE)),
            const((MLA_HEADS * V_HEAD, KV_LORA)),
            const((QK_NOPE, 1)), const((2 * QK_ROPE, 1)), const((1, 128)), const((1, 128)),
            const((V_HEAD, 1)),
        ],
        out_specs=[
            pl.BlockSpec((1, SEQ, MLA_HEADS * V_HEAD), lambda b: (b, 0, 0)),
            pl.BlockSpec((1, META_ROWS, MLA_HEADS * V_HEAD), lambda b: (b, 0, 0)),
        ],
        out_shape=[
            jax.ShapeDtypeStruct((nb, SEQ, MLA_HEADS * V_HEAD), BF16),
            jax.ShapeDtypeStruct((nb, META_ROWS, MLA_HEADS * V_HEAD), BF16),
        ],
        scratch_shapes=[
            pltpu.VMEM((ATT_META, MLA_HEADS * HEAD_SLOT), BF16),
            pltpu.VMEM((MLA_HEADS * HEAD_SLOT, ATT_META), BF16),
            pltpu.VMEM((MLA_HEADS * V_HEAD, ATT_META), BF16),
            pltpu.VMEM((SEQ, MLA_HEADS * HEAD_SLOT), BF16),
            pltpu.VMEM((nq, MLA_HEADS * HEAD_SLOT, ATT_TQ), BF16),
            pltpu.VMEM((nq, MLA_HEADS * V_HEAD, ATT_TQ), BF16),
            pltpu.VMEM((MLA_HEADS, V_HEAD, ATT_TQ), F32),
            pltpu.VMEM((MLA_HEADS, ATT_META, ATT_TQ), F32),
            pltpu.VMEM((MLA_HEADS, ATT_TQ, ATT_TQ), F32),
            pltpu.VMEM((MLA_HEADS, ATT_TQ, ATT_TQ), F32),
            pltpu.VMEM((MLA_HEADS, ATT_TQ, ATT_TQ), BF16),
        ],
        compiler_params=_cparams(("arbitrary",)),
        name="mla",
    )(latx, latm, cs, cst, qaw, wqt, kvaw, wk, wvt, qnn, qrw, knn, krw, onw)


DN_GROUP = 4
DN_GROUP_ROWS = DN_GROUP * DN_CHUNK
DN_HIST = 16


def _deltanet_kernel(dnx_ref, dnm_ref, abx_ref, abm_ref, cw_ref, alog_ref, dtb_ref, onw_ref,
                     yx_ref, ym_ref, s_s, cbuf, af_s, t_s, pa_s, pb_s, rhs_s,
                     uw0_s, qk0_s, qg0_s, kd0_s, el0_s, uw1_s, qk1_s, qg1_s, kd1_s, el1_s):
    C = DN_CHUNK
    R = DN_STACK
    row = lax.broadcasted_iota(jnp.int32, (R, R), 0)
    col = lax.broadcasted_iota(jnp.int32, (R, R), 1)
    same = lambda sh: jnp.right_shift(row, sh) == jnp.right_shift(col, sh)
    m_incl = same(6) & (col <= row)
    m_strict = same(6) & (col < row)
    m_d16 = m_strict & same(4)
    m_l32 = m_strict & same(5) & jnp.logical_not(same(4))
    m_l64 = m_strict & jnp.logical_not(same(5))
    eye = (row == col).astype(F32)
    neg_a = -jnp.exp(alog_ref[...])

    s_s[...] = jnp.zeros_like(s_s)

    def stack(x):
        return jnp.concatenate([x[:, h * DN_DIM:(h + 1) * DN_DIM] for h in range(DN_HEADS)], axis=0)

    def stack_col(x, c0):
        return jnp.concatenate(
            [jnp.broadcast_to(x[:, c0 + h:c0 + h + 1], (C, DN_DIM)) for h in range(DN_HEADS)], axis=0)

    def bdot(a, b):
        return _dot(a.astype(BF16), b.astype(BF16))

    def conv_act(pre, hist):
        n = pre.shape[0]
        cbuf[0:DN_HIST, :] = hist
        cbuf[DN_HIST:DN_HIST + n, :] = pre
        conv = cw_ref[3:4, :] * pre
        for s in range(1, DN_CONV):
            conv = conv + cw_ref[3 - s:4 - s, :] * cbuf[DN_HIST - s:DN_HIST - s + n, :]
        return _silu(conv)

    def gates(ab, row_ok):
        xa = ab + dtb_ref[...]
        softplus = jnp.maximum(xa, 0.0) + jnp.log(1.0 + jnp.exp(-jnp.abs(xa)))
        g = neg_a * softplus
        beta = jax.nn.sigmoid(ab)
        if row_ok is not None:
            g = jnp.where(row_ok, g, 0.0)
            beta = jnp.where(row_ok, beta, 0.0)
        pos = lax.broadcasted_iota(jnp.int32, g.shape, 0) & (C - 1)
        gc = g
        for s in (1, 2, 4, 8, 16, 32):
            gc = gc + jnp.where(pos >= s, pltpu.roll(gc, s, 0), 0.0)
        return gc, beta

    def phase_a(acts, gcl, betal, buf):
        uw_b, qk_b, qg_b, kd_b, el_b = buf
        n = len(acts)
        for c in range(n):
            act, gc, beta = acts[c], gcl[c], betal[c]
            q = stack(act[:, 0:DN_WIDTH])
            k = stack(act[:, DN_WIDTH:2 * DN_WIDTH])
            v = stack(act[:, 2 * DN_WIDTH:3 * DN_WIDTH])
            q = q * lax.rsqrt(jnp.sum(q * q, axis=-1, keepdims=True) + NORM_EPS) * (1.0 / math.sqrt(DN_DIM))
            k = k * lax.rsqrt(jnp.sum(k * k, axis=-1, keepdims=True) + NORM_EPS)
            gcs = stack_col(gc, 0)
            gls = stack_col(jnp.broadcast_to(gc[C - 1:C, :], (C, 128)), 0)
            bs = stack_col(beta, DN_HEADS)
            grow = gcs.T[0:1, :]
            dec = jnp.exp(jnp.where(m_incl, gcs[:, 0:1] - grow, NEG_INF))
            kb = k * bs
            kbf = k.astype(BF16)
            eg = jnp.exp(gcs)
            af_s[c] = _dot_nt(kb.astype(BF16), kbf) * dec
            qk_b[c] = jnp.where(m_incl, _dot_nt(q.astype(BF16), kbf) * dec, 0.0).astype(BF16)
            rhs_s[c] = jnp.concatenate([v * bs, kb * eg], axis=1).astype(BF16)
            qg_b[c] = (q * eg).astype(BF16)
            kd_b[c] = (k * jnp.exp(gls - gcs)).astype(BF16)
            el_b[c] = jnp.exp(gls)
            yield

        for c in range(n):
            b1 = jnp.where(m_d16, af_s[c], 0.0)
            pa_s[c] = b1.astype(BF16)
            t_s[c] = eye - b1
        for c in range(n):
            pb_s[c] = _dot(pa_s[c], pa_s[c]).astype(BF16)
        yield
        for c in range(n):
            t_s[c] = t_s[c] + _dot(t_s[c].astype(BF16), pb_s[c])
            pa_s[c] = _dot(pb_s[c], pb_s[c]).astype(BF16)
        yield
        for c in range(n):
            t_s[c] = t_s[c] + _dot(t_s[c].astype(BF16), pa_s[c])
            pb_s[c] = _dot(pa_s[c], pa_s[c]).astype(BF16)
        yield
        for c in range(n):
            t_s[c] = t_s[c] + _dot(t_s[c].astype(BF16), pb_s[c])
        yield
        for m_low in (m_l32, m_l64):
            for c in range(n):
                pa_s[c] = _dot(jnp.where(m_low, af_s[c], 0.0).astype(BF16), t_s[c].astype(BF16)).astype(BF16)
            yield
            for c in range(n):
                t_s[c] = t_s[c] - _dot(t_s[c].astype(BF16), pa_s[c])
            yield
        for c in range(n):
            uw_b[c] = _dot(t_s[c].astype(BF16), rhs_s[c])

    def phase_b(zs, buf, outs):
        uw_b, qk_b, qg_b, kd_b, el_b = buf
        for c in range(len(zs)):
            vnew = []
            o_inter = []
            for h in range(DN_HEADS):
                r0 = h * C
                s_h = s_s[h]
                sb = s_h.astype(BF16)
                vn = uw_b[c, r0:r0 + C, 0:DN_DIM] - _dot(uw_b[c, r0:r0 + C, DN_DIM:].astype(BF16), sb)
                o_inter.append(_dot(qg_b[c, r0:r0 + C, :], sb))
                s_s[h] = s_h * el_b[c, r0:r0 + 1, :] + _dot_tn(kd_b[c, r0:r0 + C, :], vn.astype(BF16))
                vnew.append(vn)
            o = jnp.concatenate(o_inter, axis=0) + _dot(qk_b[c], jnp.concatenate(vnew, axis=0).astype(BF16))
            o = _rms(o, onw_ref[...])
            out = jnp.concatenate([o[h * C:(h + 1) * C, :] for h in range(DN_HEADS)], axis=1) * _silu(zs[c])
            outs.append(out.astype(BF16))
            yield

    def run(*gens_and_steps):
        gens = [g for g, _ in gens_and_steps]
        lens = [s for _, s in gens_and_steps]
        done = [0] * len(gens)
        alive = [True] * len(gens)
        while any(alive):
            i = min((j for j in range(len(gens)) if alive[j]), key=lambda j: (done[j] + 0.5) / lens[j])
            try:
                next(gens[i])
                done[i] += 1
            except StopIteration:
                alive[i] = False

    bufs = ((uw0_s, qk0_s, qg0_s, kd0_s, el0_s), (uw1_s, qk1_s, qg1_s, kd1_s, el1_s))

    meta_ok = lax.broadcasted_iota(jnp.int32, (C, 128), 0) >= META_PAD
    dn0 = dnm_ref[...]
    act0 = conv_act(dn0[:, 0:3 * DN_WIDTH].astype(F32), jnp.zeros((DN_HIST, 3 * DN_WIDTH), F32))
    gc0, beta0 = gates(abm_ref[...], meta_ok)
    run((phase_a([act0], [gc0], [beta0], bufs[1]), 1))
    out0 = []
    run((phase_b([dn0[:, 3 * DN_WIDTH:].astype(F32)], bufs[1], out0), 1))
    ym_ref[0] = out0[0]

    rows = [slice(c * C, (c + 1) * C) for c in range(DN_GROUP)]
    a_steps = DN_GROUP + 10
    b_steps = DN_GROUP + 1

    def group_a(p, buf):
        r0 = p * DN_GROUP_ROWS
        if isinstance(p, int) and p == 0:
            hist = dnm_ref[META_ROWS - DN_HIST:META_ROWS, 0:3 * DN_WIDTH]
        else:
            hist = dnx_ref[0, pl.ds(pl.multiple_of(r0 - DN_HIST, DN_HIST), DN_HIST), 0:3 * DN_WIDTH]
        if not isinstance(p, int):
            r0 = pl.multiple_of(r0, DN_GROUP_ROWS)
        act = conv_act(dnx_ref[0, pl.ds(r0, DN_GROUP_ROWS), 0:3 * DN_WIDTH].astype(F32), hist.astype(F32))
        gc, beta = gates(abx_ref[0, pl.ds(r0, DN_GROUP_ROWS), :], None)
        yield
        yield from phase_a([act[r, :] for r in rows], [gc[r, :] for r in rows], [beta[r, :] for r in rows], buf)

    def group_b(p, buf):
        r0 = p * DN_GROUP_ROWS
        if not isinstance(p, int):
            r0 = pl.multiple_of(r0, DN_GROUP_ROWS)
        z = dnx_ref[0, pl.ds(r0, DN_GROUP_ROWS), 3 * DN_WIDTH:].astype(F32)
        outs = []
        yield from phase_b([z[r, :] for r in rows], buf, outs)
        yx_ref[0, pl.ds(r0, DN_GROUP_ROWS), :] = jnp.concatenate(outs, axis=0)

    n_groups = SEQ // DN_GROUP_ROWS
    run((group_a(0, bufs[0]), a_steps))

    def body(kk, carry):
        p = 2 * kk
        run((group_a(p + 1, bufs[1]), a_steps), (group_b(p, bufs[0]), b_steps))
        run((group_a(p + 2, bufs[0]), a_steps), (group_b(p + 1, bufs[1]), b_steps))
        return carry

    lax.fori_loop(0, n_groups // 2 - 1, body, 0)
    run((group_a(n_groups - 1, bufs[1]), a_steps), (group_b(n_groups - 2, bufs[0]), b_steps))
    run((group_b(n_groups - 1, bufs[1]), b_steps))


def _deltanet(dnx, dnm, abx, abm, cw, alog, dtb, onw):
    nb = dnx.shape[0]
    const = lambda shape: pl.BlockSpec(shape, lambda b: (0,) * len(shape))
    return pl.pallas_call(
        _deltanet_kernel,
        grid=(nb,),
        in_specs=[
            pl.BlockSpec((1, SEQ, N_DN), lambda b: (b, 0, 0)),
            const((META_ROWS, N_DN)),
            pl.BlockSpec((1, SEQ, N_AB), lambda b: (b, 0, 0)),
            const((META_ROWS, N_AB)),
            const((DN_CONV, 3 * DN_WIDTH)),
            const((1, 128)), const((1, 128)), const((1, 128)),
        ],
        out_specs=[
            pl.BlockSpec((1, SEQ, DN_WIDTH), lambda b: (b, 0, 0)),
            pl.BlockSpec((1, META_ROWS, DN_WIDTH), lambda b: (b, 0, 0)),
        ],
        out_shape=[
            jax.ShapeDtypeStruct((nb, SEQ, DN_WIDTH), BF16),
            jax.ShapeDtypeStruct((nb, META_ROWS, DN_WIDTH), BF16),
        ],
        scratch_shapes=[
            pltpu.VMEM((DN_HEADS, DN_DIM, DN_DIM), F32),
            pltpu.VMEM((DN_HIST + DN_GROUP_ROWS, 3 * DN_WIDTH), F32),
            pltpu.VMEM((DN_GROUP, DN_STACK, DN_STACK), F32),
            pltpu.VMEM((DN_GROUP, DN_STACK, DN_STACK), F32),
            pltpu.VMEM((DN_GROUP, DN_STACK, DN_STACK), BF16),
            pltpu.VMEM((DN_GROUP, DN_STACK, DN_STACK), BF16),
            pltpu.VMEM((DN_GROUP, DN_STACK, 2 * DN_DIM), BF16),
        ] + 2 * [
            pltpu.VMEM((DN_GROUP, DN_STACK, 2 * DN_DIM), F32),
            pltpu.VMEM((DN_GROUP, DN_STACK, DN_STACK), BF16),
            pltpu.VMEM((DN_GROUP, DN_STACK, DN_DIM), BF16),
            pltpu.VMEM((DN_GROUP, DN_STACK, DN_DIM), BF16),
            pltpu.VMEM((DN_GROUP, DN_STACK, DN_DIM), F32),
        ],
        compiler_params=_cparams(("arbitrary",)),
        name="deltanet",
    )(dnx, dnm, abx, abm, cw, alog, dtb, onw)


FFN_ROWS = 512
FFN_HALO = 16
N_FF_BLK = D_FF // FF_BLK


def _ffn_kernel(x_ref, xh_ref, mh_ref, ya_ref, yah_ref, yam_ref, yd_ref, ydh_ref, ydm_ref,
                wo_ref, nw_ref, wg_ref, wu_ref, cw_ref, cb_ref, wd_ref,
                o_ref, u_s, g0_s, g1_s, up0_s, up1_s, act_s):
    r = pl.program_id(1)
    mixed = jnp.concatenate([ya_ref[0], yd_ref[0]], axis=1)
    h_mid = x_ref[0] + _dot(mixed, wo_ref[...])
    o_ref[0] = h_mid
    u_s[FFN_HALO:, :] = _rms(h_mid, nw_ref[...]).astype(BF16)
    first = r == 0
    mixed_h = jnp.concatenate([jnp.where(first, yam_ref[0], yah_ref[0]),
                               jnp.where(first, ydm_ref[0], ydh_ref[0])], axis=1)
    h_halo = jnp.where(first, mh_ref[...], xh_ref[0]) + _dot(mixed_h, wo_ref[...])
    u_s[0:FFN_HALO, :] = _rms(h_halo, nw_ref[...]).astype(BF16)

    g_bufs = (g0_s, g1_s)
    up_bufs = (up0_s, up1_s)

    def project(f):
        g_bufs[f % 2][...] = _dot(u_s[...], wg_ref[f])
        up_bufs[f % 2][...] = _dot(u_s[FFN_HALO:, :], wu_ref[f])

    project(0)
    for f in range(N_FF_BLK):
        if f + 1 < N_FF_BLK:
            project(f + 1)
        g_s = g_bufs[f % 2]
        cols = slice(f * FF_BLK, (f + 1) * FF_BLK)
        gate = (cw_ref[2:3, cols] * g_s[FFN_HALO:, :]
                + cw_ref[1:2, cols] * g_s[FFN_HALO - 1:FFN_HALO - 1 + FFN_ROWS, :]
                + cw_ref[0:1, cols] * g_s[FFN_HALO - 2:FFN_HALO - 2 + FFN_ROWS, :]
                + cb_ref[:, cols])
        act_s[:, cols] = (_silu(gate) * up_bufs[f % 2][...]).astype(BF16)
    o_ref[0] += _dot(act_s[...], wd_ref[...])


def _ffn(x, hp_meta, yax, yam, ydx, ydm, wo, nw, wg, wu, cw, cb, wd):
    nb = x.shape[0]
    nr = SEQ // FFN_ROWS
    hb = FFN_ROWS // FFN_HALO
    halo_idx = lambda b, r: (b, jnp.maximum(r * hb - 1, 0), 0)
    meta_idx = lambda b, r: (b, META_ROWS // FFN_HALO - 1, 0)
    main_idx = lambda b, r: (b, r, 0)
    resident = lambda shape: pl.BlockSpec(shape, lambda b, r: (0,) * len(shape),
                                          pipeline_mode=pl.Buffered(1))
    return pl.pallas_call(
        _ffn_kernel,
        grid=(nb, nr),
        in_specs=[
            pl.BlockSpec((1, FFN_ROWS, D_MODEL), main_idx),
            pl.BlockSpec((1, FFN_HALO, D_MODEL), halo_idx),
            pl.BlockSpec((FFN_HALO, D_MODEL), lambda b, r: (META_ROWS // FFN_HALO - 1, 0)),
            pl.BlockSpec((1, FFN_ROWS, MLA_HEADS * V_HEAD), main_idx),
            pl.BlockSpec((1, FFN_HALO, MLA_HEADS * V_HEAD), halo_idx),
            pl.BlockSpec((1, FFN_HALO, MLA_HEADS * V_HEAD), meta_idx),
            pl.BlockSpec((1, FFN_ROWS, DN_WIDTH), main_idx),
            pl.BlockSpec((1, FFN_HALO, DN_WIDTH), halo_idx),
            pl.BlockSpec((1, FFN_HALO, DN_WIDTH), meta_idx),
            resident((D_MODEL, D_MODEL)),
            resident((1, D_MODEL)),
            resident((N_FF_BLK, D_MODEL, FF_BLK)),
            resident((N_FF_BLK, D_MODEL, FF_BLK)),
            resident((3, D_FF)),
            resident((1, D_FF)),
            resident((D_FF, D_MODEL)),
        ],
        out_specs=pl.BlockSpec((1, FFN_ROWS, D_MODEL), main_idx),
        out_shape=jax.ShapeDtypeStruct((nb, SEQ, D_MODEL), F32),
        scratch_shapes=[
            pltpu.VMEM((FFN_HALO + FFN_ROWS, D_MODEL), BF16),
            pltpu.VMEM((FFN_HALO + FFN_ROWS, FF_BLK), F32),
            pltpu.VMEM((FFN_HALO + FFN_ROWS, FF_BLK), F32),
            pltpu.VMEM((FFN_ROWS, FF_BLK), F32),
            pltpu.VMEM((FFN_ROWS, FF_BLK), F32),
            pltpu.VMEM((FFN_ROWS, D_FF), BF16),
        ],
        compiler_params=_cparams(("arbitrary", "arbitrary")),
        name="outproj_ffn",
    )(x, x, hp_meta, yax, yax, yam, ydx, ydx, ydm, wo, nw, wg, wu, cw, cb, wd)


def _rot_cols(w):
    half = QK_ROPE // 2
    return jnp.concatenate([-w[..., half:], w[..., :half]], axis=-1)


def _swap_halves(w):
    half = QK_ROPE // 2
    return jnp.concatenate([w[..., half:], w[..., :half]], axis=-1)


def _pad_lanes(v, n=128):
    return jnp.pad(v.astype(F32), (0, n - v.shape[0])).reshape(1, n)


def _layer(x, hp_meta, l, attn_norm_w, w_in, q_a_norm_w, w_q_b, kv_a_norm_w, w_kv_b, q_norm_w,
           k_norm_w, mla_out_norm_w, dn_conv_w, dn_A_log, dn_dt_bias, dn_out_norm_w, w_out,
           ffn_norm_w, w_gate, w_up, ffn_conv_w, ffn_conv_b, w_down):
    nb = x.shape[0]
    c1 = Q_LORA
    c2 = c1 + KV_LORA
    c3 = c2 + QK_ROPE
    c4 = c3 + 3 * DN_WIDTH
    c5 = c4 + DN_WIDTH
    win = w_in[l]
    k_pe_w = win[:, c2:c3]
    w1 = jnp.concatenate(
        [win[:, :c2], k_pe_w, _rot_cols(k_pe_w), win[:, c3:c5], win[:, c5:],
         jnp.zeros((D_MODEL, N_AB - 2 * DN_HEADS), F32)], axis=1).astype(BF16)

    wqb = w_q_b[l].reshape(Q_LORA, MLA_HEADS, QK_HEAD)
    wqt = jnp.concatenate([wqb[..., :QK_NOPE], wqb[..., QK_NOPE:], _rot_cols(wqb[..., QK_NOPE:])],
                          axis=-1).reshape(Q_LORA, MLA_HEADS * HEAD_SLOT).T.astype(BF16)
    wkvb = w_kv_b[l].reshape(KV_LORA, MLA_HEADS, QK_NOPE + V_HEAD)
    wk = wkvb[..., :QK_NOPE].reshape(KV_LORA, MLA_HEADS * QK_NOPE).astype(BF16)
    wvt = wkvb[..., QK_NOPE:].reshape(KV_LORA, MLA_HEADS * V_HEAD).T.astype(BF16)
    qn = q_norm_w[l].astype(F32)
    kn = k_norm_w[l].astype(F32)
    qnn = qn[:QK_NOPE].reshape(QK_NOPE, 1)
    knn = kn[:QK_NOPE].reshape(1, 128)
    qrw = jnp.concatenate([qn[QK_NOPE:], _swap_halves(qn[QK_NOPE:])]).reshape(2 * QK_ROPE, 1)
    krw = jnp.concatenate([kn[QK_NOPE:], _swap_halves(kn[QK_NOPE:])]).reshape(1, 128)

    half = QK_ROPE // 2
    inv_freq = ROPE_THETA ** (-jnp.arange(half, dtype=F32) / half)
    pos = (jnp.arange(ATT_ROWS, dtype=jnp.int32) - ATT_META_VALID).astype(F32)
    ang = pos[:, None] * inv_freq[None, :]
    cs = jnp.concatenate([jnp.cos(ang), jnp.cos(ang), jnp.sin(ang), jnp.sin(ang)], axis=1)
    cst = cs.T

    nw1 = attn_norm_w[l].astype(F32).reshape(1, D_MODEL)
    latm, dnm, abm = _inproj(hp_meta, nw1, w1, META_ROWS)
    latx, dnx, abx = _inproj(x.reshape(nb * SEQ, D_MODEL), nw1, w1, 512)
    latx = latx.reshape(nb, SEQ, N_LAT)
    dnx = dnx.reshape(nb, SEQ, N_DN)
    abx = abx.reshape(nb, SEQ, N_AB)

    yax, yam = _mla(latx, latm, cs, cst, q_a_norm_w[l].astype(F32).reshape(1, Q_LORA), wqt,
                    kv_a_norm_w[l].astype(F32).reshape(1, KV_LORA), wk, wvt,
                    qnn, qrw, knn, krw, mla_out_norm_w[l].astype(F32).reshape(V_HEAD, 1))
    ydx, ydm = _deltanet(dnx, dnm, abx, abm, dn_conv_w[l].astype(F32),
                         _pad_lanes(dn_A_log[l]), _pad_lanes(dn_dt_bias[l]),
                         dn_out_norm_w[l].astype(F32).reshape(1, DN_DIM))
    blocked = lambda w: w.reshape(D_MODEL, N_FF_BLK, FF_BLK).transpose(1, 0, 2).astype(BF16)
    return _ffn(x, hp_meta, yax, yam, ydx, ydm, w_out[l].astype(BF16),
                ffn_norm_w[l].astype(F32).reshape(1, D_MODEL), blocked(w_gate[l]),
                blocked(w_up[l]), ffn_conv_w[l].astype(F32),
                ffn_conv_b[l].astype(F32).reshape(1, D_FF), w_down[l].astype(BF16))


def kernel(x, meta_tokens, attn_norm_w, w_in, q_a_norm_w, w_q_b, kv_a_norm_w, w_kv_b, q_norm_w, k_norm_w, mla_out_norm_w, dn_conv_w, dn_A_log, dn_dt_bias, dn_out_norm_w, w_out, ffn_norm_w, w_gate, w_up, ffn_conv_w, ffn_conv_b, w_down):
    assert x.shape[1:] == (SEQ, D_MODEL) and w_in.shape[0] == 1
    hp_meta = jnp.concatenate([jnp.zeros((META_PAD, D_MODEL), x.dtype), meta_tokens.astype(x.dtype)], axis=0)
    return _layer(x, hp_meta, 0, attn_norm_w, w_in, q_a_norm_w, w_q_b, kv_a_norm_w, w_kv_b, q_norm_w,
                  k_norm_w, mla_out_norm_w, dn_conv_w, dn_A_log, dn_dt_bias, dn_out_norm_w, w_out,
                  ffn_norm_w, w_gate, w_up, ffn_conv_w, ffn_conv_b, w_down)
```

```python
import math

import jax
import jax.numpy as jnp
from jax import lax
from jax.experimental import pallas as pl
from jax.experimental.pallas import tpu as pltpu

F32 = jnp.float32
BF16 = jnp.bfloat16

D_MODEL = 1024
SEQ = 2048
N_META = 16
META_ROWS = 64
META_PAD = META_ROWS - N_META

MLA_HEADS = 4
QK_NOPE = 128
QK_ROPE = 64
QK_HEAD = QK_NOPE + QK_ROPE
V_HEAD = 128
Q_LORA = 256
KV_LORA = 256
ROPE_THETA = 10000.0
HEAD_SLOT = 256

DN_HEADS = 4
DN_DIM = 128
DN_WIDTH = DN_HEADS * DN_DIM
DN_CHUNK = 64
DN_STACK = DN_HEADS * DN_CHUNK
DN_CONV = 4

D_FF = 2816
FF_BLK = 256
NORM_EPS = 1e-6

C_LAT = 0
N_LAT = Q_LORA + KV_LORA + 2 * QK_ROPE
C_DN = N_LAT
N_DN = 4 * DN_WIDTH
C_AB = C_DN + N_DN
N_AB = 128
N_PROJ = C_AB + N_AB

VMEM_LIMIT = 56 * 1024 * 1024


def _cparams(sem):
    return pltpu.CompilerParams(dimension_semantics=sem, vmem_limit_bytes=VMEM_LIMIT)


def _rms(x, w):
    return x * lax.rsqrt(jnp.mean(x * x, axis=-1, keepdims=True) + NORM_EPS) * w


def _dot(a, b):
    return jnp.dot(a, b, preferred_element_type=F32)


def _dot_nt(a, b):
    return lax.dot_general(a, b, (((1,), (1,)), ((), ())), preferred_element_type=F32)


def _dot_tn(a, b):
    return lax.dot_general(a, b, (((0,), (0,)), ((), ())), preferred_element_type=F32)


def _silu(x):
    return x * jax.nn.sigmoid(x)


def _inproj_kernel(x_ref, nw_ref, w_ref, lat_ref, dn_ref, ab_ref):
    u = _rms(x_ref[...], nw_ref[...]).astype(BF16)
    p = _dot(u, w_ref[...])
    lat_ref[...] = p[:, C_LAT:C_LAT + N_LAT].astype(BF16)
    dn_ref[...] = p[:, C_DN:C_DN + N_DN].astype(BF16)
    ab_ref[...] = p[:, C_AB:C_AB + N_AB]


def _inproj(x2d, nw, w, row_tile):
    rows = x2d.shape[0]
    grid = (rows // row_tile,)
    return pl.pallas_call(
        _inproj_kernel,
        grid=grid,
        in_specs=[
            pl.BlockSpec((row_tile, D_MODEL), lambda i: (i, 0)),
            pl.BlockSpec((1, D_MODEL), lambda i: (0, 0)),
            pl.BlockSpec((D_MODEL, N_PROJ), lambda i: (0, 0)),
        ],
        out_specs=[
            pl.BlockSpec((row_tile, N_LAT), lambda i: (i, 0)),
            pl.BlockSpec((row_tile, N_DN), lambda i: (i, 0)),
            pl.BlockSpec((row_tile, N_AB), lambda i: (i, 0)),
        ],
        out_shape=[
            jax.ShapeDtypeStruct((rows, N_LAT), BF16),
            jax.ShapeDtypeStruct((rows, N_DN), BF16),
            jax.ShapeDtypeStruct((rows, N_AB), F32),
        ],
        compiler_params=_cparams(("arbitrary",)),
        name="inproj",
    )(x2d, nw, w)


ATT_TQ = 256
ATT_PROJ_ROWS = 512
ATT_META = 2 * META_ROWS
ATT_META_VALID = ATT_META - N_META
ATT_ROWS = ATT_META + SEQ
NEG_INF = float("-inf")


def _mla_kernel(latx_ref, latm_ref, cs_ref, cst_ref, qaw_ref, wqt_ref, kvaw_ref, wk_ref, wvt_ref,
                qnn_ref, qrw_ref, knn_ref, krw_ref, onw_ref,
                yx_ref, ym_ref, km_s, qm_s, vm_s, kx_s, qx_s, vx_s, acc_s, stm_s, st0_s, st1_s, p_s):
    low = lax.broadcasted_iota(jnp.int32, (1, 128), 1) < QK_ROPE
    scale = 1.0 / math.sqrt(QK_HEAD)

    def project(lat, cs, cst):
        nrows = lat.shape[0]
        qn = _rms(lat[:, 0:Q_LORA], qaw_ref[...])
        kvn = _rms(lat[:, Q_LORA:Q_LORA + KV_LORA], kvaw_ref[...])
        pe = lat[:, Q_LORA + KV_LORA:N_LAT]
        qt = _dot(wqt_ref[...], qn.T.astype(BF16))
        vt = _dot(wvt_ref[...], kvn.T.astype(BF16)).astype(BF16)
        kn = _dot(kvn.astype(BF16), wk_ref[...])
        a = pe * (cs * krw_ref[...])
        k_rope = jnp.where(low, a + pltpu.roll(a, QK_ROPE, 1), 0.0)
        pe_ss = jnp.sum(jnp.where(low, pe * pe, 0.0), axis=-1, keepdims=True)
        cos_t = cst[0:QK_ROPE, :]
        sin_t = cst[QK_ROPE:2 * QK_ROPE, :]
        k_parts = []
        q_parts = []
        for h in range(MLA_HEADS):
            nope = kn[:, h * QK_NOPE:(h + 1) * QK_NOPE]
            rs = lax.rsqrt((jnp.sum(nope * nope, axis=-1, keepdims=True) + pe_ss) * (1.0 / QK_HEAD) + NORM_EPS)
            k_parts += [(nope * rs * knn_ref[...]).astype(BF16), (k_rope * rs).astype(BF16)]
            r0 = h * HEAD_SLOT
            qnope = qt[r0:r0 + QK_NOPE, :]
            qrope = qt[r0 + QK_NOPE:r0 + QK_HEAD, :]
            qrot = qt[r0 + QK_HEAD:r0 + HEAD_SLOT, :]
            ssq = (jnp.sum(qnope * qnope, axis=0, keepdims=True)
                   + jnp.sum(qrope * qrope, axis=0, keepdims=True))
            rsq = lax.rsqrt(ssq * (1.0 / QK_HEAD) + NORM_EPS) * scale
            roped = (qrope * (qrw_ref[0:QK_ROPE, :] * cos_t)
                     + qrot * (qrw_ref[QK_ROPE:2 * QK_ROPE, :] * sin_t))
            q_parts += [(qnope * qnn_ref[...] * rsq).astype(BF16), (roped * rsq).astype(BF16),
                        jnp.zeros((HEAD_SLOT - QK_HEAD, nrows), BF16)]
        return jnp.concatenate(k_parts, axis=1), jnp.concatenate(q_parts, axis=0), vt

    latm = jnp.concatenate([jnp.zeros((META_ROWS, N_LAT), F32), latm_ref[...].astype(F32)], axis=0)
    km_s[...], qm_s[...], vm_s[...] = project(latm, cs_ref[0:ATT_META, :], cst_ref[:, 0:ATT_META])
    for c in range(SEQ // ATT_PROJ_ROWS):
        r0 = c * ATT_PROJ_ROWS
        k, qt, vt = project(latx_ref[0, r0:r0 + ATT_PROJ_ROWS, :].astype(F32),
                            cs_ref[ATT_META + r0:ATT_META + r0 + ATT_PROJ_ROWS, :],
                            cst_ref[:, ATT_META + r0:ATT_META + r0 + ATT_PROJ_ROWS])
        kx_s[r0:r0 + ATT_PROJ_ROWS, :] = k
        for t in range(ATT_PROJ_ROWS // ATT_TQ):
            qx_s[c * (ATT_PROJ_ROWS // ATT_TQ) + t] = qt[:, t * ATT_TQ:(t + 1) * ATT_TQ]
            vx_s[c * (ATT_PROJ_ROWS // ATT_TQ) + t] = vt[:, t * ATT_TQ:(t + 1) * ATT_TQ]

    def first_block(st):
        m = jnp.max(st, axis=0, keepdims=True)
        p = jnp.exp(st - m)
        return p.astype(BF16), m, jnp.sum(p, axis=0, keepdims=True)

    def next_block(st, m, l):
        m_new = jnp.maximum(m, jnp.max(st, axis=0, keepdims=True))
        alpha = jnp.exp(m - m_new)
        p = jnp.exp(st - m_new)
        return p.astype(BF16), m_new, alpha * l + jnp.sum(p, axis=0, keepdims=True), alpha

    def finish(acc_t, l):
        o = acc_t * (1.0 / l)
        o = o * lax.rsqrt(jnp.mean(o * o, axis=0, keepdims=True) + NORM_EPS) * onw_ref[...]
        return o.T.astype(BF16)

    hs = lambda h: slice(h * HEAD_SLOT, (h + 1) * HEAD_SLOT)
    vs = lambda h: slice(h * V_HEAD, (h + 1) * V_HEAD)

    mkey = lax.broadcasted_iota(jnp.int32, (ATT_META, ATT_META), 0)
    mqry = lax.broadcasted_iota(jnp.int32, (ATT_META, ATT_META), 1)
    meta_mask = (mkey <= mqry) & ((mkey >= ATT_META_VALID) | (mkey == mqry))
    for h in range(MLA_HEADS):
        st = jnp.where(meta_mask, _dot(km_s[:, hs(h)], qm_s[hs(h), :]), NEG_INF)
        p, m, l = first_block(st)
        o = finish(_dot(vm_s[vs(h), :], p), l)
        ym_ref[0, :, vs(h)] = o[META_ROWS:, :]

    meta_key_ok = lax.broadcasted_iota(jnp.int32, (ATT_META, ATT_TQ), 0) >= ATT_META_VALID
    diag_mask = (lax.broadcasted_iota(jnp.int32, (ATT_TQ, ATT_TQ), 0)
                 <= lax.broadcasted_iota(jnp.int32, (ATT_TQ, ATT_TQ), 1))

    st_bufs = (st0_s, st1_s)

    for i in range(SEQ // ATT_TQ):
        def scores(buf, k_of, nk):
            for h in range(MLA_HEADS):
                buf[h, 0:nk, :] = _dot(k_of(h), qx_s[i, hs(h), :])

        def absorb(buf, v_of, nk, mask, ms, ls):
            ms2, ls2, alphas = [], [], []
            for h in range(MLA_HEADS):
                st = buf[h, 0:nk, :]
                if mask is not None:
                    st = jnp.where(mask, st, NEG_INF)
                p, m, l, alpha = next_block(st, ms[h], ls[h])
                p_s[h, 0:nk, :] = p
                ms2.append(m)
                ls2.append(l)
                alphas.append(alpha)
            for h in range(MLA_HEADS):
                acc_s[h] = acc_s[h] * alphas[h] + _dot(v_of(h), p_s[h, 0:nk, :])
            return tuple(ms2), tuple(ls2)

        def k_blk(j):
            k0 = j * ATT_TQ
            if not isinstance(j, int):
                k0 = pl.multiple_of(k0, ATT_TQ)
            return lambda h: kx_s[pl.ds(k0, ATT_TQ), hs(h)]

        v_blk = lambda j: (lambda h: vx_s[j, vs(h), :])

        acc_s[...] = jnp.zeros_like(acc_s)
        ms = (jnp.full((1, ATT_TQ), NEG_INF, F32),) * MLA_HEADS
        ls = (jnp.zeros((1, ATT_TQ), F32),) * MLA_HEADS
        scores(stm_s, lambda h: km_s[:, hs(h)], ATT_META)
        scores(st_bufs[0], k_blk(0), ATT_TQ)
        ms, ls = absorb(stm_s, lambda h: vm_s[vs(h), :], ATT_META, meta_key_ok, ms, ls)

        def pair(t, st_):
            ms, ls = st_
            scores(st_bufs[1], k_blk(2 * t + 1), ATT_TQ)
            ms, ls = absorb(st_bufs[0], v_blk(2 * t), ATT_TQ, None, ms, ls)
            scores(st_bufs[0], k_blk(2 * t + 2), ATT_TQ)
            return absorb(st_bufs[1], v_blk(2 * t + 1), ATT_TQ, None, ms, ls)

        if i // 2 > 0:
            ms, ls = lax.fori_loop(0, i // 2, pair, (ms, ls))
        if i % 2 == 1:
            scores(st_bufs[1], k_blk(i), ATT_TQ)
            ms, ls = absorb(st_bufs[0], v_blk(i - 1), ATT_TQ, None, ms, ls)
        ms, ls = absorb(st_bufs[i % 2], v_blk(i), ATT_TQ, diag_mask, ms, ls)
        for h in range(MLA_HEADS):
            yx_ref[0, i * ATT_TQ:(i + 1) * ATT_TQ, vs(h)] = finish(acc_s[h], ls[h])


def _mla(latx, latm, cs, cst, qaw, wqt, kvaw, wk, wvt, qnn, qrw, knn, krw, onw):
    nb = latx.shape[0]
    const = lambda shape: pl.BlockSpec(shape, lambda b: (0,) * len(shape))
    nq = SEQ // ATT_TQ
    return pl.pallas_call(
        _mla_kernel,
        grid=(nb,),
        in_specs=[
            pl.BlockSpec((1, SEQ, N_LAT), lambda b: (b, 0, 0)),
            const((META_ROWS, N_LAT)),
            const((ATT_ROWS, 128)),
            const((128, ATT_ROWS)),
            const((1, Q_LORA)),
            const((MLA_HEADS * HEAD_SLOT, Q_LORA)),
            const((1, KV_LORA)),
            const((KV_LORA, MLA_HEADS * QK_NOPE)),
            const((MLA_HEADS * V_HEAD, KV_LORA)),
            const((QK_NOPE, 1)), const((2 * QK_ROPE, 1)), const((1, 128)), const((1, 128)),
            const((V_HEAD, 1)),
        ],
        out_specs=[
            pl.BlockSpec((1, SEQ, MLA_HEADS * V_HEAD), lambda b: (b, 0, 0)),
            pl.BlockSpec((1, META_ROWS, MLA_HEADS * V_HEAD), lambda b: (b, 0, 0)),
        ],
        out_shape=[
            jax.ShapeDtypeStruct((nb, SEQ, MLA_HEADS * V_HEAD), BF16),
            jax.ShapeDtypeStruct((nb, META_ROWS, MLA_HEADS * V_HEAD), BF16),
        ],
        scratch_shapes=[
            pltpu.VMEM((ATT_META, MLA_HEADS * HEAD_SLOT), BF16),
            pltpu.VMEM((MLA_HEADS * HEAD_SLOT, ATT_META), BF16),
            pltpu.VMEM((MLA_HEADS * V_HEAD, ATT_META), BF16),
            pltpu.VMEM((SEQ, MLA_HEADS * HEAD_SLOT), BF16),
            pltpu.VMEM((nq, MLA_HEADS * HEAD_SLOT, ATT_TQ), BF16),
            pltpu.VMEM((nq, MLA_HEADS * V_HEAD, ATT_TQ), BF16),
            pltpu.VMEM((MLA_HEADS, V_HEAD, ATT_TQ), F32),
            pltpu.VMEM((MLA_HEADS, ATT_META, ATT_TQ), F32),
            pltpu.VMEM((MLA_HEADS, ATT_TQ, ATT_TQ), F32),
            pltpu.VMEM((MLA_HEADS, ATT_TQ, ATT_TQ), F32),
            pltpu.VMEM((MLA_HEADS, ATT_TQ, ATT_TQ), BF16),
        ],
        compiler_params=_cparams(("arbitrary",)),
        name="mla",
    )(latx, latm, cs, cst, qaw, wqt, kvaw, wk, wvt, qnn, qrw, knn, krw, onw)


DN_GROUP = 4
DN_GROUP_ROWS = DN_GROUP * DN_CHUNK
DN_HIST = 16


def _deltanet_kernel(dnx_ref, dnm_ref, abx_ref, abm_ref, cw_ref, alog_ref, dtb_ref, onw_ref,
                     yx_ref, ym_ref, s_s, cbuf, af_s, t_s, pa_s, pb_s, rhs_s,
                     uw0_s, qk0_s, qg0_s, kd0_s, el0_s, uw1_s, qk1_s, qg1_s, kd1_s, el1_s,
                     smeta_s, ymeta_s):
    C = DN_CHUNK
    R = DN_STACK
    row = lax.broadcasted_iota(jnp.int32, (R, R), 0)
    col = lax.broadcasted_iota(jnp.int32, (R, R), 1)
    same = lambda sh: jnp.right_shift(row, sh) == jnp.right_shift(col, sh)
    m_incl = same(6) & (col <= row)
    m_strict = same(6) & (col < row)
    m_d16 = m_strict & same(4)
    m_l32 = m_strict & same(5) & jnp.logical_not(same(4))
    m_l64 = m_strict & jnp.logical_not(same(5))
    eye = (row == col).astype(F32)
    neg_a = -jnp.exp(alog_ref[...])

    s_s[...] = jnp.zeros_like(s_s)

    def stack(x):
        return jnp.concatenate([x[:, h * DN_DIM:(h + 1) * DN_DIM] for h in range(DN_HEADS)], axis=0)

    def stack_col(x, c0):
        return jnp.concatenate(
            [jnp.broadcast_to(x[:, c0 + h:c0 + h + 1], (C, DN_DIM)) for h in range(DN_HEADS)], axis=0)

    def bdot(a, b):
        return _dot(a.astype(BF16), b.astype(BF16))

    def conv_act(pre, hist):
        n = pre.shape[0]
        full = jnp.concatenate([hist, pre], axis=0)
        conv = cw_ref[3:4, :] * pre
        for s in range(1, DN_CONV):
            conv = conv + cw_ref[3 - s:4 - s, :] * pltpu.roll(full, s, 0)[DN_HIST:, :]
        return _silu(conv)

    def gates(ab, row_ok):
        xa = ab + dtb_ref[...]
        softplus = jnp.maximum(xa, 0.0) + jnp.log(1.0 + jnp.exp(-jnp.abs(xa)))
        g = neg_a * softplus
        beta = jax.nn.sigmoid(ab)
        if row_ok is not None:
            g = jnp.where(row_ok, g, 0.0)
            beta = jnp.where(row_ok, beta, 0.0)
        pos = lax.broadcasted_iota(jnp.int32, g.shape, 0) & (C - 1)
        gc = g
        for s in (1, 2, 4, 8, 16, 32):
            gc = gc + jnp.where(pos >= s, pltpu.roll(gc, s, 0), 0.0)
        return gc, beta

    def phase_a(acts, gcl, betal, buf):
        uw_b, qk_b, qg_b, kd_b, el_b = buf
        n = len(acts)
        for c in range(n):
            act, gc, beta = acts[c], gcl[c], betal[c]
            q = stack(act[:, 0:DN_WIDTH])
            k = stack(act[:, DN_WIDTH:2 * DN_WIDTH])
            v = stack(act[:, 2 * DN_WIDTH:3 * DN_WIDTH])
            q = q * lax.rsqrt(jnp.sum(q * q, axis=-1, keepdims=True) + NORM_EPS) * (1.0 / math.sqrt(DN_DIM))
            k = k * lax.rsqrt(jnp.sum(k * k, axis=-1, keepdims=True) + NORM_EPS)
            gcs = stack_col(gc, 0)
            gls = stack_col(jnp.broadcast_to(gc[C - 1:C, :], (C, 128)), 0)
            bs = stack_col(beta, DN_HEADS)
            grow = gcs.T[0:1, :]
            dec = jnp.exp(jnp.where(m_incl, gcs[:, 0:1] - grow, NEG_INF))
            kb = k * bs
            kbf = k.astype(BF16)
            eg = jnp.exp(gcs)
            af_s[c] = _dot_nt(kb.astype(BF16), kbf) * dec
            qk_b[c] = jnp.where(m_incl, _dot_nt(q.astype(BF16), kbf) * dec, 0.0).astype(BF16)
            rhs_s[c] = jnp.concatenate([v * bs, kb * eg], axis=1).astype(BF16)
            qg_b[c] = (q * eg).astype(BF16)
            kd_b[c] = (k * jnp.exp(gls - gcs)).astype(BF16)
            el_b[c] = jnp.exp(gls)
            yield

        for c in range(n):
            b1 = jnp.where(m_d16, af_s[c], 0.0)
            pa_s[c] = b1.astype(BF16)
            t_s[c] = eye - b1
        for c in range(n):
            pb_s[c] = _dot(pa_s[c], pa_s[c]).astype(BF16)
        yield
        for c in range(n):
            t_s[c] = t_s[c] + _dot(t_s[c].astype(BF16), pb_s[c])
            pa_s[c] = _dot(pb_s[c], pb_s[c]).astype(BF16)
        yield
        for c in range(n):
            t_s[c] = t_s[c] + _dot(t_s[c].astype(BF16), pa_s[c])
            pb_s[c] = _dot(pa_s[c], pa_s[c]).astype(BF16)
        yield
        for c in range(n):
            t_s[c] = t_s[c] + _dot(t_s[c].astype(BF16), pb_s[c])
        yield
        for m_low in (m_l32, m_l64):
            for c in range(n):
                pa_s[c] = _dot(jnp.where(m_low, af_s[c], 0.0).astype(BF16), t_s[c].astype(BF16)).astype(BF16)
            yield
            for c in range(n):
                t_s[c] = t_s[c] - _dot(t_s[c].astype(BF16), pa_s[c])
            yield
        for c in range(n):
            uw_b[c] = _dot(t_s[c].astype(BF16), rhs_s[c])

    def phase_b(zs, buf, outs):
        uw_b, qk_b, qg_b, kd_b, el_b = buf
        for c in range(len(zs)):
            vnew = []
            o_inter = []
            for h in range(DN_HEADS):
                r0 = h * C
                s_h = s_s[h]
                sb = s_h.astype(BF16)
                vn = uw_b[c, r0:r0 + C, 0:DN_DIM] - _dot(uw_b[c, r0:r0 + C, DN_DIM:].astype(BF16), sb)
                o_inter.append(_dot(qg_b[c, r0:r0 + C, :], sb))
                s_s[h] = s_h * el_b[c, r0:r0 + 1, :] + _dot_tn(kd_b[c, r0:r0 + C, :], vn.astype(BF16))
                vnew.append(vn)
            o = jnp.concatenate(o_inter, axis=0) + _dot(qk_b[c], jnp.concatenate(vnew, axis=0).astype(BF16))
            o = _rms(o, onw_ref[...])
            out = jnp.concatenate([o[h * C:(h + 1) * C, :] for h in range(DN_HEADS)], axis=1) * _silu(zs[c])
            outs.append(out.astype(BF16))
            yield

    def run(*gens_and_steps):
        gens = [g for g, _ in gens_and_steps]
        lens = [s for _, s in gens_and_steps]
        done = [0] * len(gens)
        alive = [True] * len(gens)
        while any(alive):
            i = min((j for j in range(len(gens)) if alive[j]), key=lambda j: (done[j] + 0.5) / lens[j])
            try:
                next(gens[i])
                done[i] += 1
            except StopIteration:
                alive[i] = False

    bufs = ((uw0_s, qk0_s, qg0_s, kd0_s, el0_s), (uw1_s, qk1_s, qg1_s, kd1_s, el1_s))

    @pl.when(pl.program_id(0) == 0)
    def _():
        meta_ok = lax.broadcasted_iota(jnp.int32, (C, 128), 0) >= META_PAD
        dn0 = dnm_ref[...]
        act0 = conv_act(dn0[:, 0:3 * DN_WIDTH].astype(F32), jnp.zeros((DN_HIST, 3 * DN_WIDTH), F32))
        gc0, beta0 = gates(abm_ref[...], meta_ok)
        run((phase_a([act0], [gc0], [beta0], bufs[1]), 1))
        out0 = []
        run((phase_b([dn0[:, 3 * DN_WIDTH:].astype(F32)], bufs[1], out0), 1))
        ymeta_s[...] = out0[0]
        smeta_s[...] = s_s[...]

    s_s[...] = smeta_s[...]
    ym_ref[0] = ymeta_s[...]

    rows = [slice(c * C, (c + 1) * C) for c in range(DN_GROUP)]
    a_steps = DN_GROUP + 10
    b_steps = DN_GROUP + 1

    def group_a(p, buf):
        r0 = p * DN_GROUP_ROWS
        if isinstance(p, int) and p == 0:
            hist = dnm_ref[META_ROWS - DN_HIST:META_ROWS, 0:3 * DN_WIDTH]
        else:
            hist = dnx_ref[0, pl.ds(pl.multiple_of(r0 - DN_HIST, DN_HIST), DN_HIST), 0:3 * DN_WIDTH]
        if not isinstance(p, int):
            r0 = pl.multiple_of(r0, DN_GROUP_ROWS)
        act = conv_act(dnx_ref[0, pl.ds(r0, DN_GROUP_ROWS), 0:3 * DN_WIDTH].astype(F32), hist.astype(F32))
        gc, beta = gates(abx_ref[0, pl.ds(r0, DN_GROUP_ROWS), :], None)
        yield
        yield from phase_a([act[r, :] for r in rows], [gc[r, :] for r in rows], [beta[r, :] for r in rows], buf)

    def group_b(p, buf):
        r0 = p * DN_GROUP_ROWS
        if not isinstance(p, int):
            r0 = pl.multiple_of(r0, DN_GROUP_ROWS)
        z = dnx_ref[0, pl.ds(r0, DN_GROUP_ROWS), 3 * DN_WIDTH:].astype(F32)
        outs = []
        yield from phase_b([z[r, :] for r in rows], buf, outs)
        yx_ref[0, pl.ds(r0, DN_GROUP_ROWS), :] = jnp.concatenate(outs, axis=0)

    n_groups = SEQ // DN_GROUP_ROWS
    run((group_a(0, bufs[0]), a_steps))

    def body(kk, carry):
        p = 2 * kk
        run((group_a(p + 1, bufs[1]), a_steps), (group_b(p, bufs[0]), b_steps))
        run((group_a(p + 2, bufs[0]), a_steps), (group_b(p + 1, bufs[1]), b_steps))
        return carry

    lax.fori_loop(0, n_groups // 2 - 1, body, 0)
    run((group_a(n_groups - 1, bufs[1]), a_steps), (group_b(n_groups - 2, bufs[0]), b_steps))
    run((group_b(n_groups - 1, bufs[1]), b_steps))


def _deltanet(dnx, dnm, abx, abm, cw, alog, dtb, onw):
    nb = dnx.shape[0]
    const = lambda shape: pl.BlockSpec(shape, lambda b: (0,) * len(shape))
    return pl.pallas_call(
        _deltanet_kernel,
        grid=(nb,),
        in_specs=[
            pl.BlockSpec((1, SEQ, N_DN), lambda b: (b, 0, 0)),
            const((META_ROWS, N_DN)),
            pl.BlockSpec((1, SEQ, N_AB), lambda b: (b, 0, 0)),
            const((META_ROWS, N_AB)),
            const((DN_CONV, 3 * DN_WIDTH)),
            const((1, 128)), const((1, 128)), const((1, 128)),
        ],
        out_specs=[
            pl.BlockSpec((1, SEQ, DN_WIDTH), lambda b: (b, 0, 0)),
            pl.BlockSpec((1, META_ROWS, DN_WIDTH), lambda b: (b, 0, 0)),
        ],
        out_shape=[
            jax.ShapeDtypeStruct((nb, SEQ, DN_WIDTH), BF16),
            jax.ShapeDtypeStruct((nb, META_ROWS, DN_WIDTH), BF16),
        ],
        scratch_shapes=[
            pltpu.VMEM((DN_HEADS, DN_DIM, DN_DIM), F32),
            pltpu.VMEM((DN_HIST + DN_GROUP_ROWS, 3 * DN_WIDTH), F32),
            pltpu.VMEM((DN_GROUP, DN_STACK, DN_STACK), F32),
            pltpu.VMEM((DN_GROUP, DN_STACK, DN_STACK), F32),
            pltpu.VMEM((DN_GROUP, DN_STACK, DN_STACK), BF16),
            pltpu.VMEM((DN_GROUP, DN_STACK, DN_STACK), BF16),
            pltpu.VMEM((DN_GROUP, DN_STACK, 2 * DN_DIM), BF16),
        ] + 2 * [
            pltpu.VMEM((DN_GROUP, DN_STACK, 2 * DN_DIM), F32),
            pltpu.VMEM((DN_GROUP, DN_STACK, DN_STACK), BF16),
            pltpu.VMEM((DN_GROUP, DN_STACK, DN_DIM), BF16),
            pltpu.VMEM((DN_GROUP, DN_STACK, DN_DIM), BF16),
            pltpu.VMEM((DN_GROUP, DN_STACK, DN_DIM), F32),
        ] + [
            pltpu.VMEM((DN_HEADS, DN_DIM, DN_DIM), F32),
            pltpu.VMEM((META_ROWS, DN_WIDTH), BF16),
        ],
        compiler_params=_cparams(("arbitrary",)),
        name="deltanet",
    )(dnx, dnm, abx, abm, cw, alog, dtb, onw)


FFN_ROWS = 512
FFN_HALO = 16
N_FF_BLK = D_FF // FF_BLK


def _ffn_kernel(x_ref, xh_ref, mh_ref, ya_ref, yah_ref, yam_ref, yd_ref, ydh_ref, ydm_ref,
                wo_ref, nw_ref, wg_ref, wu_ref, cw_ref, cb_ref, wd_ref,
                o_ref, u_s, g0_s, g1_s, up0_s, up1_s, act_s):
    r = pl.program_id(1)
    mixed = jnp.concatenate([ya_ref[0], yd_ref[0]], axis=1)
    h_mid = x_ref[0] + _dot(mixed, wo_ref[...])
    o_ref[0] = h_mid
    u_s[FFN_HALO:, :] = _rms(h_mid, nw_ref[...]).astype(BF16)
    first = r == 0
    mixed_h = jnp.concatenate([jnp.where(first, yam_ref[0], yah_ref[0]),
                               jnp.where(first, ydm_ref[0], ydh_ref[0])], axis=1)
    h_halo = jnp.where(first, mh_ref[...], xh_ref[0]) + _dot(mixed_h, wo_ref[...])
    u_s[0:FFN_HALO, :] = _rms(h_halo, nw_ref[...]).astype(BF16)

    g_bufs = (g0_s, g1_s)
    up_bufs = (up0_s, up1_s)

    def project(f):
        cols = slice(f * FF_BLK, (f + 1) * FF_BLK)
        g_bufs[f % 2][...] = _dot(u_s[...], wg_ref[:, cols])
        up_bufs[f % 2][...] = _dot(u_s[FFN_HALO:, :], wu_ref[:, cols])

    project(0)
    for f in range(N_FF_BLK):
        if f + 1 < N_FF_BLK:
            project(f + 1)
        g_s = g_bufs[f % 2]
        cols = slice(f * FF_BLK, (f + 1) * FF_BLK)
        gate = (cw_ref[2:3, cols] * g_s[FFN_HALO:, :]
                + cw_ref[1:2, cols] * g_s[FFN_HALO - 1:FFN_HALO - 1 + FFN_ROWS, :]
                + cw_ref[0:1, cols] * g_s[FFN_HALO - 2:FFN_HALO - 2 + FFN_ROWS, :]
                + cb_ref[:, cols])
        act_s[:, cols] = (_silu(gate) * up_bufs[f % 2][...]).astype(BF16)
    o_ref[0] += _dot(act_s[...], wd_ref[...])


def _ffn(x, hp_meta, yax, yam, ydx, ydm, wo, nw, wg, wu, cw, cb, wd):
    nb = x.shape[0]
    nr = SEQ // FFN_ROWS
    hb = FFN_ROWS // FFN_HALO
    halo_idx = lambda b, r: (b, jnp.maximum(r * hb - 1, 0), 0)
    meta_idx = lambda b, r: (b, META_ROWS // FFN_HALO - 1, 0)
    main_idx = lambda b, r: (b, r, 0)
    resident = lambda shape: pl.BlockSpec(shape, lambda b, r: (0,) * len(shape),
                                          pipeline_mode=pl.Buffered(1))
    return pl.pallas_call(
        _ffn_kernel,
        grid=(nb, nr),
        in_specs=[
            pl.BlockSpec((1, FFN_ROWS, D_MODEL), main_idx),
            pl.BlockSpec((1, FFN_HALO, D_MODEL), halo_idx),
            pl.BlockSpec((FFN_HALO, D_MODEL), lambda b, r: (META_ROWS // FFN_HALO - 1, 0)),
            pl.BlockSpec((1, FFN_ROWS, MLA_HEADS * V_HEAD), main_idx),
            pl.BlockSpec((1, FFN_HALO, MLA_HEADS * V_HEAD), halo_idx),
            pl.BlockSpec((1, FFN_HALO, MLA_HEADS * V_HEAD), meta_idx),
            pl.BlockSpec((1, FFN_ROWS, DN_WIDTH), main_idx),
            pl.BlockSpec((1, FFN_HALO, DN_WIDTH), halo_idx),
            pl.BlockSpec((1, FFN_HALO, DN_WIDTH), meta_idx),
            resident((D_MODEL, D_MODEL)),
            resident((1, D_MODEL)),
            resident((D_MODEL, D_FF)),
            resident((D_MODEL, D_FF)),
            resident((3, D_FF)),
            resident((1, D_FF)),
            resident((D_FF, D_MODEL)),
        ],
        out_specs=pl.BlockSpec((1, FFN_ROWS, D_MODEL), main_idx),
        out_shape=jax.ShapeDtypeStruct((nb, SEQ, D_MODEL), F32),
        scratch_shapes=[
            pltpu.VMEM((FFN_HALO + FFN_ROWS, D_MODEL), BF16),
            pltpu.VMEM((FFN_HALO + FFN_ROWS, FF_BLK), F32),
            pltpu.VMEM((FFN_HALO + FFN_ROWS, FF_BLK), F32),
            pltpu.VMEM((FFN_ROWS, FF_BLK), F32),
            pltpu.VMEM((FFN_ROWS, FF_BLK), F32),
            pltpu.VMEM((FFN_ROWS, D_FF), BF16),
        ],
        compiler_params=_cparams(("arbitrary", "arbitrary")),
        name="outproj_ffn",
    )(x, x, hp_meta, yax, yax, yam, ydx, ydx, ydm, wo, nw, wg, wu, cw, cb, wd)


def _rot_cols(w):
    half = QK_ROPE // 2
    return jnp.concatenate([-w[..., half:], w[..., :half]], axis=-1)


def _swap_halves(w):
    half = QK_ROPE // 2
    return jnp.concatenate([w[..., half:], w[..., :half]], axis=-1)


def _pad_lanes(v, n=128):
    return jnp.pad(v.astype(F32), (0, n - v.shape[0])).reshape(1, n)


def _layer(x, hp_meta, l, attn_norm_w, w_in, q_a_norm_w, w_q_b, kv_a_norm_w, w_kv_b, q_norm_w,
           k_norm_w, mla_out_norm_w, dn_conv_w, dn_A_log, dn_dt_bias, dn_out_norm_w, w_out,
           ffn_norm_w, w_gate, w_up, ffn_conv_w, ffn_conv_b, w_down):
    nb = x.shape[0]
    c1 = Q_LORA
    c2 = c1 + KV_LORA
    c3 = c2 + QK_ROPE
    c4 = c3 + 3 * DN_WIDTH
    c5 = c4 + DN_WIDTH
    win = w_in[l]
    k_pe_w = win[:, c2:c3]
    w1 = jnp.concatenate(
        [win[:, :c2], k_pe_w, _rot_cols(k_pe_w), win[:, c3:c5], win[:, c5:],
         jnp.zeros((D_MODEL, N_AB - 2 * DN_HEADS), F32)], axis=1).astype(BF16)

    wqb = w_q_b[l].reshape(Q_LORA, MLA_HEADS, QK_HEAD)
    wqt = jnp.concatenate([wqb[..., :QK_NOPE], wqb[..., QK_NOPE:], _rot_cols(wqb[..., QK_NOPE:])],
                          axis=-1).reshape(Q_LORA, MLA_HEADS * HEAD_SLOT).T.astype(BF16)
    wkvb = w_kv_b[l].reshape(KV_LORA, MLA_HEADS, QK_NOPE + V_HEAD)
    wk = wkvb[..., :QK_NOPE].reshape(KV_LORA, MLA_HEADS * QK_NOPE).astype(BF16)
    wvt = wkvb[..., QK_NOPE:].reshape(KV_LORA, MLA_HEADS * V_HEAD).T.astype(BF16)
    qn = q_norm_w[l].astype(F32)
    kn = k_norm_w[l].astype(F32)
    qnn = qn[:QK_NOPE].reshape(QK_NOPE, 1)
    knn = kn[:QK_NOPE].reshape(1, 128)
    qrw = jnp.concatenate([qn[QK_NOPE:], _swap_halves(qn[QK_NOPE:])]).reshape(2 * QK_ROPE, 1)
    krw = jnp.concatenate([kn[QK_NOPE:], _swap_halves(kn[QK_NOPE:])]).reshape(1, 128)

    half = QK_ROPE // 2
    inv_freq = ROPE_THETA ** (-jnp.arange(half, dtype=F32) / half)
    pos = (jnp.arange(ATT_ROWS, dtype=jnp.int32) - ATT_META_VALID).astype(F32)
    ang = pos[:, None] * inv_freq[None, :]
    cs = jnp.concatenate([jnp.cos(ang), jnp.cos(ang), jnp.sin(ang), jnp.sin(ang)], axis=1)
    cst = cs.T

    nw1 = attn_norm_w[l].astype(F32).reshape(1, D_MODEL)
    latm, dnm, abm = _inproj(hp_meta, nw1, w1, META_ROWS)
    latx, dnx, abx = _inproj(x.reshape(nb * SEQ, D_MODEL), nw1, w1, 512)
    latx = latx.reshape(nb, SEQ, N_LAT)
    dnx = dnx.reshape(nb, SEQ, N_DN)
    abx = abx.reshape(nb, SEQ, N_AB)

    yax, yam = _mla(latx, latm, cs, cst, q_a_norm_w[l].astype(F32).reshape(1, Q_LORA), wqt,
                    kv_a_norm_w[l].astype(F32).reshape(1, KV_LORA), wk, wvt,
                    qnn, qrw, knn, krw, mla_out_norm_w[l].astype(F32).reshape(V_HEAD, 1))
    ydx, ydm = _deltanet(dnx, dnm, abx, abm, dn_conv_w[l].astype(F32),
                         _pad_lanes(dn_A_log[l]), _pad_lanes(dn_dt_bias[l]),
                         dn_out_norm_w[l].astype(F32).reshape(1, DN_DIM))
    return _ffn(x, hp_meta, yax, yam, ydx, ydm, w_out[l].astype(BF16),
                ffn_norm_w[l].astype(F32).reshape(1, D_MODEL), w_gate[l].astype(BF16),
                w_up[l].astype(BF16), ffn_conv_w[l].astype(F32),
                ffn_conv_b[l].astype(F32).reshape(1, D_FF), w_down[l].astype(BF16))


def kernel(x, meta_tokens, attn_norm_w, w_in, q_a_norm_w, w_q_b, kv_a_norm_w, w_kv_b, q_norm_w, k_norm_w, mla_out_norm_w, dn_conv_w, dn_A_log, dn_dt_bias, dn_out_norm_w, w_out, ffn_norm_w, w_gate, w_up, ffn_conv_w, ffn_conv_b, w_down):
    assert x.shape[1:] == (SEQ, D_MODEL) and w_in.shape[0] == 1
    hp_meta = jnp.concatenate([jnp.zeros((META_PAD, D_MODEL), x.dtype), meta_tokens.astype(x.dtype)], axis=0)
    return _layer(x, hp_meta, 0, attn_norm_w, w_in, q_a_norm_w, w_q_b, kv_a_norm_w, w_kv_b, q_norm_w,
                  k_norm_w, mla_out_norm_w, dn_conv_w, dn_A_log, dn_dt_bias, dn_out_norm_w, w_out,
                  ffn_norm_w, w_gate, w_up, ffn_conv_w, ffn_conv_b, w_down)
```

```python
import functools
import math

import jax
import jax.numpy as jnp
from jax import lax
from jax.experimental import pallas as pl
from jax.experimental.pallas import tpu as pltpu

F32 = jnp.float32
BF16 = jnp.bfloat16

D_MODEL = 1024
SEQ = 2048
N_META = 16
META_ROWS = 64
META_PAD = META_ROWS - N_META

MLA_HEADS = 4
QK_NOPE = 128
QK_ROPE = 64
QK_HEAD = QK_NOPE + QK_ROPE
V_HEAD = 128
Q_LORA = 256
KV_LORA = 256
ROPE_THETA = 10000.0
HEAD_SLOT = 256

DN_HEADS = 4
DN_DIM = 128
DN_WIDTH = DN_HEADS * DN_DIM
DN_CHUNK = 64
DN_STACK = DN_HEADS * DN_CHUNK
DN_CONV = 4

D_FF = 2816
FF_BLK = 256
NORM_EPS = 1e-6

C_LAT = 0
N_LAT = Q_LORA + KV_LORA + 2 * QK_ROPE
C_DN = N_LAT
N_DN = 4 * DN_WIDTH
C_AB = C_DN + N_DN
N_AB = 128
N_PROJ = C_AB + N_AB

VMEM_LIMIT = 56 * 1024 * 1024


def _cparams(sem):
    return pltpu.CompilerParams(dimension_semantics=sem, vmem_limit_bytes=VMEM_LIMIT)


def _rms(x, w):
    return x * lax.rsqrt(jnp.mean(x * x, axis=-1, keepdims=True) + NORM_EPS) * w


def _dot(a, b):
    return jnp.dot(a, b, preferred_element_type=F32)


def _dot_nt(a, b):
    return lax.dot_general(a, b, (((1,), (1,)), ((), ())), preferred_element_type=F32)


def _dot_tn(a, b):
    return lax.dot_general(a, b, (((0,), (0,)), ((), ())), preferred_element_type=F32)


def _silu(x):
    return x * jax.nn.sigmoid(x)


CONV_HIST = 16
INPROJ_ROWS = 1024
INPROJ_SUB = 256


def _inproj_kernel(tiles_per_seq, x_ref, nw_ref, w_ref, qaw_ref, kvaw_ref, cw_ref, hist_in_ref,
                   lat_ref, dn_ref, ab_ref, tail_ref, hist_s, p0_s, p1_s):
    n = x_ref.shape[0]
    sub = min(INPROJ_SUB, n)
    p_bufs = (p0_s, p1_s)

    @pl.when(pl.program_id(0) % tiles_per_seq == 0)
    def _():
        hist_s[...] = hist_in_ref[...]

    def project(r):
        u = _rms(x_ref[r * sub:(r + 1) * sub, :], nw_ref[...]).astype(BF16)
        p_bufs[r % 2][0:sub, :] = _dot(u, w_ref[...])

    def post(r):
        p = p_bufs[r % 2]
        rows = slice(r * sub, (r + 1) * sub)
        lat_ref[rows, 0:Q_LORA] = _rms(p[0:sub, 0:Q_LORA], qaw_ref[...]).astype(BF16)
        lat_ref[rows, Q_LORA:Q_LORA + KV_LORA] = _rms(p[0:sub, Q_LORA:Q_LORA + KV_LORA],
                                                      kvaw_ref[...]).astype(BF16)
        lat_ref[rows, Q_LORA + KV_LORA:N_LAT] = p[0:sub, Q_LORA + KV_LORA:N_LAT].astype(BF16)
        ab_ref[rows, :] = p[0:sub, C_AB:C_AB + N_AB]
        pre = p[0:sub, C_DN:C_DN + 3 * DN_WIDTH]
        full = jnp.concatenate([hist_s[...], pre], axis=0)
        hist_s[...] = pre[sub - CONV_HIST:, :]
        conv = cw_ref[DN_CONV - 1:DN_CONV, :] * pre
        for s in range(1, DN_CONV):
            conv = conv + cw_ref[DN_CONV - 1 - s:DN_CONV - s, :] * pltpu.roll(full, s, 0)[CONV_HIST:, :]
        act = _silu(conv)
        for h in range(2 * DN_HEADS):
            a = act[:, h * DN_DIM:(h + 1) * DN_DIM]
            a = a * lax.rsqrt(jnp.sum(a * a, axis=-1, keepdims=True) + NORM_EPS)
            if h < DN_HEADS:
                a = a * (1.0 / math.sqrt(DN_DIM))
            dn_ref[rows, h * DN_DIM:(h + 1) * DN_DIM] = a.astype(BF16)
        dn_ref[rows, 2 * DN_WIDTH:3 * DN_WIDTH] = act[:, 2 * DN_WIDTH:].astype(BF16)
        dn_ref[rows, 3 * DN_WIDTH:] = _silu(p[0:sub, C_DN + 3 * DN_WIDTH:C_DN + N_DN]).astype(BF16)

    project(0)
    for r in range(n // sub):
        if r + 1 < n // sub:
            project(r + 1)
        post(r)
    tail_ref[...] = hist_s[...]


def _inproj(x2d, nw, w, qaw, kvaw, cw, hist_in, row_tile, tiles_per_seq):
    rows = x2d.shape[0]
    grid = (rows // row_tile,)
    const = lambda shape: pl.BlockSpec(shape, lambda i: (0,) * len(shape))
    return pl.pallas_call(
        functools.partial(_inproj_kernel, tiles_per_seq),
        grid=grid,
        in_specs=[
            pl.BlockSpec((row_tile, D_MODEL), lambda i: (i, 0)),
            const((1, D_MODEL)),
            const((D_MODEL, N_PROJ)),
            const((1, Q_LORA)),
            const((1, KV_LORA)),
            const((DN_CONV, 3 * DN_WIDTH)),
            const((CONV_HIST, 3 * DN_WIDTH)),
        ],
        out_specs=[
            pl.BlockSpec((row_tile, N_LAT), lambda i: (i, 0)),
            pl.BlockSpec((row_tile, N_DN), lambda i: (i, 0)),
            pl.BlockSpec((row_tile, N_AB), lambda i: (i, 0)),
            pl.BlockSpec((CONV_HIST, 3 * DN_WIDTH), lambda i: (i, 0)),
        ],
        out_shape=[
            jax.ShapeDtypeStruct((rows, N_LAT), BF16),
            jax.ShapeDtypeStruct((rows, N_DN), BF16),
            jax.ShapeDtypeStruct((rows, N_AB), F32),
            jax.ShapeDtypeStruct((grid[0] * CONV_HIST, 3 * DN_WIDTH), F32),
        ],
        scratch_shapes=[pltpu.VMEM((CONV_HIST, 3 * DN_WIDTH), F32),
                        pltpu.VMEM((min(INPROJ_SUB, row_tile), N_PROJ), F32),
                        pltpu.VMEM((min(INPROJ_SUB, row_tile), N_PROJ), F32)],
        compiler_params=_cparams(("arbitrary",)),
        name="inproj",
    )(x2d, nw, w, qaw, kvaw, cw, hist_in)


ATT_TQ = 256
ATT_PROJ_ROWS = 512
ATT_META = 2 * META_ROWS
ATT_META_VALID = ATT_META - N_META
ATT_ROWS = ATT_META + SEQ
NEG_INF = float("-inf")


def _mla_kernel(latx_ref, latm_ref, cs_ref, cst_ref, wqt_ref, wk_ref, wvt_ref,
                qnn_ref, qrw_ref, knn_ref, krw_ref, onw_ref,
                yx_ref, ym_ref, km_s, qm_s, vm_s, kx_s, qx_s, vx_s, acc_s, stm_s, st0_s, st1_s, p_s):
    low = lax.broadcasted_iota(jnp.int32, (1, 128), 1) < QK_ROPE
    scale = 1.0 / math.sqrt(QK_HEAD)

    def project(lat, cs, cst):
        nrows = lat.shape[0]
        qn = lat[:, 0:Q_LORA]
        kvn = lat[:, Q_LORA:Q_LORA + KV_LORA]
        pe = lat[:, Q_LORA + KV_LORA:N_LAT]
        qt = _dot(wqt_ref[...], qn.T.astype(BF16))
        vt = _dot(wvt_ref[...], kvn.T.astype(BF16)).astype(BF16)
        kn = _dot(kvn.astype(BF16), wk_ref[...])
        a = pe * (cs * krw_ref[...])
        k_rope = jnp.where(low, a + pltpu.roll(a, QK_ROPE, 1), 0.0)
        pe_ss = jnp.sum(jnp.where(low, pe * pe, 0.0), axis=-1, keepdims=True)
        cos_t = cst[0:QK_ROPE, :]
        sin_t = cst[QK_ROPE:2 * QK_ROPE, :]
        k_parts = []
        q_parts = []
        for h in range(MLA_HEADS):
            nope = kn[:, h * QK_NOPE:(h + 1) * QK_NOPE]
            rs = lax.rsqrt((jnp.sum(nope * nope, axis=-1, keepdims=True) + pe_ss) * (1.0 / QK_HEAD) + NORM_EPS)
            k_parts += [(nope * rs * knn_ref[...]).astype(BF16), (k_rope * rs).astype(BF16)]
            r0 = h * HEAD_SLOT
            qnope = qt[r0:r0 + QK_NOPE, :]
            qrope = qt[r0 + QK_NOPE:r0 + QK_HEAD, :]
            qrot = qt[r0 + QK_HEAD:r0 + HEAD_SLOT, :]
            ssq = (jnp.sum(qnope * qnope, axis=0, keepdims=True)
                   + jnp.sum(qrope * qrope, axis=0, keepdims=True))
            rsq = lax.rsqrt(ssq * (1.0 / QK_HEAD) + NORM_EPS) * scale
            roped = (qrope * (qrw_ref[0:QK_ROPE, :] * cos_t)
                     + qrot * (qrw_ref[QK_ROPE:2 * QK_ROPE, :] * sin_t))
            q_parts += [(qnope * qnn_ref[...] * rsq).astype(BF16), (roped * rsq).astype(BF16),
                        jnp.zeros((HEAD_SLOT - QK_HEAD, nrows), BF16)]
        return jnp.concatenate(k_parts, axis=1), jnp.concatenate(q_parts, axis=0), vt

    latm = jnp.concatenate([jnp.zeros((META_ROWS, N_LAT), F32), latm_ref[...].astype(F32)], axis=0)
    km_s[...], qm_s[...], vm_s[...] = project(latm, cs_ref[0:ATT_META, :], cst_ref[:, 0:ATT_META])
    for c in range(SEQ // ATT_PROJ_ROWS):
        r0 = c * ATT_PROJ_ROWS
        k, qt, vt = project(latx_ref[0, r0:r0 + ATT_PROJ_ROWS, :].astype(F32),
                            cs_ref[ATT_META + r0:ATT_META + r0 + ATT_PROJ_ROWS, :],
                            cst_ref[:, ATT_META + r0:ATT_META + r0 + ATT_PROJ_ROWS])
        kx_s[r0:r0 + ATT_PROJ_ROWS, :] = k
        for t in range(ATT_PROJ_ROWS // ATT_TQ):
            qx_s[c * (ATT_PROJ_ROWS // ATT_TQ) + t] = qt[:, t * ATT_TQ:(t + 1) * ATT_TQ]
            vx_s[c * (ATT_PROJ_ROWS // ATT_TQ) + t] = vt[:, t * ATT_TQ:(t + 1) * ATT_TQ]

    def first_block(st):
        m = jnp.max(st, axis=0, keepdims=True)
        p = jnp.exp(st - m)
        return p.astype(BF16), m, jnp.sum(p, axis=0, keepdims=True)

    def next_block(st, m, l):
        m_new = jnp.maximum(m, jnp.max(st, axis=0, keepdims=True))
        alpha = jnp.exp(m - m_new)
        p = jnp.exp(st - m_new)
        return p.astype(BF16), m_new, alpha * l + jnp.sum(p, axis=0, keepdims=True), alpha

    def finish(acc_t, l):
        o = acc_t * (1.0 / l)
        o = o * lax.rsqrt(jnp.mean(o * o, axis=0, keepdims=True) + NORM_EPS) * onw_ref[...]
        return o.T.astype(BF16)

    hs = lambda h: slice(h * HEAD_SLOT, (h + 1) * HEAD_SLOT)
    vs = lambda h: slice(h * V_HEAD, (h + 1) * V_HEAD)

    mkey = lax.broadcasted_iota(jnp.int32, (ATT_META, ATT_META), 0)
    mqry = lax.broadcasted_iota(jnp.int32, (ATT_META, ATT_META), 1)
    meta_mask = (mkey <= mqry) & ((mkey >= ATT_META_VALID) | (mkey == mqry))
    for h in range(MLA_HEADS):
        st = jnp.where(meta_mask, _dot(km_s[:, hs(h)], qm_s[hs(h), :]), NEG_INF)
        p, m, l = first_block(st)
        o = finish(_dot(vm_s[vs(h), :], p), l)
        ym_ref[0, :, vs(h)] = o[META_ROWS:, :]

    meta_key_ok = lax.broadcasted_iota(jnp.int32, (ATT_META, ATT_TQ), 0) >= ATT_META_VALID
    diag_mask = (lax.broadcasted_iota(jnp.int32, (ATT_TQ, ATT_TQ), 0)
                 <= lax.broadcasted_iota(jnp.int32, (ATT_TQ, ATT_TQ), 1))

    st_bufs = (st0_s, st1_s)

    for i in range(SEQ // ATT_TQ):
        def scores(buf, k_of, nk):
            for h in range(MLA_HEADS):
                buf[h, 0:nk, :] = _dot(k_of(h), qx_s[i, hs(h), :])

        def absorb(buf, v_of, nk, mask, ms, ls):
            ms2, ls2, alphas = [], [], []
            for h in range(MLA_HEADS):
                st = buf[h, 0:nk, :]
                if mask is not None:
                    st = jnp.where(mask, st, NEG_INF)
                p, m, l, alpha = next_block(st, ms[h], ls[h])
                p_s[h, 0:nk, :] = p
                ms2.append(m)
                ls2.append(l)
                alphas.append(alpha)
            for h in range(MLA_HEADS):
                acc_s[h] = acc_s[h] * alphas[h] + _dot(v_of(h), p_s[h, 0:nk, :])
            return tuple(ms2), tuple(ls2)

        def k_blk(j):
            k0 = j * ATT_TQ
            if not isinstance(j, int):
                k0 = pl.multiple_of(k0, ATT_TQ)
            return lambda h: kx_s[pl.ds(k0, ATT_TQ), hs(h)]

        v_blk = lambda j: (lambda h: vx_s[j, vs(h), :])

        acc_s[...] = jnp.zeros_like(acc_s)
        ms = (jnp.full((1, ATT_TQ), NEG_INF, F32),) * MLA_HEADS
        ls = (jnp.zeros((1, ATT_TQ), F32),) * MLA_HEADS
        scores(stm_s, lambda h: km_s[:, hs(h)], ATT_META)
        scores(st_bufs[0], k_blk(0), ATT_TQ)
        ms, ls = absorb(stm_s, lambda h: vm_s[vs(h), :], ATT_META, meta_key_ok, ms, ls)

        def pair(t, st_):
            ms, ls = st_
            scores(st_bufs[1], k_blk(2 * t + 1), ATT_TQ)
            ms, ls = absorb(st_bufs[0], v_blk(2 * t), ATT_TQ, None, ms, ls)
            scores(st_bufs[0], k_blk(2 * t + 2), ATT_TQ)
            return absorb(st_bufs[1], v_blk(2 * t + 1), ATT_TQ, None, ms, ls)

        if i // 2 > 0:
            ms, ls = lax.fori_loop(0, i // 2, pair, (ms, ls))
        if i % 2 == 1:
            scores(st_bufs[1], k_blk(i), ATT_TQ)
            ms, ls = absorb(st_bufs[0], v_blk(i - 1), ATT_TQ, None, ms, ls)
        ms, ls = absorb(st_bufs[i % 2], v_blk(i), ATT_TQ, diag_mask, ms, ls)
        for h in range(MLA_HEADS):
            yx_ref[0, i * ATT_TQ:(i + 1) * ATT_TQ, vs(h)] = finish(acc_s[h], ls[h])


def _mla(latx, latm, cs, cst, wqt, wk, wvt, qnn, qrw, knn, krw, onw):
    nb = latx.shape[0]
    const = lambda shape: pl.BlockSpec(shape, lambda b: (0,) * len(shape))
    nq = SEQ // ATT_TQ
    return pl.pallas_call(
        _mla_kernel,
        grid=(nb,),
        in_specs=[
            pl.BlockSpec((1, SEQ, N_LAT), lambda b: (b, 0, 0)),
            const((META_ROWS, N_LAT)),
            const((ATT_ROWS, 128)),
            const((128, ATT_ROWS)),
            const((MLA_HEADS * HEAD_SLOT, Q_LORA)),
            const((KV_LORA, MLA_HEADS * QK_NOPE)),
            const((MLA_HEADS * V_HEAD, KV_LORA)),
            const((QK_NOPE, 1)), const((2 * QK_ROPE, 1)), const((1, 128)), const((1, 128)),
            const((V_HEAD, 1)),
        ],
        out_specs=[
            pl.BlockSpec((1, SEQ, MLA_HEADS * V_HEAD), lambda b: (b, 0, 0)),
            pl.BlockSpec((1, META_ROWS, MLA_HEADS * V_HEAD), lambda b: (b, 0, 0)),
        ],
        out_shape=[
            jax.ShapeDtypeStruct((nb, SEQ, MLA_HEADS * V_HEAD), BF16),
            jax.ShapeDtypeStruct((nb, META_ROWS, MLA_HEADS * V_HEAD), BF16),
        ],
        scratch_shapes=[
            pltpu.VMEM((ATT_META, MLA_HEADS * HEAD_SLOT), BF16),
            pltpu.VMEM((MLA_HEADS * HEAD_SLOT, ATT_META), BF16),
            pltpu.VMEM((MLA_HEADS * V_HEAD, ATT_META), BF16),
            pltpu.VMEM((SEQ, MLA_HEADS * HEAD_SLOT), BF16),
            pltpu.VMEM((nq, MLA_HEADS * HEAD_SLOT, ATT_TQ), BF16),
            pltpu.VMEM((nq, MLA_HEADS * V_HEAD, ATT_TQ), BF16),
            pltpu.VMEM((MLA_HEADS, V_HEAD, ATT_TQ), F32),
            pltpu.VMEM((MLA_HEADS, ATT_META, ATT_TQ), F32),
            pltpu.VMEM((MLA_HEADS, ATT_TQ, ATT_TQ), F32),
            pltpu.VMEM((MLA_HEADS, ATT_TQ, ATT_TQ), F32),
            pltpu.VMEM((MLA_HEADS, ATT_TQ, ATT_TQ), BF16),
        ],
        compiler_params=_cparams(("arbitrary",)),
        name="mla",
    )(latx, latm, cs, cst, wqt, wk, wvt, qnn, qrw, knn, krw, onw)


DN_GROUP = 4
DN_GROUP_ROWS = DN_GROUP * DN_CHUNK


def _deltanet_kernel(dnx_ref, dnm_ref, abx_ref, abm_ref, alog_ref, dtb_ref, onw_ref,
                     yx_ref, ym_ref, s_s, af_s, t_s, pa_s, pb_s, rhs_s,
                     uw0_s, qk0_s, qg0_s, kd0_s, el0_s, uw1_s, qk1_s, qg1_s, kd1_s, el1_s,
                     smeta_s, ymeta_s):
    C = DN_CHUNK
    R = DN_STACK
    row = lax.broadcasted_iota(jnp.int32, (R, R), 0)
    col = lax.broadcasted_iota(jnp.int32, (R, R), 1)
    same = lambda sh: jnp.right_shift(row, sh) == jnp.right_shift(col, sh)
    m_incl = same(6) & (col <= row)
    m_strict = same(6) & (col < row)
    m_d16 = m_strict & same(4)
    m_l32 = m_strict & same(5) & jnp.logical_not(same(4))
    m_l64 = m_strict & jnp.logical_not(same(5))
    eye = (row == col).astype(F32)
    neg_a = -jnp.exp(alog_ref[...])

    s_s[...] = jnp.zeros_like(s_s)

    def stack(x):
        return jnp.concatenate([x[:, h * DN_DIM:(h + 1) * DN_DIM] for h in range(DN_HEADS)], axis=0)

    def stack_col(x, c0):
        return jnp.concatenate(
            [jnp.broadcast_to(x[:, c0 + h:c0 + h + 1], (C, DN_DIM)) for h in range(DN_HEADS)], axis=0)

    def bdot(a, b):
        return _dot(a.astype(BF16), b.astype(BF16))

    def gates(ab, row_ok):
        xa = ab + dtb_ref[...]
        softplus = jnp.maximum(xa, 0.0) + jnp.log(1.0 + jnp.exp(-jnp.abs(xa)))
        g = neg_a * softplus
        beta = jax.nn.sigmoid(ab)
        if row_ok is not None:
            g = jnp.where(row_ok, g, 0.0)
            beta = jnp.where(row_ok, beta, 0.0)
        pos = lax.broadcasted_iota(jnp.int32, g.shape, 0) & (C - 1)
        gc = g
        for s in (1, 2, 4, 8, 16, 32):
            gc = gc + jnp.where(pos >= s, pltpu.roll(gc, s, 0), 0.0)
        return gc, beta

    def phase_a(acts, gcl, betal, buf):
        uw_b, qk_b, qg_b, kd_b, el_b = buf
        n = len(acts)
        for c in range(n):
            act, gc, beta = acts[c], gcl[c], betal[c]
            q = stack(act[:, 0:DN_WIDTH])
            k = stack(act[:, DN_WIDTH:2 * DN_WIDTH])
            v = stack(act[:, 2 * DN_WIDTH:3 * DN_WIDTH])
            gcs = stack_col(gc, 0)
            gls = stack_col(jnp.broadcast_to(gc[C - 1:C, :], (C, 128)), 0)
            bs = stack_col(beta, DN_HEADS)
            grow = gcs.T[0:1, :]
            dec = jnp.exp(jnp.where(m_incl, gcs[:, 0:1] - grow, NEG_INF))
            kb = k * bs
            kbf = k.astype(BF16)
            eg = jnp.exp(gcs)
            af_s[c] = _dot_nt(kb.astype(BF16), kbf) * dec
            qk_b[c] = jnp.where(m_incl, _dot_nt(q.astype(BF16), kbf) * dec, 0.0).astype(BF16)
            rhs_s[c] = jnp.concatenate([v * bs, kb * eg], axis=1).astype(BF16)
            qg_b[c] = (q * eg).astype(BF16)
            kd_b[c] = (k * jnp.exp(gls - gcs)).astype(BF16)
            el_b[c] = jnp.exp(gls)
            yield

        for c in range(n):
            b1 = jnp.where(m_d16, af_s[c], 0.0)
            pa_s[c] = b1.astype(BF16)
            t_s[c] = eye - b1
        for c in range(n):
            pb_s[c] = _dot(pa_s[c], pa_s[c]).astype(BF16)
        yield
        for c in range(n):
            t_s[c] = t_s[c] + _dot(t_s[c].astype(BF16), pb_s[c])
            pa_s[c] = _dot(pb_s[c], pb_s[c]).astype(BF16)
        yield
        for c in range(n):
            t_s[c] = t_s[c] + _dot(t_s[c].astype(BF16), pa_s[c])
            pb_s[c] = _dot(pa_s[c], pa_s[c]).astype(BF16)
        yield
        for c in range(n):
            t_s[c] = t_s[c] + _dot(t_s[c].astype(BF16), pb_s[c])
        yield
        for m_low in (m_l32, m_l64):
            for c in range(n):
                pa_s[c] = _dot(jnp.where(m_low, af_s[c], 0.0).astype(BF16), t_s[c].astype(BF16)).astype(BF16)
            yield
            for c in range(n):
                t_s[c] = t_s[c] - _dot(t_s[c].astype(BF16), pa_s[c])
            yield
        for c in range(n):
            uw_b[c] = _dot(t_s[c].astype(BF16), rhs_s[c])

    def phase_b(zs, buf, outs):
        uw_b, qk_b, qg_b, kd_b, el_b = buf
        for c in range(len(zs)):
            vnew = []
            o_inter = []
            for h in range(DN_HEADS):
                r0 = h * C
                s_h = s_s[h]
                sb = s_h.astype(BF16)
                vn = uw_b[c, r0:r0 + C, 0:DN_DIM] - _dot(uw_b[c, r0:r0 + C, DN_DIM:].astype(BF16), sb)
                o_inter.append(_dot(qg_b[c, r0:r0 + C, :], sb))
                s_s[h] = s_h * el_b[c, r0:r0 + 1, :] + _dot_tn(kd_b[c, r0:r0 + C, :], vn.astype(BF16))
                vnew.append(vn)
            o = jnp.concatenate(o_inter, axis=0) + _dot(qk_b[c], jnp.concatenate(vnew, axis=0).astype(BF16))
            o = _rms(o, onw_ref[...])
            out = jnp.concatenate([o[h * C:(h + 1) * C, :] for h in range(DN_HEADS)], axis=1) * zs[c]
            outs.append(out.astype(BF16))
            yield

    def run(*gens_and_steps):
        gens = [g for g, _ in gens_and_steps]
        lens = [s for _, s in gens_and_steps]
        done = [0] * len(gens)
        alive = [True] * len(gens)
        while any(alive):
            i = min((j for j in range(len(gens)) if alive[j]), key=lambda j: (done[j] + 0.5) / lens[j])
            try:
                next(gens[i])
                done[i] += 1
            except StopIteration:
                alive[i] = False

    bufs = ((uw0_s, qk0_s, qg0_s, kd0_s, el0_s), (uw1_s, qk1_s, qg1_s, kd1_s, el1_s))

    @pl.when(pl.program_id(0) == 0)
    def _():
        meta_ok = lax.broadcasted_iota(jnp.int32, (C, 128), 0) >= META_PAD
        dn0 = dnm_ref[...]
        act0 = dn0[:, 0:3 * DN_WIDTH].astype(F32)
        gc0, beta0 = gates(abm_ref[...], meta_ok)
        run((phase_a([act0], [gc0], [beta0], bufs[1]), 1))
        out0 = []
        run((phase_b([dn0[:, 3 * DN_WIDTH:].astype(F32)], bufs[1], out0), 1))
        ymeta_s[...] = out0[0]
        smeta_s[...] = s_s[...]

    s_s[...] = smeta_s[...]
    ym_ref[0] = ymeta_s[...]

    rows = [slice(c * C, (c + 1) * C) for c in range(DN_GROUP)]
    a_steps = DN_GROUP + 10
    b_steps = DN_GROUP + 1

    def group_a(p, buf):
        r0 = p * DN_GROUP_ROWS
        if not isinstance(p, int):
            r0 = pl.multiple_of(r0, DN_GROUP_ROWS)
        act = dnx_ref[0, pl.ds(r0, DN_GROUP_ROWS), 0:3 * DN_WIDTH].astype(F32)
        gc, beta = gates(abx_ref[0, pl.ds(r0, DN_GROUP_ROWS), :], None)
        yield
        yield from phase_a([act[r, :] for r in rows], [gc[r, :] for r in rows], [beta[r, :] for r in rows], buf)

    def group_b(p, buf):
        r0 = p * DN_GROUP_ROWS
        if not isinstance(p, int):
            r0 = pl.multiple_of(r0, DN_GROUP_ROWS)
        z = dnx_ref[0, pl.ds(r0, DN_GROUP_ROWS), 3 * DN_WIDTH:].astype(F32)
        outs = []
        yield from phase_b([z[r, :] for r in rows], buf, outs)
        yx_ref[0, pl.ds(r0, DN_GROUP_ROWS), :] = jnp.concatenate(outs, axis=0)

    n_groups = SEQ // DN_GROUP_ROWS
    run((group_a(0, bufs[0]), a_steps))

    def body(kk, carry):
        p = 2 * kk
        run((group_a(p + 1, bufs[1]), a_steps), (group_b(p, bufs[0]), b_steps))
        run((group_a(p + 2, bufs[0]), a_steps), (group_b(p + 1, bufs[1]), b_steps))
        return carry

    lax.fori_loop(0, n_groups // 2 - 1, body, 0)
    run((group_a(n_groups - 1, bufs[1]), a_steps), (group_b(n_groups - 2, bufs[0]), b_steps))
    run((group_b(n_groups - 1, bufs[1]), b_steps))


def _deltanet(dnx, dnm, abx, abm, alog, dtb, onw):
    nb = dnx.shape[0]
    const = lambda shape: pl.BlockSpec(shape, lambda b: (0,) * len(shape))
    return pl.pallas_call(
        _deltanet_kernel,
        grid=(nb,),
        in_specs=[
            pl.BlockSpec((1, SEQ, N_DN), lambda b: (b, 0, 0)),
            const((META_ROWS, N_DN)),
            pl.BlockSpec((1, SEQ, N_AB), lambda b: (b, 0, 0)),
            const((META_ROWS, N_AB)),
            const((1, 128)), const((1, 128)), const((1, 128)),
        ],
        out_specs=[
            pl.BlockSpec((1, SEQ, DN_WIDTH), lambda b: (b, 0, 0)),
            pl.BlockSpec((1, META_ROWS, DN_WIDTH), lambda b: (b, 0, 0)),
        ],
        out_shape=[
            jax.ShapeDtypeStruct((nb, SEQ, DN_WIDTH), BF16),
            jax.ShapeDtypeStruct((nb, META_ROWS, DN_WIDTH), BF16),
        ],
        scratch_shapes=[
            pltpu.VMEM((DN_HEADS, DN_DIM, DN_DIM), F32),
            pltpu.VMEM((DN_GROUP, DN_STACK, DN_STACK), F32),
            pltpu.VMEM((DN_GROUP, DN_STACK, DN_STACK), F32),
            pltpu.VMEM((DN_GROUP, DN_STACK, DN_STACK), BF16),
            pltpu.VMEM((DN_GROUP, DN_STACK, DN_STACK), BF16),
            pltpu.VMEM((DN_GROUP, DN_STACK, 2 * DN_DIM), BF16),
        ] + 2 * [
            pltpu.VMEM((DN_GROUP, DN_STACK, 2 * DN_DIM), F32),
            pltpu.VMEM((DN_GROUP, DN_STACK, DN_STACK), BF16),
            pltpu.VMEM((DN_GROUP, DN_STACK, DN_DIM), BF16),
            pltpu.VMEM((DN_GROUP, DN_STACK, DN_DIM), BF16),
            pltpu.VMEM((DN_GROUP, DN_STACK, DN_DIM), F32),
        ] + [
            pltpu.VMEM((DN_HEADS, DN_DIM, DN_DIM), F32),
            pltpu.VMEM((META_ROWS, DN_WIDTH), BF16),
        ],
        compiler_params=_cparams(("arbitrary",)),
        name="deltanet",
    )(dnx, dnm, abx, abm, alog, dtb, onw)


FFN_ROWS = 512
FFN_HALO = 16
N_FF_BLK = D_FF // FF_BLK


def _ffn_kernel(x_ref, xh_ref, mh_ref, ya_ref, yah_ref, yam_ref, yd_ref, ydh_ref, ydm_ref,
                wo_ref, nw_ref, wg_ref, wu_ref, cw_ref, cb_ref, wd_ref,
                o_ref, u_s, g0_s, g1_s, up0_s, up1_s, act_s):
    r = pl.program_id(1)
    mixed = jnp.concatenate([ya_ref[0], yd_ref[0]], axis=1)
    h_mid = x_ref[0] + _dot(mixed, wo_ref[...])
    o_ref[0] = h_mid
    u_s[FFN_HALO:, :] = _rms(h_mid, nw_ref[...]).astype(BF16)
    first = r == 0
    mixed_h = jnp.concatenate([jnp.where(first, yam_ref[0], yah_ref[0]),
                               jnp.where(first, ydm_ref[0], ydh_ref[0])], axis=1)
    h_halo = jnp.where(first, mh_ref[...], xh_ref[0]) + _dot(mixed_h, wo_ref[...])
    u_s[0:FFN_HALO, :] = _rms(h_halo, nw_ref[...]).astype(BF16)

    g_bufs = (g0_s, g1_s)
    up_bufs = (up0_s, up1_s)

    def project(f):
        cols = slice(f * FF_BLK, (f + 1) * FF_BLK)
        g_bufs[f % 2][...] = _dot(u_s[...], wg_ref[:, cols])
        up_bufs[f % 2][...] = _dot(u_s[FFN_HALO:, :], wu_ref[:, cols])

    project(0)
    for f in range(N_FF_BLK):
        if f + 1 < N_FF_BLK:
            project(f + 1)
        g_s = g_bufs[f % 2]
        cols = slice(f * FF_BLK, (f + 1) * FF_BLK)
        gate = (cw_ref[2:3, cols] * g_s[FFN_HALO:, :]
                + cw_ref[1:2, cols] * g_s[FFN_HALO - 1:FFN_HALO - 1 + FFN_ROWS, :]
                + cw_ref[0:1, cols] * g_s[FFN_HALO - 2:FFN_HALO - 2 + FFN_ROWS, :]
                + cb_ref[:, cols])
        act_s[:, cols] = (_silu(gate) * up_bufs[f % 2][...]).astype(BF16)
    o_ref[0] += _dot(act_s[...], wd_ref[...])


def _ffn(x, hp_meta, yax, yam, ydx, ydm, wo, nw, wg, wu, cw, cb, wd):
    nb = x.shape[0]
    nr = SEQ // FFN_ROWS
    hb = FFN_ROWS // FFN_HALO
    halo_idx = lambda b, r: (b, jnp.maximum(r * hb - 1, 0), 0)
    meta_idx = lambda b, r: (b, META_ROWS // FFN_HALO - 1, 0)
    main_idx = lambda b, r: (b, r, 0)
    resident = lambda shape: pl.BlockSpec(shape, lambda b, r: (0,) * len(shape),
                                          pipeline_mode=pl.Buffered(1))
    return pl.pallas_call(
        _ffn_kernel,
        grid=(nb, nr),
        in_specs=[
            pl.BlockSpec((1, FFN_ROWS, D_MODEL), main_idx),
            pl.BlockSpec((1, FFN_HALO, D_MODEL), halo_idx),
            pl.BlockSpec((FFN_HALO, D_MODEL), lambda b, r: (META_ROWS // FFN_HALO - 1, 0)),
            pl.BlockSpec((1, FFN_ROWS, MLA_HEADS * V_HEAD), main_idx),
            pl.BlockSpec((1, FFN_HALO, MLA_HEADS * V_HEAD), halo_idx),
            pl.BlockSpec((1, FFN_HALO, MLA_HEADS * V_HEAD), meta_idx),
            pl.BlockSpec((1, FFN_ROWS, DN_WIDTH), main_idx),
            pl.BlockSpec((1, FFN_HALO, DN_WIDTH), halo_idx),
            pl.BlockSpec((1, FFN_HALO, DN_WIDTH), meta_idx),
            resident((D_MODEL, D_MODEL)),
            resident((1, D_MODEL)),
            resident((D_MODEL, D_FF)),
            resident((D_MODEL, D_FF)),
            resident((3, D_FF)),
            resident((1, D_FF)),
            resident((D_FF, D_MODEL)),
        ],
        out_specs=pl.BlockSpec((1, FFN_ROWS, D_MODEL), main_idx),
        out_shape=jax.ShapeDtypeStruct((nb, SEQ, D_MODEL), F32),
        scratch_shapes=[
            pltpu.VMEM((FFN_HALO + FFN_ROWS, D_MODEL), BF16),
            pltpu.VMEM((FFN_HALO + FFN_ROWS, FF_BLK), F32),
            pltpu.VMEM((FFN_HALO + FFN_ROWS, FF_BLK), F32),
            pltpu.VMEM((FFN_ROWS, FF_BLK), F32),
            pltpu.VMEM((FFN_ROWS, FF_BLK), F32),
            pltpu.VMEM((FFN_ROWS, D_FF), BF16),
        ],
        compiler_params=_cparams(("arbitrary", "arbitrary")),
        name="outproj_ffn",
    )(x, x, hp_meta, yax, yax, yam, ydx, ydx, ydm, wo, nw, wg, wu, cw, cb, wd)


def _rot_cols(w):
    half = QK_ROPE // 2
    return jnp.concatenate([-w[..., half:], w[..., :half]], axis=-1)


def _swap_halves(w):
    half = QK_ROPE // 2
    return jnp.concatenate([w[..., half:], w[..., :half]], axis=-1)


def _pad_lanes(v, n=128):
    return jnp.pad(v.astype(F32), (0, n - v.shape[0])).reshape(1, n)


def _layer(x, hp_meta, l, attn_norm_w, w_in, q_a_norm_w, w_q_b, kv_a_norm_w, w_kv_b, q_norm_w,
           k_norm_w, mla_out_norm_w, dn_conv_w, dn_A_log, dn_dt_bias, dn_out_norm_w, w_out,
           ffn_norm_w, w_gate, w_up, ffn_conv_w, ffn_conv_b, w_down):
    nb = x.shape[0]
    c1 = Q_LORA
    c2 = c1 + KV_LORA
    c3 = c2 + QK_ROPE
    c4 = c3 + 3 * DN_WIDTH
    c5 = c4 + DN_WIDTH
    win = w_in[l]
    k_pe_w = win[:, c2:c3]
    w1 = jnp.concatenate(
        [win[:, :c2], k_pe_w, _rot_cols(k_pe_w), win[:, c3:c5], win[:, c5:],
         jnp.zeros((D_MODEL, N_AB - 2 * DN_HEADS), F32)], axis=1).astype(BF16)

    wqb = w_q_b[l].reshape(Q_LORA, MLA_HEADS, QK_HEAD)
    wqt = jnp.concatenate([wqb[..., :QK_NOPE], wqb[..., QK_NOPE:], _rot_cols(wqb[..., QK_NOPE:])],
                          axis=-1).reshape(Q_LORA, MLA_HEADS * HEAD_SLOT).T.astype(BF16)
    wkvb = w_kv_b[l].reshape(KV_LORA, MLA_HEADS, QK_NOPE + V_HEAD)
    wk = wkvb[..., :QK_NOPE].reshape(KV_LORA, MLA_HEADS * QK_NOPE).astype(BF16)
    wvt = wkvb[..., QK_NOPE:].reshape(KV_LORA, MLA_HEADS * V_HEAD).T.astype(BF16)
    qn = q_norm_w[l].astype(F32)
    kn = k_norm_w[l].astype(F32)
    qnn = qn[:QK_NOPE].reshape(QK_NOPE, 1)
    knn = kn[:QK_NOPE].reshape(1, 128)
    qrw = jnp.concatenate([qn[QK_NOPE:], _swap_halves(qn[QK_NOPE:])]).reshape(2 * QK_ROPE, 1)
    krw = jnp.concatenate([kn[QK_NOPE:], _swap_halves(kn[QK_NOPE:])]).reshape(1, 128)

    half = QK_ROPE // 2
    inv_freq = ROPE_THETA ** (-jnp.arange(half, dtype=F32) / half)
    pos = (jnp.arange(ATT_ROWS, dtype=jnp.int32) - ATT_META_VALID).astype(F32)
    ang = pos[:, None] * inv_freq[None, :]
    cs = jnp.concatenate([jnp.cos(ang), jnp.cos(ang), jnp.sin(ang), jnp.sin(ang)], axis=1)
    cst = cs.T

    nw1 = attn_norm_w[l].astype(F32).reshape(1, D_MODEL)
    qaw = q_a_norm_w[l].astype(F32).reshape(1, Q_LORA)
    kvaw = kv_a_norm_w[l].astype(F32).reshape(1, KV_LORA)
    cw = dn_conv_w[l].astype(F32)
    no_hist = jnp.zeros((CONV_HIST, 3 * DN_WIDTH), F32)
    latm, dnm, abm, meta_tail = _inproj(hp_meta, nw1, w1, qaw, kvaw, cw, no_hist, META_ROWS, 1)
    latx, dnx, abx, _ = _inproj(x.reshape(nb * SEQ, D_MODEL), nw1, w1, qaw, kvaw, cw, meta_tail,
                                INPROJ_ROWS, SEQ // INPROJ_ROWS)
    latx = latx.reshape(nb, SEQ, N_LAT)
    dnx = dnx.reshape(nb, SEQ, N_DN)
    abx = abx.reshape(nb, SEQ, N_AB)

    yax, yam = _mla(latx, latm, cs, cst, wqt, wk, wvt,
                    qnn, qrw, knn, krw, mla_out_norm_w[l].astype(F32).reshape(V_HEAD, 1))
    ydx, ydm = _deltanet(dnx, dnm, abx, abm,
                         _pad_lanes(dn_A_log[l]), _pad_lanes(dn_dt_bias[l]),
                         dn_out_norm_w[l].astype(F32).reshape(1, DN_DIM))
    return _ffn(x, hp_meta, yax, yam, ydx, ydm, w_out[l].astype(BF16),
                ffn_norm_w[l].astype(F32).reshape(1, D_MODEL), w_gate[l].astype(BF16),
                w_up[l].astype(BF16), ffn_conv_w[l].astype(F32),
                ffn_conv_b[l].astype(F32).reshape(1, D_FF), w_down[l].astype(BF16))


def kernel(x, meta_tokens, attn_norm_w, w_in, q_a_norm_w, w_q_b, kv_a_norm_w, w_kv_b, q_norm_w, k_norm_w, mla_out_norm_w, dn_conv_w, dn_A_log, dn_dt_bias, dn_out_norm_w, w_out, ffn_norm_w, w_gate, w_up, ffn_conv_w, ffn_conv_b, w_down):
    assert x.shape[1:] == (SEQ, D_MODEL) and w_in.shape[0] == 1
    hp_meta = jnp.concatenate([jnp.zeros((META_PAD, D_MODEL), x.dtype), meta_tokens.astype(x.dtype)], axis=0)
    return _layer(x, hp_meta, 0, attn_norm_w, w_in, q_a_norm_w, w_q_b, kv_a_norm_w, w_kv_b, q_norm_w,
                  k_norm_w, mla_out_norm_w, dn_conv_w, dn_A_log, dn_dt_bias, dn_out_norm_w, w_out,
                  ffn_norm_w, w_gate, w_up, ffn_conv_w, ffn_conv_b, w_down)
```

```python
import functools
import math

import jax
import jax.numpy as jnp
from jax import lax
from jax.experimental import pallas as pl
from jax.experimental.pallas import tpu as pltpu

F32 = jnp.float32
BF16 = jnp.bfloat16

D_MODEL = 1024
SEQ = 2048
N_META = 16
META_ROWS = 64
META_PAD = META_ROWS - N_META

MLA_HEADS = 4
QK_NOPE = 128
QK_ROPE = 64
QK_HEAD = QK_NOPE + QK_ROPE
V_HEAD = 128
Q_LORA = 256
KV_LORA = 256
ROPE_THETA = 10000.0
HEAD_SLOT = 256

DN_HEADS = 4
DN_DIM = 128
DN_WIDTH = DN_HEADS * DN_DIM
DN_CHUNK = 64
DN_STACK = DN_HEADS * DN_CHUNK
DN_CONV = 4

D_FF = 2816
FF_BLK = 256
NORM_EPS = 1e-6

C_LAT = 0
N_LAT = Q_LORA + KV_LORA + 2 * QK_ROPE
C_DN = N_LAT
N_DN = 4 * DN_WIDTH
C_AB = C_DN + N_DN
N_AB = 128
N_PROJ = C_AB + N_AB

VMEM_LIMIT = 56 * 1024 * 1024


def _cparams(sem):
    return pltpu.CompilerParams(dimension_semantics=sem, vmem_limit_bytes=VMEM_LIMIT)


def _rms(x, w):
    return x * lax.rsqrt(jnp.mean(x * x, axis=-1, keepdims=True) + NORM_EPS) * w


def _dot(a, b):
    return jnp.dot(a, b, preferred_element_type=F32)


def _dot_nt(a, b):
    return lax.dot_general(a, b, (((1,), (1,)), ((), ())), preferred_element_type=F32)


def _dot_tn(a, b):
    return lax.dot_general(a, b, (((0,), (0,)), ((), ())), preferred_element_type=F32)


def _silu(x):
    return x * jax.nn.sigmoid(x)


CONV_HIST = 16
INPROJ_ROWS = 1024
INPROJ_SUB = 256


def _inproj_kernel(tiles_per_seq, x_ref, nw_ref, w_ref, qaw_ref, kvaw_ref, cw_ref, hist_in_ref,
                   lat_ref, dn_ref, ab_ref, tail_ref, hist_s, p0_s, p1_s):
    n = x_ref.shape[0]
    sub = min(INPROJ_SUB, n)
    p_bufs = (p0_s, p1_s)

    @pl.when(pl.program_id(0) % tiles_per_seq == 0)
    def _():
        hist_s[...] = hist_in_ref[...]

    def project(r):
        u = _rms(x_ref[r * sub:(r + 1) * sub, :], nw_ref[...]).astype(BF16)
        p_bufs[r % 2][0:sub, :] = _dot(u, w_ref[...])

    def post(r):
        p = p_bufs[r % 2]
        rows = slice(r * sub, (r + 1) * sub)
        lat_ref[rows, 0:Q_LORA] = _rms(p[0:sub, 0:Q_LORA], qaw_ref[...]).astype(BF16)
        lat_ref[rows, Q_LORA:Q_LORA + KV_LORA] = _rms(p[0:sub, Q_LORA:Q_LORA + KV_LORA],
                                                      kvaw_ref[...]).astype(BF16)
        lat_ref[rows, Q_LORA + KV_LORA:N_LAT] = p[0:sub, Q_LORA + KV_LORA:N_LAT].astype(BF16)
        ab_ref[rows, :] = p[0:sub, C_AB:C_AB + N_AB]
        pre = p[0:sub, C_DN:C_DN + 3 * DN_WIDTH]
        full = jnp.concatenate([hist_s[...], pre], axis=0)
        hist_s[...] = pre[sub - CONV_HIST:, :]
        full1 = pltpu.roll(full, 1, 0)
        near = cw_ref[3:4, :] * full + cw_ref[2:3, :] * full1
        far = cw_ref[1:2, :] * full + cw_ref[0:1, :] * full1
        act = _silu((near + pltpu.roll(far, 2, 0))[CONV_HIST:, :])
        for h in range(2 * DN_HEADS):
            a = act[:, h * DN_DIM:(h + 1) * DN_DIM]
            a = a * lax.rsqrt(jnp.sum(a * a, axis=-1, keepdims=True) + NORM_EPS)
            if h < DN_HEADS:
                a = a * (1.0 / math.sqrt(DN_DIM))
            dn_ref[rows, h * DN_DIM:(h + 1) * DN_DIM] = a.astype(BF16)
        dn_ref[rows, 2 * DN_WIDTH:3 * DN_WIDTH] = act[:, 2 * DN_WIDTH:].astype(BF16)
        dn_ref[rows, 3 * DN_WIDTH:] = _silu(p[0:sub, C_DN + 3 * DN_WIDTH:C_DN + N_DN]).astype(BF16)

    project(0)
    for r in range(n // sub):
        if r + 1 < n // sub:
            project(r + 1)
        post(r)
    tail_ref[...] = hist_s[...]


def _inproj(x2d, nw, w, qaw, kvaw, cw, hist_in, row_tile, tiles_per_seq):
    rows = x2d.shape[0]
    grid = (rows // row_tile,)
    const = lambda shape: pl.BlockSpec(shape, lambda i: (0,) * len(shape))
    return pl.pallas_call(
        functools.partial(_inproj_kernel, tiles_per_seq),
        grid=grid,
        in_specs=[
            pl.BlockSpec((row_tile, D_MODEL), lambda i: (i, 0)),
            const((1, D_MODEL)),
            const((D_MODEL, N_PROJ)),
            const((1, Q_LORA)),
            const((1, KV_LORA)),
            const((DN_CONV, 3 * DN_WIDTH)),
            const((CONV_HIST, 3 * DN_WIDTH)),
        ],
        out_specs=[
            pl.BlockSpec((row_tile, N_LAT), lambda i: (i, 0)),
            pl.BlockSpec((row_tile, N_DN), lambda i: (i, 0)),
            pl.BlockSpec((row_tile, N_AB), lambda i: (i, 0)),
            pl.BlockSpec((CONV_HIST, 3 * DN_WIDTH), lambda i: (i, 0)),
        ],
        out_shape=[
            jax.ShapeDtypeStruct((rows, N_LAT), BF16),
            jax.ShapeDtypeStruct((rows, N_DN), BF16),
            jax.ShapeDtypeStruct((rows, N_AB), F32),
            jax.ShapeDtypeStruct((grid[0] * CONV_HIST, 3 * DN_WIDTH), F32),
        ],
        scratch_shapes=[pltpu.VMEM((CONV_HIST, 3 * DN_WIDTH), F32),
                        pltpu.VMEM((min(INPROJ_SUB, row_tile), N_PROJ), F32),
                        pltpu.VMEM((min(INPROJ_SUB, row_tile), N_PROJ), F32)],
        compiler_params=_cparams(("arbitrary",)),
        name="inproj",
    )(x2d, nw, w, qaw, kvaw, cw, hist_in)


ATT_TQ = 256
ATT_PROJ_ROWS = 512
ATT_META = 2 * META_ROWS
ATT_META_VALID = ATT_META - N_META
ATT_ROWS = ATT_META + SEQ
NEG_INF = float("-inf")


def _mla_kernel(latx_ref, latm_ref, cs_ref, cst_ref, wqt_ref, wk_ref, wvt_ref,
                qnn_ref, qrw_ref, knn_ref, krw_ref, onw_ref,
                yx_ref, ym_ref, km_s, qm_s, vm_s, kx_s, qx_s, vx_s, acc_s, stm_s, st0_s, st1_s, p_s):
    low = lax.broadcasted_iota(jnp.int32, (1, 128), 1) < QK_ROPE
    scale = 1.0 / math.sqrt(QK_HEAD)

    def project(lat, cs, cst):
        nrows = lat.shape[0]
        qn = lat[:, 0:Q_LORA]
        kvn = lat[:, Q_LORA:Q_LORA + KV_LORA]
        pe = lat[:, Q_LORA + KV_LORA:N_LAT]
        qt = _dot(wqt_ref[...], qn.T.astype(BF16))
        vt = _dot(wvt_ref[...], kvn.T.astype(BF16)).astype(BF16)
        kn = _dot(kvn.astype(BF16), wk_ref[...])
        a = pe * (cs * krw_ref[...])
        k_rope = jnp.where(low, a + pltpu.roll(a, QK_ROPE, 1), 0.0)
        pe_ss = jnp.sum(jnp.where(low, pe * pe, 0.0), axis=-1, keepdims=True)
        cos_t = cst[0:QK_ROPE, :]
        sin_t = cst[QK_ROPE:2 * QK_ROPE, :]
        k_parts = []
        q_parts = []
        for h in range(MLA_HEADS):
            nope = kn[:, h * QK_NOPE:(h + 1) * QK_NOPE]
            rs = lax.rsqrt((jnp.sum(nope * nope, axis=-1, keepdims=True) + pe_ss) * (1.0 / QK_HEAD) + NORM_EPS)
            k_parts += [(nope * rs * knn_ref[...]).astype(BF16), (k_rope * rs).astype(BF16)]
            r0 = h * HEAD_SLOT
            qnope = qt[r0:r0 + QK_NOPE, :]
            qrope = qt[r0 + QK_NOPE:r0 + QK_HEAD, :]
            qrot = qt[r0 + QK_HEAD:r0 + HEAD_SLOT, :]
            ssq = (jnp.sum(qnope * qnope, axis=0, keepdims=True)
                   + jnp.sum(qrope * qrope, axis=0, keepdims=True))
            rsq = lax.rsqrt(ssq * (1.0 / QK_HEAD) + NORM_EPS) * scale
            roped = (qrope * (qrw_ref[0:QK_ROPE, :] * cos_t)
                     + qrot * (qrw_ref[QK_ROPE:2 * QK_ROPE, :] * sin_t))
            q_parts += [(qnope * qnn_ref[...] * rsq).astype(BF16), (roped * rsq).astype(BF16),
                        jnp.zeros((HEAD_SLOT - QK_HEAD, nrows), BF16)]
        return jnp.concatenate(k_parts, axis=1), jnp.concatenate(q_parts, axis=0), vt

    latm = jnp.concatenate([jnp.zeros((META_ROWS, N_LAT), F32), latm_ref[...].astype(F32)], axis=0)
    km_s[...], qm_s[...], vm_s[...] = project(latm, cs_ref[0:ATT_META, :], cst_ref[:, 0:ATT_META])
    for c in range(SEQ // ATT_PROJ_ROWS):
        r0 = c * ATT_PROJ_ROWS
        k, qt, vt = project(latx_ref[0, r0:r0 + ATT_PROJ_ROWS, :].astype(F32),
                            cs_ref[ATT_META + r0:ATT_META + r0 + ATT_PROJ_ROWS, :],
                            cst_ref[:, ATT_META + r0:ATT_META + r0 + ATT_PROJ_ROWS])
        kx_s[r0:r0 + ATT_PROJ_ROWS, :] = k
        for t in range(ATT_PROJ_ROWS // ATT_TQ):
            qx_s[c * (ATT_PROJ_ROWS // ATT_TQ) + t] = qt[:, t * ATT_TQ:(t + 1) * ATT_TQ]
            vx_s[c * (ATT_PROJ_ROWS // ATT_TQ) + t] = vt[:, t * ATT_TQ:(t + 1) * ATT_TQ]

    def first_block(st):
        m = jnp.max(st, axis=0, keepdims=True)
        p = jnp.exp(st - m)
        return p.astype(BF16), m, jnp.sum(p, axis=0, keepdims=True)

    def next_block(st, m, l):
        m_new = jnp.maximum(m, jnp.max(st, axis=0, keepdims=True))
        alpha = jnp.exp(m - m_new)
        p = jnp.exp(st - m_new)
        return p.astype(BF16), m_new, alpha * l + jnp.sum(p, axis=0, keepdims=True), alpha

    def finish(acc_t, l):
        o = acc_t * (1.0 / l)
        o = o * lax.rsqrt(jnp.mean(o * o, axis=0, keepdims=True) + NORM_EPS) * onw_ref[...]
        return o.T.astype(BF16)

    hs = lambda h: slice(h * HEAD_SLOT, (h + 1) * HEAD_SLOT)
    vs = lambda h: slice(h * V_HEAD, (h + 1) * V_HEAD)

    mkey = lax.broadcasted_iota(jnp.int32, (ATT_META, ATT_META), 0)
    mqry = lax.broadcasted_iota(jnp.int32, (ATT_META, ATT_META), 1)
    meta_mask = (mkey <= mqry) & ((mkey >= ATT_META_VALID) | (mkey == mqry))
    for h in range(MLA_HEADS):
        st = jnp.where(meta_mask, _dot(km_s[:, hs(h)], qm_s[hs(h), :]), NEG_INF)
        p, m, l = first_block(st)
        o = finish(_dot(vm_s[vs(h), :], p), l)
        ym_ref[0, :, vs(h)] = o[META_ROWS:, :]

    meta_key_ok = lax.broadcasted_iota(jnp.int32, (ATT_META, ATT_TQ), 0) >= ATT_META_VALID
    diag_mask = (lax.broadcasted_iota(jnp.int32, (ATT_TQ, ATT_TQ), 0)
                 <= lax.broadcasted_iota(jnp.int32, (ATT_TQ, ATT_TQ), 1))

    def k_blk(j):
        k0 = j * ATT_TQ
        if not isinstance(j, int):
            k0 = pl.multiple_of(k0, ATT_TQ)
        return lambda h: kx_s[pl.ds(k0, ATT_TQ), hs(h)]

    v_blk = lambda j: (lambda h: vx_s[j, vs(h), :])

    acc_t, p_t, stm_t, st_t = acc_s, p_s, stm_s, (st0_s, st1_s)

    for i in range(SEQ // ATT_TQ):
        def scores(buf, k_of, nk):
            for h in range(MLA_HEADS):
                buf[h, 0:nk, :] = _dot(k_of(h), qx_s[i, hs(h), :])

        def absorb(buf, v_of, nk, mask, ms, ls):
            ms2, ls2, alphas = [], [], []
            for h in range(MLA_HEADS):
                st = buf[h, 0:nk, :]
                if mask is not None:
                    st = jnp.where(mask, st, NEG_INF)
                p, m, l, alpha = next_block(st, ms[h], ls[h])
                p_t[h, 0:nk, :] = p
                ms2.append(m)
                ls2.append(l)
                alphas.append(alpha)
            for h in range(MLA_HEADS):
                acc_t[h] = acc_t[h] * alphas[h] + _dot(v_of(h), p_t[h, 0:nk, :])
            return tuple(ms2), tuple(ls2)

        acc_t[...] = jnp.zeros_like(acc_t)
        scores(stm_t, lambda h: km_s[:, hs(h)], ATT_META)
        scores(st_t[0], k_blk(0), ATT_TQ)
        ms = (jnp.full((1, ATT_TQ), NEG_INF, F32),) * MLA_HEADS
        ls = (jnp.zeros((1, ATT_TQ), F32),) * MLA_HEADS
        ms, ls = absorb(stm_t, lambda h: vm_s[vs(h), :], ATT_META, meta_key_ok, ms, ls)

        def pair(t, st_):
            ms, ls = st_
            scores(st_t[1], k_blk(2 * t + 1), ATT_TQ)
            ms, ls = absorb(st_t[0], v_blk(2 * t), ATT_TQ, None, ms, ls)
            scores(st_t[0], k_blk(2 * t + 2), ATT_TQ)
            return absorb(st_t[1], v_blk(2 * t + 1), ATT_TQ, None, ms, ls)

        if i // 2 > 0:
            ms, ls = lax.fori_loop(0, i // 2, pair, (ms, ls))
        if i % 2 == 1:
            scores(st_t[1], k_blk(i), ATT_TQ)
            ms, ls = absorb(st_t[0], v_blk(i - 1), ATT_TQ, None, ms, ls)
        ms, ls = absorb(st_t[i % 2], v_blk(i), ATT_TQ, diag_mask, ms, ls)
        for h in range(MLA_HEADS):
            yx_ref[0, i * ATT_TQ:(i + 1) * ATT_TQ, vs(h)] = finish(acc_t[h], ls[h])


def _mla(latx, latm, cs, cst, wqt, wk, wvt, qnn, qrw, knn, krw, onw):
    nb = latx.shape[0]
    const = lambda shape: pl.BlockSpec(shape, lambda b: (0,) * len(shape))
    nq = SEQ // ATT_TQ
    return pl.pallas_call(
        _mla_kernel,
        grid=(nb,),
        in_specs=[
            pl.BlockSpec((1, SEQ, N_LAT), lambda b: (b, 0, 0)),
            const((META_ROWS, N_LAT)),
            const((ATT_ROWS, 128)),
            const((128, ATT_ROWS)),
            const((MLA_HEADS * HEAD_SLOT, Q_LORA)),
            const((KV_LORA, MLA_HEADS * QK_NOPE)),
            const((MLA_HEADS * V_HEAD, KV_LORA)),
            const((QK_NOPE, 1)), const((2 * QK_ROPE, 1)), const((1, 128)), const((1, 128)),
            const((V_HEAD, 1)),
        ],
        out_specs=[
            pl.BlockSpec((1, SEQ, MLA_HEADS * V_HEAD), lambda b: (b, 0, 0)),
            pl.BlockSpec((1, META_ROWS, MLA_HEADS * V_HEAD), lambda b: (b, 0, 0)),
        ],
        out_shape=[
            jax.ShapeDtypeStruct((nb, SEQ, MLA_HEADS * V_HEAD), BF16),
            jax.ShapeDtypeStruct((nb, META_ROWS, MLA_HEADS * V_HEAD), BF16),
        ],
        scratch_shapes=[
            pltpu.VMEM((ATT_META, MLA_HEADS * HEAD_SLOT), BF16),
            pltpu.VMEM((MLA_HEADS * HEAD_SLOT, ATT_META), BF16),
            pltpu.VMEM((MLA_HEADS * V_HEAD, ATT_META), BF16),
            pltpu.VMEM((SEQ, MLA_HEADS * HEAD_SLOT), BF16),
            pltpu.VMEM((nq, MLA_HEADS * HEAD_SLOT, ATT_TQ), BF16),
            pltpu.VMEM((nq, MLA_HEADS * V_HEAD, ATT_TQ), BF16),
        ] + [
            pltpu.VMEM((MLA_HEADS, V_HEAD, ATT_TQ), F32),
            pltpu.VMEM((MLA_HEADS, ATT_META, ATT_TQ), F32),
            pltpu.VMEM((MLA_HEADS, ATT_TQ, ATT_TQ), F32),
            pltpu.VMEM((MLA_HEADS, ATT_TQ, ATT_TQ), F32),
            pltpu.VMEM((MLA_HEADS, ATT_TQ, ATT_TQ), BF16),
        ],
        compiler_params=_cparams(("arbitrary",)),
        name="mla",
    )(latx, latm, cs, cst, wqt, wk, wvt, qnn, qrw, knn, krw, onw)


DN_GROUP = 4
DN_GROUP_ROWS = DN_GROUP * DN_CHUNK


def _deltanet_kernel(dnx_ref, dnm_ref, abx_ref, abm_ref, alog_ref, dtb_ref, onw_ref,
                     yx_ref, ym_ref, s_s, af_s, t_s, pa_s, pb_s, rhs_s,
                     uw0_s, qk0_s, qg0_s, kd0_s, el0_s, uw1_s, qk1_s, qg1_s, kd1_s, el1_s,
                     smeta_s, ymeta_s):
    C = DN_CHUNK
    R = DN_STACK
    row = lax.broadcasted_iota(jnp.int32, (R, R), 0)
    col = lax.broadcasted_iota(jnp.int32, (R, R), 1)
    same = lambda sh: jnp.right_shift(row, sh) == jnp.right_shift(col, sh)
    m_incl = same(6) & (col <= row)
    m_strict = same(6) & (col < row)
    m_d16 = m_strict & same(4)
    m_l32 = m_strict & same(5) & jnp.logical_not(same(4))
    m_l64 = m_strict & jnp.logical_not(same(5))
    eye = (row == col).astype(F32)
    neg_a = -jnp.exp(alog_ref[...])

    s_s[...] = jnp.zeros_like(s_s)

    def stack(x):
        return jnp.concatenate([x[:, h * DN_DIM:(h + 1) * DN_DIM] for h in range(DN_HEADS)], axis=0)

    def stack_col(x, c0):
        return jnp.concatenate(
            [jnp.broadcast_to(x[:, c0 + h:c0 + h + 1], (C, DN_DIM)) for h in range(DN_HEADS)], axis=0)

    def bdot(a, b):
        return _dot(a.astype(BF16), b.astype(BF16))

    def gates(ab, row_ok):
        xa = ab + dtb_ref[...]
        softplus = jnp.maximum(xa, 0.0) + jnp.log(1.0 + jnp.exp(-jnp.abs(xa)))
        g = neg_a * softplus
        beta = jax.nn.sigmoid(ab)
        if row_ok is not None:
            g = jnp.where(row_ok, g, 0.0)
            beta = jnp.where(row_ok, beta, 0.0)
        pos = lax.broadcasted_iota(jnp.int32, g.shape, 0) & (C - 1)
        gc = g
        for s in (1, 2, 4, 8, 16, 32):
            gc = gc + jnp.where(pos >= s, pltpu.roll(gc, s, 0), 0.0)
        return gc, beta

    def phase_a(acts, gcl, betal, buf):
        uw_b, qk_b, qg_b, kd_b, el_b = buf
        n = len(acts)
        for c in range(n):
            act, gc, beta = acts[c], gcl[c], betal[c]
            q = stack(act[:, 0:DN_WIDTH])
            k = stack(act[:, DN_WIDTH:2 * DN_WIDTH])
            v = stack(act[:, 2 * DN_WIDTH:3 * DN_WIDTH])
            gcs = stack_col(gc, 0)
            gls = stack_col(jnp.broadcast_to(gc[C - 1:C, :], (C, 128)), 0)
            bs = stack_col(beta, DN_HEADS)
            grow = gcs.T[0:1, :]
            dec = jnp.exp(jnp.where(m_incl, gcs[:, 0:1] - grow, NEG_INF))
            kb = k * bs
            kbf = k.astype(BF16)
            eg = jnp.exp(gcs)
            af_s[c] = _dot_nt(kb.astype(BF16), kbf) * dec
            qk_b[c] = jnp.where(m_incl, _dot_nt(q.astype(BF16), kbf) * dec, 0.0).astype(BF16)
            rhs_s[c] = jnp.concatenate([v * bs, kb * eg], axis=1).astype(BF16)
            qg_b[c] = (q * eg).astype(BF16)
            kd_b[c] = (k * jnp.exp(gls - gcs)).astype(BF16)
            el_b[c] = jnp.exp(gls)
            yield

        for c in range(n):
            b1 = jnp.where(m_d16, af_s[c], 0.0)
            pa_s[c] = b1.astype(BF16)
            t_s[c] = eye - b1
        for c in range(n):
            pb_s[c] = _dot(pa_s[c], pa_s[c]).astype(BF16)
        yield
        for c in range(n):
            t_s[c] = t_s[c] + _dot(t_s[c].astype(BF16), pb_s[c])
            pa_s[c] = _dot(pb_s[c], pb_s[c]).astype(BF16)
        yield
        for c in range(n):
            t_s[c] = t_s[c] + _dot(t_s[c].astype(BF16), pa_s[c])
            pb_s[c] = _dot(pa_s[c], pa_s[c]).astype(BF16)
        yield
        for c in range(n):
            t_s[c] = t_s[c] + _dot(t_s[c].astype(BF16), pb_s[c])
        yield
        for m_low in (m_l32, m_l64):
            for c in range(n):
                pa_s[c] = _dot(jnp.where(m_low, af_s[c], 0.0).astype(BF16), t_s[c].astype(BF16)).astype(BF16)
            yield
            for c in range(n):
                t_s[c] = t_s[c] - _dot(t_s[c].astype(BF16), pa_s[c])
            yield
        for c in range(n):
            uw_b[c] = _dot(t_s[c].astype(BF16), rhs_s[c])

    def phase_b(zs, buf, outs):
        uw_b, qk_b, qg_b, kd_b, el_b = buf
        for c in range(len(zs)):
            vnew = []
            o_inter = []
            for h in range(DN_HEADS):
                r0 = h * C
                s_h = s_s[h]
                sb = s_h.astype(BF16)
                vn = uw_b[c, r0:r0 + C, 0:DN_DIM] - _dot(uw_b[c, r0:r0 + C, DN_DIM:].astype(BF16), sb)
                o_inter.append(_dot(qg_b[c, r0:r0 + C, :], sb))
                s_s[h] = s_h * el_b[c, r0:r0 + 1, :] + _dot_tn(kd_b[c, r0:r0 + C, :], vn.astype(BF16))
                vnew.append(vn)
            o = jnp.concatenate(o_inter, axis=0) + _dot(qk_b[c], jnp.concatenate(vnew, axis=0).astype(BF16))
            o = _rms(o, onw_ref[...])
            out = jnp.concatenate([o[h * C:(h + 1) * C, :] for h in range(DN_HEADS)], axis=1) * zs[c]
            outs.append(out.astype(BF16))
            yield

    def run(*gens_and_steps):
        gens = [g for g, _ in gens_and_steps]
        lens = [s for _, s in gens_and_steps]
        done = [0] * len(gens)
        alive = [True] * len(gens)
        while any(alive):
            i = min((j for j in range(len(gens)) if alive[j]), key=lambda j: (done[j] + 0.5) / lens[j])
            try:
                next(gens[i])
                done[i] += 1
            except StopIteration:
                alive[i] = False

    bufs = ((uw0_s, qk0_s, qg0_s, kd0_s, el0_s), (uw1_s, qk1_s, qg1_s, kd1_s, el1_s))

    @pl.when(pl.program_id(0) == 0)
    def _():
        meta_ok = lax.broadcasted_iota(jnp.int32, (C, 128), 0) >= META_PAD
        dn0 = dnm_ref[...]
        act0 = dn0[:, 0:3 * DN_WIDTH].astype(F32)
        gc0, beta0 = gates(abm_ref[...], meta_ok)
        run((phase_a([act0], [gc0], [beta0], bufs[1]), 1))
        out0 = []
        run((phase_b([dn0[:, 3 * DN_WIDTH:].astype(F32)], bufs[1], out0), 1))
        ymeta_s[...] = out0[0]
        smeta_s[...] = s_s[...]

    s_s[...] = smeta_s[...]
    ym_ref[0] = ymeta_s[...]

    rows = [slice(c * C, (c + 1) * C) for c in range(DN_GROUP)]
    a_steps = DN_GROUP + 10
    b_steps = DN_GROUP + 1

    def group_a(p, buf):
        r0 = p * DN_GROUP_ROWS
        if not isinstance(p, int):
            r0 = pl.multiple_of(r0, DN_GROUP_ROWS)
        act = dnx_ref[0, pl.ds(r0, DN_GROUP_ROWS), 0:3 * DN_WIDTH].astype(F32)
        gc, beta = gates(abx_ref[0, pl.ds(r0, DN_GROUP_ROWS), :], None)
        yield
        yield from phase_a([act[r, :] for r in rows], [gc[r, :] for r in rows], [beta[r, :] for r in rows], buf)

    def group_b(p, buf):
        r0 = p * DN_GROUP_ROWS
        if not isinstance(p, int):
            r0 = pl.multiple_of(r0, DN_GROUP_ROWS)
        z = dnx_ref[0, pl.ds(r0, DN_GROUP_ROWS), 3 * DN_WIDTH:].astype(F32)
        outs = []
        yield from phase_b([z[r, :] for r in rows], buf, outs)
        yx_ref[0, pl.ds(r0, DN_GROUP_ROWS), :] = jnp.concatenate(outs, axis=0)

    n_groups = SEQ // DN_GROUP_ROWS
    run((group_a(0, bufs[0]), a_steps))

    def body(kk, carry):
        p = 2 * kk
        run((group_a(p + 1, bufs[1]), a_steps), (group_b(p, bufs[0]), b_steps))
        run((group_a(p + 2, bufs[0]), a_steps), (group_b(p + 1, bufs[1]), b_steps))
        return carry

    lax.fori_loop(0, n_groups // 2 - 1, body, 0)
    run((group_a(n_groups - 1, bufs[1]), a_steps), (group_b(n_groups - 2, bufs[0]), b_steps))
    run((group_b(n_groups - 1, bufs[1]), b_steps))


def _deltanet(dnx, dnm, abx, abm, alog, dtb, onw):
    nb = dnx.shape[0]
    const = lambda shape: pl.BlockSpec(shape, lambda b: (0,) * len(shape))
    return pl.pallas_call(
        _deltanet_kernel,
        grid=(nb,),
        in_specs=[
            pl.BlockSpec((1, SEQ, N_DN), lambda b: (b, 0, 0)),
            const((META_ROWS, N_DN)),
            pl.BlockSpec((1, SEQ, N_AB), lambda b: (b, 0, 0)),
            const((META_ROWS, N_AB)),
            const((1, 128)), const((1, 128)), const((1, 128)),
        ],
        out_specs=[
            pl.BlockSpec((1, SEQ, DN_WIDTH), lambda b: (b, 0, 0)),
            pl.BlockSpec((1, META_ROWS, DN_WIDTH), lambda b: (b, 0, 0)),
        ],
        out_shape=[
            jax.ShapeDtypeStruct((nb, SEQ, DN_WIDTH), BF16),
            jax.ShapeDtypeStruct((nb, META_ROWS, DN_WIDTH), BF16),
        ],
        scratch_shapes=[
            pltpu.VMEM((DN_HEADS, DN_DIM, DN_DIM), F32),
            pltpu.VMEM((DN_GROUP, DN_STACK, DN_STACK), F32),
            pltpu.VMEM((DN_GROUP, DN_STACK, DN_STACK), F32),
            pltpu.VMEM((DN_GROUP, DN_STACK, DN_STACK), BF16),
            pltpu.VMEM((DN_GROUP, DN_STACK, DN_STACK), BF16),
            pltpu.VMEM((DN_GROUP, DN_STACK, 2 * DN_DIM), BF16),
        ] + 2 * [
            pltpu.VMEM((DN_GROUP, DN_STACK, 2 * DN_DIM), F32),
            pltpu.VMEM((DN_GROUP, DN_STACK, DN_STACK), BF16),
            pltpu.VMEM((DN_GROUP, DN_STACK, DN_DIM), BF16),
            pltpu.VMEM((DN_GROUP, DN_STACK, DN_DIM), BF16),
            pltpu.VMEM((DN_GROUP, DN_STACK, DN_DIM), F32),
        ] + [
            pltpu.VMEM((DN_HEADS, DN_DIM, DN_DIM), F32),
            pltpu.VMEM((META_ROWS, DN_WIDTH), BF16),
        ],
        compiler_params=_cparams(("arbitrary",)),
        name="deltanet",
    )(dnx, dnm, abx, abm, alog, dtb, onw)


FFN_ROWS = 1024
FFN_HALO = 16
N_FF_BLK = D_FF // FF_BLK


def _ffn_kernel(x_ref, xh_ref, mh_ref, ya_ref, yah_ref, yam_ref, yd_ref, ydh_ref, ydm_ref,
                wo_ref, nw_ref, wg_ref, wu_ref, cw_ref, cb_ref, wd_ref,
                o_ref, u_s, g0_s, g1_s, up0_s, up1_s, act_s):
    r = pl.program_id(1)
    mixed = jnp.concatenate([ya_ref[0], yd_ref[0]], axis=1)
    h_mid = x_ref[0] + _dot(mixed, wo_ref[...])
    o_ref[0] = h_mid
    u_s[FFN_HALO:, :] = _rms(h_mid, nw_ref[...]).astype(BF16)
    first = r == 0
    mixed_h = jnp.concatenate([jnp.where(first, yam_ref[0], yah_ref[0]),
                               jnp.where(first, ydm_ref[0], ydh_ref[0])], axis=1)
    h_halo = jnp.where(first, mh_ref[...], xh_ref[0]) + _dot(mixed_h, wo_ref[...])
    u_s[0:FFN_HALO, :] = _rms(h_halo, nw_ref[...]).astype(BF16)

    g_bufs = (g0_s, g1_s)
    up_bufs = (up0_s, up1_s)

    def project(f):
        cols = slice(f * FF_BLK, (f + 1) * FF_BLK)
        g_bufs[f % 2][...] = _dot(u_s[...], wg_ref[:, cols])
        up_bufs[f % 2][...] = _dot(u_s[FFN_HALO:, :], wu_ref[:, cols])

    project(0)
    for f in range(N_FF_BLK):
        if f + 1 < N_FF_BLK:
            project(f + 1)
        g_s = g_bufs[f % 2]
        cols = slice(f * FF_BLK, (f + 1) * FF_BLK)
        gate = (cw_ref[2:3, cols] * g_s[FFN_HALO:, :]
                + cw_ref[1:2, cols] * g_s[FFN_HALO - 1:FFN_HALO - 1 + FFN_ROWS, :]
                + cw_ref[0:1, cols] * g_s[FFN_HALO - 2:FFN_HALO - 2 + FFN_ROWS, :]
                + cb_ref[:, cols])
        act_s[:, cols] = (_silu(gate) * up_bufs[f % 2][...]).astype(BF16)
    o_ref[0] += _dot(act_s[...], wd_ref[...])


def _ffn(x, hp_meta, yax, yam, ydx, ydm, wo, nw, wg, wu, cw, cb, wd):
    nb = x.shape[0]
    nr = SEQ // FFN_ROWS
    hb = FFN_ROWS // FFN_HALO
    halo_idx = lambda b, r: (b, jnp.maximum(r * hb - 1, 0), 0)
    meta_idx = lambda b, r: (b, META_ROWS // FFN_HALO - 1, 0)
    main_idx = lambda b, r: (b, r, 0)
    resident = lambda shape: pl.BlockSpec(shape, lambda b, r: (0,) * len(shape),
                                          pipeline_mode=pl.Buffered(1))
    return pl.pallas_call(
        _ffn_kernel,
        grid=(nb, nr),
        in_specs=[
            pl.BlockSpec((1, FFN_ROWS, D_MODEL), main_idx),
            pl.BlockSpec((1, FFN_HALO, D_MODEL), halo_idx),
            pl.BlockSpec((FFN_HALO, D_MODEL), lambda b, r: (META_ROWS // FFN_HALO - 1, 0)),
            pl.BlockSpec((1, FFN_ROWS, MLA_HEADS * V_HEAD), main_idx),
            pl.BlockSpec((1, FFN_HALO, MLA_HEADS * V_HEAD), halo_idx),
            pl.BlockSpec((1, FFN_HALO, MLA_HEADS * V_HEAD), meta_idx),
            pl.BlockSpec((1, FFN_ROWS, DN_WIDTH), main_idx),
            pl.BlockSpec((1, FFN_HALO, DN_WIDTH), halo_idx),
            pl.BlockSpec((1, FFN_HALO, DN_WIDTH), meta_idx),
            resident((D_MODEL, D_MODEL)),
            resident((1, D_MODEL)),
            resident((D_MODEL, D_FF)),
            resident((D_MODEL, D_FF)),
            resident((3, D_FF)),
            resident((1, D_FF)),
            resident((D_FF, D_MODEL)),
        ],
        out_specs=pl.BlockSpec((1, FFN_ROWS, D_MODEL), main_idx),
        out_shape=jax.ShapeDtypeStruct((nb, SEQ, D_MODEL), F32),
        scratch_shapes=[
            pltpu.VMEM((FFN_HALO + FFN_ROWS, D_MODEL), BF16),
            pltpu.VMEM((FFN_HALO + FFN_ROWS, FF_BLK), F32),
            pltpu.VMEM((FFN_HALO + FFN_ROWS, FF_BLK), F32),
            pltpu.VMEM((FFN_ROWS, FF_BLK), F32),
            pltpu.VMEM((FFN_ROWS, FF_BLK), F32),
            pltpu.VMEM((FFN_ROWS, D_FF), BF16),
        ],
        compiler_params=_cparams(("arbitrary", "arbitrary")),
        name="outproj_ffn",
    )(x, x, hp_meta, yax, yax, yam, ydx, ydx, ydm, wo, nw, wg, wu, cw, cb, wd)


def _rot_cols(w):
    half = QK_ROPE // 2
    return jnp.concatenate([-w[..., half:], w[..., :half]], axis=-1)


def _swap_halves(w):
    half = QK_ROPE // 2
    return jnp.concatenate([w[..., half:], w[..., :half]], axis=-1)


def _pad_lanes(v, n=128):
    return jnp.pad(v.astype(F32), (0, n - v.shape[0])).reshape(1, n)


def _layer(x, hp_meta, l, attn_norm_w, w_in, q_a_norm_w, w_q_b, kv_a_norm_w, w_kv_b, q_norm_w,
           k_norm_w, mla_out_norm_w, dn_conv_w, dn_A_log, dn_dt_bias, dn_out_norm_w, w_out,
           ffn_norm_w, w_gate, w_up, ffn_conv_w, ffn_conv_b, w_down):
    nb = x.shape[0]
    c1 = Q_LORA
    c2 = c1 + KV_LORA
    c3 = c2 + QK_ROPE
    c4 = c3 + 3 * DN_WIDTH
    c5 = c4 + DN_WIDTH
    wint = w_in[l].T.astype(BF16)
    k_pe_w = wint[c2:c3, :]
    half = QK_ROPE // 2
    w1 = jnp.concatenate(
        [wint[:c2, :], k_pe_w, -k_pe_w[half:, :], k_pe_w[:half, :], wint[c3:c5, :], wint[c5:, :],
         jnp.zeros((N_AB - 2 * DN_HEADS, D_MODEL), BF16)], axis=0).T

    wqb = w_q_b[l].reshape(Q_LORA, MLA_HEADS, QK_HEAD)
    wqt = jnp.concatenate([wqb[..., :QK_NOPE], wqb[..., QK_NOPE:], _rot_cols(wqb[..., QK_NOPE:])],
                          axis=-1).reshape(Q_LORA, MLA_HEADS * HEAD_SLOT).T.astype(BF16)
    wkvb = w_kv_b[l].reshape(KV_LORA, MLA_HEADS, QK_NOPE + V_HEAD)
    wk = wkvb[..., :QK_NOPE].reshape(KV_LORA, MLA_HEADS * QK_NOPE).astype(BF16)
    wvt = wkvb[..., QK_NOPE:].reshape(KV_LORA, MLA_HEADS * V_HEAD).T.astype(BF16)
    qn = q_norm_w[l].astype(F32)
    kn = k_norm_w[l].astype(F32)
    qnn = qn[:QK_NOPE].reshape(QK_NOPE, 1)
    knn = kn[:QK_NOPE].reshape(1, 128)
    qrw = jnp.concatenate([qn[QK_NOPE:], _swap_halves(qn[QK_NOPE:])]).reshape(2 * QK_ROPE, 1)
    krw = jnp.concatenate([kn[QK_NOPE:], _swap_halves(kn[QK_NOPE:])]).reshape(1, 128)

    half = QK_ROPE // 2
    inv_freq = ROPE_THETA ** (-jnp.arange(half, dtype=F32) / half)
    pos = (jnp.arange(ATT_ROWS, dtype=jnp.int32) - ATT_META_VALID).astype(F32)
    ang = pos[:, None] * inv_freq[None, :]
    cs = jnp.concatenate([jnp.cos(ang), jnp.cos(ang), jnp.sin(ang), jnp.sin(ang)], axis=1)
    cst = cs.T

    nw1 = attn_norm_w[l].astype(F32).reshape(1, D_MODEL)
    qaw = q_a_norm_w[l].astype(F32).reshape(1, Q_LORA)
    kvaw = kv_a_norm_w[l].astype(F32).reshape(1, KV_LORA)
    cw = dn_conv_w[l].astype(F32)
    no_hist = jnp.zeros((CONV_HIST, 3 * DN_WIDTH), F32)
    latm, dnm, abm, meta_tail = _inproj(hp_meta, nw1, w1, qaw, kvaw, cw, no_hist, META_ROWS, 1)
    latx, dnx, abx, _ = _inproj(x.reshape(nb * SEQ, D_MODEL), nw1, w1, qaw, kvaw, cw, meta_tail,
                                INPROJ_ROWS, SEQ // INPROJ_ROWS)
    latx = latx.reshape(nb, SEQ, N_LAT)
    dnx = dnx.reshape(nb, SEQ, N_DN)
    abx = abx.reshape(nb, SEQ, N_AB)

    yax, yam = _mla(latx, latm, cs, cst, wqt, wk, wvt,
                    qnn, qrw, knn, krw, mla_out_norm_w[l].astype(F32).reshape(V_HEAD, 1))
    ydx, ydm = _deltanet(dnx, dnm, abx, abm,
                         _pad_lanes(dn_A_log[l]), _pad_lanes(dn_dt_bias[l]),
                         dn_out_norm_w[l].astype(F32).reshape(1, DN_DIM))
    return _ffn(x, hp_meta, yax, yam, ydx, ydm, w_out[l].astype(BF16),
                ffn_norm_w[l].astype(F32).reshape(1, D_MODEL), w_gate[l].astype(BF16),
                w_up[l].astype(BF16), ffn_conv_w[l].astype(F32),
                ffn_conv_b[l].astype(F32).reshape(1, D_FF), w_down[l].astype(BF16))


def kernel(x, meta_tokens, attn_norm_w, w_in, q_a_norm_w, w_q_b, kv_a_norm_w, w_kv_b, q_norm_w, k_norm_w, mla_out_norm_w, dn_conv_w, dn_A_log, dn_dt_bias, dn_out_norm_w, w_out, ffn_norm_w, w_gate, w_up, ffn_conv_w, ffn_conv_b, w_down):
    assert x.shape[1:] == (SEQ, D_MODEL) and w_in.shape[0] == 1
    hp_meta = jnp.concatenate([jnp.zeros((META_PAD, D_MODEL), x.dtype), meta_tokens.astype(x.dtype)], axis=0)
    return _layer(x, hp_meta, 0, attn_norm_w, w_in, q_a_norm_w, w_q_b, kv_a_norm_w, w_kv_b, q_norm_w,
                  k_norm_w, mla_out_norm_w, dn_conv_w, dn_A_log, dn_dt_bias, dn_out_norm_w, w_out,
                  ffn_norm_w, w_gate, w_up, ffn_conv_w, ffn_conv_b, w_down)
```

```python
import functools
import math

import jax
import jax.numpy as jnp
from jax import lax
from jax.experimental import pallas as pl
from jax.experimental.pallas import tpu as pltpu

F32 = jnp.float32
BF16 = jnp.bfloat16

D_MODEL = 1024
SEQ = 2048
N_META = 16
META_ROWS = 64
META_PAD = META_ROWS - N_META

MLA_HEADS = 4
QK_NOPE = 128
QK_ROPE = 64
QK_HEAD = QK_NOPE + QK_ROPE
V_HEAD = 128
Q_LORA = 256
KV_LORA = 256
ROPE_THETA = 10000.0
HEAD_SLOT = 256

DN_HEADS = 4
DN_DIM = 128
DN_WIDTH = DN_HEADS * DN_DIM
DN_CHUNK = 64
DN_STACK = DN_HEADS * DN_CHUNK
DN_CONV = 4

D_FF = 2816
FF_BLK = 256
NORM_EPS = 1e-6

C_LAT = 0
N_LAT = Q_LORA + KV_LORA + 2 * QK_ROPE
C_DN = N_LAT
N_DN = 4 * DN_WIDTH
C_AB = C_DN + N_DN
N_AB = 128
N_PROJ = C_AB + N_AB

VMEM_LIMIT = 56 * 1024 * 1024


def _cparams(sem):
    return pltpu.CompilerParams(dimension_semantics=sem, vmem_limit_bytes=VMEM_LIMIT)


def _rms(x, w):
    return x * lax.rsqrt(jnp.mean(x * x, axis=-1, keepdims=True) + NORM_EPS) * w


def _dot(a, b):
    return jnp.dot(a, b, preferred_element_type=F32)


def _dot_nt(a, b):
    return lax.dot_general(a, b, (((1,), (1,)), ((), ())), preferred_element_type=F32)


def _dot_tn(a, b):
    return lax.dot_general(a, b, (((0,), (0,)), ((), ())), preferred_element_type=F32)


def _silu(x):
    return x * jax.nn.sigmoid(x)


CONV_HIST = 16
INPROJ_ROWS = 1024
INPROJ_SUB = 256


def _inproj_kernel(tiles_per_seq, x_ref, nw_ref, w_ref, qaw_ref, kvaw_ref, cw_ref, hist_in_ref,
                   lat_ref, dn_ref, ab_ref, tail_ref, hist_s, p0_s, p1_s):
    n = x_ref.shape[0]
    sub = min(INPROJ_SUB, n)
    p_bufs = (p0_s, p1_s)

    @pl.when(pl.program_id(0) % tiles_per_seq == 0)
    def _():
        hist_s[...] = hist_in_ref[...]

    def project(r):
        u = _rms(x_ref[r * sub:(r + 1) * sub, :], nw_ref[...]).astype(BF16)
        p_bufs[r % 2][0:sub, :] = _dot(u, w_ref[...])

    def post(r):
        p = p_bufs[r % 2]
        rows = slice(r * sub, (r + 1) * sub)
        lat_ref[rows, 0:Q_LORA] = _rms(p[0:sub, 0:Q_LORA], qaw_ref[...]).astype(BF16)
        lat_ref[rows, Q_LORA:Q_LORA + KV_LORA] = _rms(p[0:sub, Q_LORA:Q_LORA + KV_LORA],
                                                      kvaw_ref[...]).astype(BF16)
        lat_ref[rows, Q_LORA + KV_LORA:N_LAT] = p[0:sub, Q_LORA + KV_LORA:N_LAT].astype(BF16)
        ab_ref[rows, :] = p[0:sub, C_AB:C_AB + N_AB]
        pre = p[0:sub, C_DN:C_DN + 3 * DN_WIDTH]
        full = jnp.concatenate([hist_s[...], pre], axis=0)
        hist_s[...] = pre[sub - CONV_HIST:, :]
        full1 = pltpu.roll(full, 1, 0)
        near = cw_ref[3:4, :] * full + cw_ref[2:3, :] * full1
        far = cw_ref[1:2, :] * full + cw_ref[0:1, :] * full1
        act = _silu((near + pltpu.roll(far, 2, 0))[CONV_HIST:, :])
        for h in range(2 * DN_HEADS):
            a = act[:, h * DN_DIM:(h + 1) * DN_DIM]
            a = a * lax.rsqrt(jnp.sum(a * a, axis=-1, keepdims=True) + NORM_EPS)
            if h < DN_HEADS:
                a = a * (1.0 / math.sqrt(DN_DIM))
            dn_ref[rows, h * DN_DIM:(h + 1) * DN_DIM] = a.astype(BF16)
        dn_ref[rows, 2 * DN_WIDTH:3 * DN_WIDTH] = act[:, 2 * DN_WIDTH:].astype(BF16)
        dn_ref[rows, 3 * DN_WIDTH:] = _silu(p[0:sub, C_DN + 3 * DN_WIDTH:C_DN + N_DN]).astype(BF16)

    project(0)
    for r in range(n // sub):
        if r + 1 < n // sub:
            project(r + 1)
        post(r)
    tail_ref[...] = hist_s[...]


def _inproj(x2d, nw, w, qaw, kvaw, cw, hist_in, row_tile, tiles_per_seq):
    rows = x2d.shape[0]
    grid = (rows // row_tile,)
    const = lambda shape: pl.BlockSpec(shape, lambda i: (0,) * len(shape))
    return pl.pallas_call(
        functools.partial(_inproj_kernel, tiles_per_seq),
        grid=grid,
        in_specs=[
            pl.BlockSpec((row_tile, D_MODEL), lambda i: (i, 0)),
            const((1, D_MODEL)),
            const((D_MODEL, N_PROJ)),
            const((1, Q_LORA)),
            const((1, KV_LORA)),
            const((DN_CONV, 3 * DN_WIDTH)),
            const((CONV_HIST, 3 * DN_WIDTH)),
        ],
        out_specs=[
            pl.BlockSpec((row_tile, N_LAT), lambda i: (i, 0)),
            pl.BlockSpec((row_tile, N_DN), lambda i: (i, 0)),
            pl.BlockSpec((row_tile, N_AB), lambda i: (i, 0)),
            pl.BlockSpec((CONV_HIST, 3 * DN_WIDTH), lambda i: (i, 0)),
        ],
        out_shape=[
            jax.ShapeDtypeStruct((rows, N_LAT), BF16),
            jax.ShapeDtypeStruct((rows, N_DN), BF16),
            jax.ShapeDtypeStruct((rows, N_AB), F32),
            jax.ShapeDtypeStruct((grid[0] * CONV_HIST, 3 * DN_WIDTH), F32),
        ],
        scratch_shapes=[pltpu.VMEM((CONV_HIST, 3 * DN_WIDTH), F32),
                        pltpu.VMEM((min(INPROJ_SUB, row_tile), N_PROJ), F32),
                        pltpu.VMEM((min(INPROJ_SUB, row_tile), N_PROJ), F32)],
        compiler_params=_cparams(("arbitrary",)),
        name="inproj",
    )(x2d, nw, w, qaw, kvaw, cw, hist_in)


ATT_TQ = 256
V_SLOT = V_HEAD + 16
ATT_PROJ_ROWS = 512
ATT_META = 2 * META_ROWS
ATT_META_VALID = ATT_META - N_META
ATT_ROWS = ATT_META + SEQ
NEG_INF = float("-inf")


def _mla_kernel(latx_ref, latm_ref, cs_ref, cst_ref, wqt_ref, wk_ref, wvt_ref,
                qnn_ref, qrw_ref, knn_ref, krw_ref, onw_ref,
                yx_ref, ym_ref, km_s, qm_s, vm_s, kx_s, qx_s, vx_s, acc_s, stm_s, st0_s, st1_s, p_s):
    low = lax.broadcasted_iota(jnp.int32, (1, 128), 1) < QK_ROPE
    scale = 1.0 / math.sqrt(QK_HEAD)

    def project(lat, cs, cst):
        nrows = lat.shape[0]
        qn = lat[:, 0:Q_LORA]
        kvn = lat[:, Q_LORA:Q_LORA + KV_LORA]
        pe = lat[:, Q_LORA + KV_LORA:N_LAT]
        qt = _dot(wqt_ref[...], qn.T.astype(BF16))
        vt = _dot(wvt_ref[...], kvn.T.astype(BF16)).astype(BF16)
        kn = _dot(kvn.astype(BF16), wk_ref[...])
        a = pe * (cs * krw_ref[...])
        k_rope = jnp.where(low, a + pltpu.roll(a, QK_ROPE, 1), 0.0)
        pe_ss = jnp.sum(jnp.where(low, pe * pe, 0.0), axis=-1, keepdims=True)
        cos_t = cst[0:QK_ROPE, :]
        sin_t = cst[QK_ROPE:2 * QK_ROPE, :]
        k_parts = []
        q_parts = []
        for h in range(MLA_HEADS):
            nope = kn[:, h * QK_NOPE:(h + 1) * QK_NOPE]
            rs = lax.rsqrt((jnp.sum(nope * nope, axis=-1, keepdims=True) + pe_ss) * (1.0 / QK_HEAD) + NORM_EPS)
            k_parts += [(nope * rs * knn_ref[...]).astype(BF16), (k_rope * rs).astype(BF16)]
            r0 = h * HEAD_SLOT
            qnope = qt[r0:r0 + QK_NOPE, :]
            qrope = qt[r0 + QK_NOPE:r0 + QK_HEAD, :]
            qrot = qt[r0 + QK_HEAD:r0 + HEAD_SLOT, :]
            ssq = (jnp.sum(qnope * qnope, axis=0, keepdims=True)
                   + jnp.sum(qrope * qrope, axis=0, keepdims=True))
            rsq = lax.rsqrt(ssq * (1.0 / QK_HEAD) + NORM_EPS) * scale
            roped = (qrope * (qrw_ref[0:QK_ROPE, :] * cos_t)
                     + qrot * (qrw_ref[QK_ROPE:2 * QK_ROPE, :] * sin_t))
            q_parts += [(qnope * qnn_ref[...] * rsq).astype(BF16), (roped * rsq).astype(BF16),
                        jnp.zeros((HEAD_SLOT - QK_HEAD, nrows), BF16)]
        ones_row = (lax.broadcasted_iota(jnp.int32, (V_SLOT - V_HEAD, nrows), 0) == 0).astype(BF16)
        v_parts = []
        for h in range(MLA_HEADS):
            v_parts += [vt[h * V_HEAD:(h + 1) * V_HEAD, :], ones_row]
        return jnp.concatenate(k_parts, axis=1), jnp.concatenate(q_parts, axis=0), jnp.concatenate(v_parts, axis=0)

    latm = jnp.concatenate([jnp.zeros((META_ROWS, N_LAT), F32), latm_ref[...].astype(F32)], axis=0)
    km_s[...], qm_s[...], vm_s[...] = project(latm, cs_ref[0:ATT_META, :], cst_ref[:, 0:ATT_META])
    for c in range(SEQ // ATT_PROJ_ROWS):
        r0 = c * ATT_PROJ_ROWS
        k, qt, vt = project(latx_ref[0, r0:r0 + ATT_PROJ_ROWS, :].astype(F32),
                            cs_ref[ATT_META + r0:ATT_META + r0 + ATT_PROJ_ROWS, :],
                            cst_ref[:, ATT_META + r0:ATT_META + r0 + ATT_PROJ_ROWS])
        kx_s[r0:r0 + ATT_PROJ_ROWS, :] = k
        for t in range(ATT_PROJ_ROWS // ATT_TQ):
            qx_s[c * (ATT_PROJ_ROWS // ATT_TQ) + t] = qt[:, t * ATT_TQ:(t + 1) * ATT_TQ]
            vx_s[c * (ATT_PROJ_ROWS // ATT_TQ) + t] = vt[:, t * ATT_TQ:(t + 1) * ATT_TQ]

    def next_block(st, m):
        m_new = jnp.maximum(m, jnp.max(st, axis=0, keepdims=True))
        alpha = jnp.exp(m - m_new)
        p = jnp.exp((st - m_new).astype(BF16))
        return p, m_new, alpha

    def finish(acc_ext):
        o = acc_ext[0:V_HEAD, :] * (1.0 / acc_ext[V_HEAD:V_HEAD + 1, :])
        o = o * lax.rsqrt(jnp.mean(o * o, axis=0, keepdims=True) + NORM_EPS) * onw_ref[...]
        return o.T.astype(BF16)

    hs = lambda h: slice(h * HEAD_SLOT, (h + 1) * HEAD_SLOT)
    vs = lambda h: slice(h * V_SLOT, (h + 1) * V_SLOT)
    outs = lambda h: slice(h * V_HEAD, (h + 1) * V_HEAD)

    mkey = lax.broadcasted_iota(jnp.int32, (ATT_META, ATT_META), 0)
    mqry = lax.broadcasted_iota(jnp.int32, (ATT_META, ATT_META), 1)
    meta_mask = (mkey <= mqry) & ((mkey >= ATT_META_VALID) | (mkey == mqry))
    for h in range(MLA_HEADS):
        st = jnp.where(meta_mask, _dot(km_s[:, hs(h)], qm_s[hs(h), :]), NEG_INF)
        p, _, _ = next_block(st, jnp.full((1, ATT_META), NEG_INF, F32))
        o = finish(_dot(vm_s[vs(h), :], p))
        ym_ref[0, :, outs(h)] = o[META_ROWS:, :]

    meta_key_ok = lax.broadcasted_iota(jnp.int32, (ATT_META, ATT_TQ), 0) >= ATT_META_VALID
    diag_mask = (lax.broadcasted_iota(jnp.int32, (ATT_TQ, ATT_TQ), 0)
                 <= lax.broadcasted_iota(jnp.int32, (ATT_TQ, ATT_TQ), 1))

    def k_blk(j):
        k0 = j * ATT_TQ
        if not isinstance(j, int):
            k0 = pl.multiple_of(k0, ATT_TQ)
        return lambda h: kx_s[pl.ds(k0, ATT_TQ), hs(h)]

    v_blk = lambda j: (lambda h: vx_s[j, vs(h), :])

    acc_t, p_t, stm_t, st_t = acc_s, p_s, stm_s, (st0_s, st1_s)

    for i in range(SEQ // ATT_TQ):
        def scores(buf, k_of, nk):
            for h in range(MLA_HEADS):
                buf[h, 0:nk, :] = _dot(k_of(h), qx_s[i, hs(h), :])

        def absorb(buf, v_of, nk, mask, ms):
            ms2, alphas = [], []
            for h in range(MLA_HEADS):
                st = buf[h, 0:nk, :]
                if mask is not None:
                    st = jnp.where(mask, st, NEG_INF)
                p, m, alpha = next_block(st, ms[h])
                p_t[h, 0:nk, :] = p
                ms2.append(m)
                alphas.append(alpha)
            for h in range(MLA_HEADS):
                acc_t[h] = acc_t[h] * alphas[h] + _dot(v_of(h), p_t[h, 0:nk, :])
            return tuple(ms2)

        acc_t[...] = jnp.zeros_like(acc_t)
        scores(stm_t, lambda h: km_s[:, hs(h)], ATT_META)
        scores(st_t[0], k_blk(0), ATT_TQ)
        ms = (jnp.full((1, ATT_TQ), NEG_INF, F32),) * MLA_HEADS
        ms = absorb(stm_t, lambda h: vm_s[vs(h), :], ATT_META, meta_key_ok, ms)

        def pair(t, ms):
            scores(st_t[1], k_blk(2 * t + 1), ATT_TQ)
            ms = absorb(st_t[0], v_blk(2 * t), ATT_TQ, None, ms)
            scores(st_t[0], k_blk(2 * t + 2), ATT_TQ)
            return absorb(st_t[1], v_blk(2 * t + 1), ATT_TQ, None, ms)

        if i // 2 > 0:
            ms = lax.fori_loop(0, i // 2, pair, ms)
        if i % 2 == 1:
            scores(st_t[1], k_blk(i), ATT_TQ)
            ms = absorb(st_t[0], v_blk(i - 1), ATT_TQ, None, ms)
        absorb(st_t[i % 2], v_blk(i), ATT_TQ, diag_mask, ms)
        for h in range(MLA_HEADS):
            yx_ref[0, i * ATT_TQ:(i + 1) * ATT_TQ, outs(h)] = finish(acc_t[h])


def _mla(latx, latm, cs, cst, wqt, wk, wvt, qnn, qrw, knn, krw, onw):
    nb = latx.shape[0]
    const = lambda shape: pl.BlockSpec(shape, lambda b: (0,) * len(shape))
    nq = SEQ // ATT_TQ
    return pl.pallas_call(
        _mla_kernel,
        grid=(nb,),
        in_specs=[
            pl.BlockSpec((1, SEQ, N_LAT), lambda b: (b, 0, 0)),
            const((META_ROWS, N_LAT)),
            const((ATT_ROWS, 128)),
            const((128, ATT_ROWS)),
            const((MLA_HEADS * HEAD_SLOT, Q_LORA)),
            const((KV_LORA, MLA_HEADS * QK_NOPE)),
            const((MLA_HEADS * V_HEAD, KV_LORA)),
            const((QK_NOPE, 1)), const((2 * QK_ROPE, 1)), const((1, 128)), const((1, 128)),
            const((V_HEAD, 1)),
        ],
        out_specs=[
            pl.BlockSpec((1, SEQ, MLA_HEADS * V_HEAD), lambda b: (b, 0, 0)),
            pl.BlockSpec((1, META_ROWS, MLA_HEADS * V_HEAD), lambda b: (b, 0, 0)),
        ],
        out_shape=[
            jax.ShapeDtypeStruct((nb, SEQ, MLA_HEADS * V_HEAD), BF16),
            jax.ShapeDtypeStruct((nb, META_ROWS, MLA_HEADS * V_HEAD), BF16),
        ],
        scratch_shapes=[
            pltpu.VMEM((ATT_META, MLA_HEADS * HEAD_SLOT), BF16),
            pltpu.VMEM((MLA_HEADS * HEAD_SLOT, ATT_META), BF16),
            pltpu.VMEM((MLA_HEADS * V_SLOT, ATT_META), BF16),
            pltpu.VMEM((SEQ, MLA_HEADS * HEAD_SLOT), BF16),
            pltpu.VMEM((nq, MLA_HEADS * HEAD_SLOT, ATT_TQ), BF16),
            pltpu.VMEM((nq, MLA_HEADS * V_SLOT, ATT_TQ), BF16),
        ] + [
            pltpu.VMEM((MLA_HEADS, V_SLOT, ATT_TQ), F32),
            pltpu.VMEM((MLA_HEADS, ATT_META, ATT_TQ), F32),
            pltpu.VMEM((MLA_HEADS, ATT_TQ, ATT_TQ), F32),
            pltpu.VMEM((MLA_HEADS, ATT_TQ, ATT_TQ), F32),
            pltpu.VMEM((MLA_HEADS, ATT_TQ, ATT_TQ), BF16),
        ],
        compiler_params=_cparams(("arbitrary",)),
        name="mla",
    )(latx, latm, cs, cst, wqt, wk, wvt, qnn, qrw, knn, krw, onw)


DN_GROUP = 4
DN_GROUP_ROWS = DN_GROUP * DN_CHUNK


def _deltanet_kernel(dnx_ref, dnm_ref, abx_ref, abm_ref, alog_ref, dtb_ref, onw_ref,
                     yx_ref, ym_ref, s_s, af_s, t_s, pa_s, pb_s, rhs_s,
                     uw0_s, qk0_s, qg0_s, kd0_s, el0_s, uw1_s, qk1_s, qg1_s, kd1_s, el1_s,
                     smeta_s, ymeta_s):
    C = DN_CHUNK
    R = DN_STACK
    row = lax.broadcasted_iota(jnp.int32, (R, R), 0)
    col = lax.broadcasted_iota(jnp.int32, (R, R), 1)
    same = lambda sh: jnp.right_shift(row, sh) == jnp.right_shift(col, sh)
    m_incl = same(6) & (col <= row)
    m_strict = same(6) & (col < row)
    m_d16 = m_strict & same(4)
    m_l32 = m_strict & same(5) & jnp.logical_not(same(4))
    m_l64 = m_strict & jnp.logical_not(same(5))
    eye = (row == col).astype(F32)
    neg_a = -jnp.exp(alog_ref[...])

    s_s[...] = jnp.zeros_like(s_s)

    def stack(x):
        return jnp.concatenate([x[:, h * DN_DIM:(h + 1) * DN_DIM] for h in range(DN_HEADS)], axis=0)

    def stack_col(x, c0):
        return jnp.concatenate(
            [jnp.broadcast_to(x[:, c0 + h:c0 + h + 1], (C, DN_DIM)) for h in range(DN_HEADS)], axis=0)

    def bdot(a, b):
        return _dot(a.astype(BF16), b.astype(BF16))

    def gates(ab, row_ok):
        xa = ab + dtb_ref[...]
        softplus = jnp.maximum(xa, 0.0) + jnp.log(1.0 + jnp.exp(-jnp.abs(xa)))
        g = neg_a * softplus
        beta = jax.nn.sigmoid(ab)
        if row_ok is not None:
            g = jnp.where(row_ok, g, 0.0)
            beta = jnp.where(row_ok, beta, 0.0)
        pos = lax.broadcasted_iota(jnp.int32, g.shape, 0) & (C - 1)
        gc = g
        for s in (1, 2, 4, 8, 16, 32):
            gc = gc + jnp.where(pos >= s, pltpu.roll(gc, s, 0), 0.0)
        return gc, beta

    def phase_a(acts, gcl, betal, buf):
        uw_b, qk_b, qg_b, kd_b, el_b = buf
        n = len(acts)
        for c in range(n):
            act, gc, beta = acts[c], gcl[c], betal[c]
            q = stack(act[:, 0:DN_WIDTH])
            k = stack(act[:, DN_WIDTH:2 * DN_WIDTH])
            v = stack(act[:, 2 * DN_WIDTH:3 * DN_WIDTH])
            gcs = stack_col(gc, 0)
            gls = stack_col(jnp.broadcast_to(gc[C - 1:C, :], (C, 128)), 0)
            bs = stack_col(beta, DN_HEADS)
            grow = gcs.T[0:1, :]
            dec = jnp.exp(jnp.where(m_incl, gcs[:, 0:1] - grow, NEG_INF))
            kb = k * bs
            kbf = k.astype(BF16)
            eg = jnp.exp(gcs)
            af_s[c] = _dot_nt(kb.astype(BF16), kbf) * dec
            qk_b[c] = jnp.where(m_incl, _dot_nt(q.astype(BF16), kbf) * dec, 0.0).astype(BF16)
            rhs_s[c] = jnp.concatenate([v * bs, kb * eg], axis=1).astype(BF16)
            qg_b[c] = (q * eg).astype(BF16)
            kd_b[c] = (k * jnp.exp(gls - gcs)).astype(BF16)
            el_b[c] = jnp.exp(gls)
            yield

        for c in range(n):
            b1 = jnp.where(m_d16, af_s[c], 0.0)
            pa_s[c] = b1.astype(BF16)
            t_s[c] = eye - b1
        for c in range(n):
            pb_s[c] = _dot(pa_s[c], pa_s[c]).astype(BF16)
        yield
        for c in range(n):
            t_s[c] = t_s[c] + _dot(t_s[c].astype(BF16), pb_s[c])
            pa_s[c] = _dot(pb_s[c], pb_s[c]).astype(BF16)
        yield
        for c in range(n):
            t_s[c] = t_s[c] + _dot(t_s[c].astype(BF16), pa_s[c])
            pb_s[c] = _dot(pa_s[c], pa_s[c]).astype(BF16)
        yield
        for c in range(n):
            t_s[c] = t_s[c] + _dot(t_s[c].astype(BF16), pb_s[c])
        yield
        for m_low in (m_l32, m_l64):
            for c in range(n):
                pa_s[c] = _dot(jnp.where(m_low, af_s[c], 0.0).astype(BF16), t_s[c].astype(BF16)).astype(BF16)
            yield
            for c in range(n):
                t_s[c] = t_s[c] - _dot(t_s[c].astype(BF16), pa_s[c])
            yield
        for c in range(n):
            uw_b[c] = _dot(t_s[c].astype(BF16), rhs_s[c])

    def phase_b(zs, buf, outs):
        uw_b, qk_b, qg_b, kd_b, el_b = buf
        for c in range(len(zs)):
            vnew = []
            o_inter = []
            for h in range(DN_HEADS):
                r0 = h * C
                s_h = s_s[h]
                sb = s_h.astype(BF16)
                vn = uw_b[c, r0:r0 + C, 0:DN_DIM] - _dot(uw_b[c, r0:r0 + C, DN_DIM:].astype(BF16), sb)
                o_inter.append(_dot(qg_b[c, r0:r0 + C, :], sb))
                s_s[h] = s_h * el_b[c, r0:r0 + 1, :] + _dot_tn(kd_b[c, r0:r0 + C, :], vn.astype(BF16))
                vnew.append(vn)
            o = jnp.concatenate(o_inter, axis=0) + _dot(qk_b[c], jnp.concatenate(vnew, axis=0).astype(BF16))
            o = _rms(o, onw_ref[...])
            out = jnp.concatenate([o[h * C:(h + 1) * C, :] for h in range(DN_HEADS)], axis=1) * zs[c]
            outs.append(out.astype(BF16))
            yield

    def run(*gens_and_steps):
        gens = [g for g, _ in gens_and_steps]
        lens = [s for _, s in gens_and_steps]
        done = [0] * len(gens)
        alive = [True] * len(gens)
        while any(alive):
            i = min((j for j in range(len(gens)) if alive[j]), key=lambda j: (done[j] + 0.5) / lens[j])
            try:
                next(gens[i])
                done[i] += 1
            except StopIteration:
                alive[i] = False

    bufs = ((uw0_s, qk0_s, qg0_s, kd0_s, el0_s), (uw1_s, qk1_s, qg1_s, kd1_s, el1_s))

    @pl.when(pl.program_id(0) == 0)
    def _():
        meta_ok = lax.broadcasted_iota(jnp.int32, (C, 128), 0) >= META_PAD
        dn0 = dnm_ref[...]
        act0 = dn0[:, 0:3 * DN_WIDTH].astype(F32)
        gc0, beta0 = gates(abm_ref[...], meta_ok)
        run((phase_a([act0], [gc0], [beta0], bufs[1]), 1))
        out0 = []
        run((phase_b([dn0[:, 3 * DN_WIDTH:].astype(F32)], bufs[1], out0), 1))
        ymeta_s[...] = out0[0]
        smeta_s[...] = s_s[...]

    s_s[...] = smeta_s[...]
    ym_ref[0] = ymeta_s[...]

    rows = [slice(c * C, (c + 1) * C) for c in range(DN_GROUP)]
    a_steps = DN_GROUP + 10
    b_steps = DN_GROUP + 1

    def group_a(p, buf):
        r0 = p * DN_GROUP_ROWS
        if not isinstance(p, int):
            r0 = pl.multiple_of(r0, DN_GROUP_ROWS)
        act = dnx_ref[0, pl.ds(r0, DN_GROUP_ROWS), 0:3 * DN_WIDTH].astype(F32)
        gc, beta = gates(abx_ref[0, pl.ds(r0, DN_GROUP_ROWS), :], None)
        yield
        yield from phase_a([act[r, :] for r in rows], [gc[r, :] for r in rows], [beta[r, :] for r in rows], buf)

    def group_b(p, buf):
        r0 = p * DN_GROUP_ROWS
        if not isinstance(p, int):
            r0 = pl.multiple_of(r0, DN_GROUP_ROWS)
        z = dnx_ref[0, pl.ds(r0, DN_GROUP_ROWS), 3 * DN_WIDTH:].astype(F32)
        outs = []
        yield from phase_b([z[r, :] for r in rows], buf, outs)
        yx_ref[0, pl.ds(r0, DN_GROUP_ROWS), :] = jnp.concatenate(outs, axis=0)

    n_groups = SEQ // DN_GROUP_ROWS
    run((group_a(0, bufs[0]), a_steps))

    def body(kk, carry):
        p = 2 * kk
        run((group_a(p + 1, bufs[1]), a_steps), (group_b(p, bufs[0]), b_steps))
        run((group_a(p + 2, bufs[0]), a_steps), (group_b(p + 1, bufs[1]), b_steps))
        return carry

    lax.fori_loop(0, n_groups // 2 - 1, body, 0)
    run((group_a(n_groups - 1, bufs[1]), a_steps), (group_b(n_groups - 2, bufs[0]), b_steps))
    run((group_b(n_groups - 1, bufs[1]), b_steps))


def _deltanet(dnx, dnm, abx, abm, alog, dtb, onw):
    nb = dnx.shape[0]
    const = lambda shape: pl.BlockSpec(shape, lambda b: (0,) * len(shape))
    return pl.pallas_call(
        _deltanet_kernel,
        grid=(nb,),
        in_specs=[
            pl.BlockSpec((1, SEQ, N_DN), lambda b: (b, 0, 0)),
            const((META_ROWS, N_DN)),
            pl.BlockSpec((1, SEQ, N_AB), lambda b: (b, 0, 0)),
            const((META_ROWS, N_AB)),
            const((1, 128)), const((1, 128)), const((1, 128)),
        ],
        out_specs=[
            pl.BlockSpec((1, SEQ, DN_WIDTH), lambda b: (b, 0, 0)),
            pl.BlockSpec((1, META_ROWS, DN_WIDTH), lambda b: (b, 0, 0)),
        ],
        out_shape=[
            jax.ShapeDtypeStruct((nb, SEQ, DN_WIDTH), BF16),
            jax.ShapeDtypeStruct((nb, META_ROWS, DN_WIDTH), BF16),
        ],
        scratch_shapes=[
            pltpu.VMEM((DN_HEADS, DN_DIM, DN_DIM), F32),
            pltpu.VMEM((DN_GROUP, DN_STACK, DN_STACK), F32),
            pltpu.VMEM((DN_GROUP, DN_STACK, DN_STACK), F32),
            pltpu.VMEM((DN_GROUP, DN_STACK, DN_STACK), BF16),
            pltpu.VMEM((DN_GROUP, DN_STACK, DN_STACK), BF16),
            pltpu.VMEM((DN_GROUP, DN_STACK, 2 * DN_DIM), BF16),
        ] + 2 * [
            pltpu.VMEM((DN_GROUP, DN_STACK, 2 * DN_DIM), F32),
            pltpu.VMEM((DN_GROUP, DN_STACK, DN_STACK), BF16),
            pltpu.VMEM((DN_GROUP, DN_STACK, DN_DIM), BF16),
            pltpu.VMEM((DN_GROUP, DN_STACK, DN_DIM), BF16),
            pltpu.VMEM((DN_GROUP, DN_STACK, DN_DIM), F32),
        ] + [
            pltpu.VMEM((DN_HEADS, DN_DIM, DN_DIM), F32),
            pltpu.VMEM((META_ROWS, DN_WIDTH), BF16),
        ],
        compiler_params=_cparams(("arbitrary",)),
        name="deltanet",
    )(dnx, dnm, abx, abm, alog, dtb, onw)


FFN_ROWS = 1024
FFN_HALO = 16
N_FF_BLK = D_FF // FF_BLK


def _ffn_kernel(x_ref, xh_ref, mh_ref, ya_ref, yah_ref, yam_ref, yd_ref, ydh_ref, ydm_ref,
                wo_ref, nw_ref, wg_ref, wu_ref, cw_ref, cb_ref, wd_ref,
                o_ref, u_s, g0_s, g1_s, up0_s, up1_s, act_s):
    r = pl.program_id(1)
    mixed = jnp.concatenate([ya_ref[0], yd_ref[0]], axis=1)
    h_mid = x_ref[0] + _dot(mixed, wo_ref[...])
    o_ref[0] = h_mid
    u_s[FFN_HALO:, :] = _rms(h_mid, nw_ref[...]).astype(BF16)
    first = r == 0
    mixed_h = jnp.concatenate([jnp.where(first, yam_ref[0], yah_ref[0]),
                               jnp.where(first, ydm_ref[0], ydh_ref[0])], axis=1)
    h_halo = jnp.where(first, mh_ref[...], xh_ref[0]) + _dot(mixed_h, wo_ref[...])
    u_s[0:FFN_HALO, :] = _rms(h_halo, nw_ref[...]).astype(BF16)

    g_bufs = (g0_s, g1_s)
    up_bufs = (up0_s, up1_s)

    def project(f):
        cols = slice(f * FF_BLK, (f + 1) * FF_BLK)
        g_bufs[f % 2][...] = _dot(u_s[...], wg_ref[:, cols])
        up_bufs[f % 2][...] = _dot(u_s[FFN_HALO:, :], wu_ref[:, cols])

    project(0)
    for f in range(N_FF_BLK):
        if f + 1 < N_FF_BLK:
            project(f + 1)
        g_s = g_bufs[f % 2]
        cols = slice(f * FF_BLK, (f + 1) * FF_BLK)
        gate = (cw_ref[2:3, cols] * g_s[FFN_HALO:, :]
                + cw_ref[1:2, cols] * g_s[FFN_HALO - 1:FFN_HALO - 1 + FFN_ROWS, :]
                + cw_ref[0:1, cols] * g_s[FFN_HALO - 2:FFN_HALO - 2 + FFN_ROWS, :]
                + cb_ref[:, cols])
        act_s[:, cols] = (_silu(gate) * up_bufs[f % 2][...]).astype(BF16)
    o_ref[0] += _dot(act_s[...], wd_ref[...])


def _ffn(x, hp_meta, yax, yam, ydx, ydm, wo, nw, wg, wu, cw, cb, wd):
    nb = x.shape[0]
    nr = SEQ // FFN_ROWS
    hb = FFN_ROWS // FFN_HALO
    halo_idx = lambda b, r: (b, jnp.maximum(r * hb - 1, 0), 0)
    meta_idx = lambda b, r: (b, META_ROWS // FFN_HALO - 1, 0)
    main_idx = lambda b, r: (b, r, 0)
    resident = lambda shape: pl.BlockSpec(shape, lambda b, r: (0,) * len(shape),
                                          pipeline_mode=pl.Buffered(1))
    return pl.pallas_call(
        _ffn_kernel,
        grid=(nb, nr),
        in_specs=[
            pl.BlockSpec((1, FFN_ROWS, D_MODEL), main_idx),
            pl.BlockSpec((1, FFN_HALO, D_MODEL), halo_idx),
            pl.BlockSpec((FFN_HALO, D_MODEL), lambda b, r: (META_ROWS // FFN_HALO - 1, 0)),
            pl.BlockSpec((1, FFN_ROWS, MLA_HEADS * V_HEAD), main_idx),
            pl.BlockSpec((1, FFN_HALO, MLA_HEADS * V_HEAD), halo_idx),
            pl.BlockSpec((1, FFN_HALO, MLA_HEADS * V_HEAD), meta_idx),
            pl.BlockSpec((1, FFN_ROWS, DN_WIDTH), main_idx),
            pl.BlockSpec((1, FFN_HALO, DN_WIDTH), halo_idx),
            pl.BlockSpec((1, FFN_HALO, DN_WIDTH), meta_idx),
            resident((D_MODEL, D_MODEL)),
            resident((1, D_MODEL)),
            resident((D_MODEL, D_FF)),
            resident((D_MODEL, D_FF)),
            resident((3, D_FF)),
            resident((1, D_FF)),
            resident((D_FF, D_MODEL)),
        ],
        out_specs=pl.BlockSpec((1, FFN_ROWS, D_MODEL), main_idx),
        out_shape=jax.ShapeDtypeStruct((nb, SEQ, D_MODEL), F32),
        scratch_shapes=[
            pltpu.VMEM((FFN_HALO + FFN_ROWS, D_MODEL), BF16),
            pltpu.VMEM((FFN_HALO + FFN_ROWS, FF_BLK), F32),
            pltpu.VMEM((FFN_HALO + FFN_ROWS, FF_BLK), F32),
            pltpu.VMEM((FFN_ROWS, FF_BLK), F32),
            pltpu.VMEM((FFN_ROWS, FF_BLK), F32),
            pltpu.VMEM((FFN_ROWS, D_FF), BF16),
        ],
        compiler_params=_cparams(("arbitrary", "arbitrary")),
        name="outproj_ffn",
    )(x, x, hp_meta, yax, yax, yam, ydx, ydx, ydm, wo, nw, wg, wu, cw, cb, wd)


def _rot_cols(w):
    half = QK_ROPE // 2
    return jnp.concatenate([-w[..., half:], w[..., :half]], axis=-1)


def _swap_halves(w):
    half = QK_ROPE // 2
    return jnp.concatenate([w[..., half:], w[..., :half]], axis=-1)


def _pad_lanes(v, n=128):
    return jnp.pad(v.astype(F32), (0, n - v.shape[0])).reshape(1, n)


def _layer(x, hp_meta, l, attn_norm_w, w_in, q_a_norm_w, w_q_b, kv_a_norm_w, w_kv_b, q_norm_w,
           k_norm_w, mla_out_norm_w, dn_conv_w, dn_A_log, dn_dt_bias, dn_out_norm_w, w_out,
           ffn_norm_w, w_gate, w_up, ffn_conv_w, ffn_conv_b, w_down):
    nb = x.shape[0]
    c1 = Q_LORA
    c2 = c1 + KV_LORA
    c3 = c2 + QK_ROPE
    c4 = c3 + 3 * DN_WIDTH
    c5 = c4 + DN_WIDTH
    wint = w_in[l].T.astype(BF16)
    k_pe_w = wint[c2:c3, :]
    half = QK_ROPE // 2
    w1 = jnp.concatenate(
        [wint[:c2, :], k_pe_w, -k_pe_w[half:, :], k_pe_w[:half, :], wint[c3:c5, :], wint[c5:, :],
         jnp.zeros((N_AB - 2 * DN_HEADS, D_MODEL), BF16)], axis=0).T

    wqb = w_q_b[l].reshape(Q_LORA, MLA_HEADS, QK_HEAD)
    wqt = jnp.concatenate([wqb[..., :QK_NOPE], wqb[..., QK_NOPE:], _rot_cols(wqb[..., QK_NOPE:])],
                          axis=-1).reshape(Q_LORA, MLA_HEADS * HEAD_SLOT).T.astype(BF16)
    wkvb = w_kv_b[l].reshape(KV_LORA, MLA_HEADS, QK_NOPE + V_HEAD)
    wk = wkvb[..., :QK_NOPE].reshape(KV_LORA, MLA_HEADS * QK_NOPE).astype(BF16)
    wvt = wkvb[..., QK_NOPE:].reshape(KV_LORA, MLA_HEADS * V_HEAD).T.astype(BF16)
    qn = q_norm_w[l].astype(F32)
    kn = k_norm_w[l].astype(F32)
    qnn = qn[:QK_NOPE].reshape(QK_NOPE, 1)
    knn = kn[:QK_NOPE].reshape(1, 128)
    qrw = jnp.concatenate([qn[QK_NOPE:], _swap_halves(qn[QK_NOPE:])]).reshape(2 * QK_ROPE, 1)
    krw = jnp.concatenate([kn[QK_NOPE:], _swap_halves(kn[QK_NOPE:])]).reshape(1, 128)

    half = QK_ROPE // 2
    inv_freq = ROPE_THETA ** (-jnp.arange(half, dtype=F32) / half)
    pos = (jnp.arange(ATT_ROWS, dtype=jnp.int32) - ATT_META_VALID).astype(F32)
    ang = pos[:, None] * inv_freq[None, :]
    cs = jnp.concatenate([jnp.cos(ang), jnp.cos(ang), jnp.sin(ang), jnp.sin(ang)], axis=1)
    cst = cs.T

    nw1 = attn_norm_w[l].astype(F32).reshape(1, D_MODEL)
    qaw = q_a_norm_w[l].astype(F32).reshape(1, Q_LORA)
    kvaw = kv_a_norm_w[l].astype(F32).reshape(1, KV_LORA)
    cw = dn_conv_w[l].astype(F32)
    no_hist = jnp.zeros((CONV_HIST, 3 * DN_WIDTH), F32)
    latm, dnm, abm, meta_tail = _inproj(hp_meta, nw1, w1, qaw, kvaw, cw, no_hist, META_ROWS, 1)
    latx, dnx, abx, _ = _inproj(x.reshape(nb * SEQ, D_MODEL), nw1, w1, qaw, kvaw, cw, meta_tail,
                                INPROJ_ROWS, SEQ // INPROJ_ROWS)
    latx = latx.reshape(nb, SEQ, N_LAT)
    dnx = dnx.reshape(nb, SEQ, N_DN)
    abx = abx.reshape(nb, SEQ, N_AB)

    yax, yam = _mla(latx, latm, cs, cst, wqt, wk, wvt,
                    qnn, qrw, knn, krw, mla_out_norm_w[l].astype(F32).reshape(V_HEAD, 1))
    ydx, ydm = _deltanet(dnx, dnm, abx, abm,
                         _pad_lanes(dn_A_log[l]), _pad_lanes(dn_dt_bias[l]),
                         dn_out_norm_w[l].astype(F32).reshape(1, DN_DIM))
    return _ffn(x, hp_meta, yax, yam, ydx, ydm, w_out[l].astype(BF16),
                ffn_norm_w[l].astype(F32).reshape(1, D_MODEL), w_gate[l].astype(BF16),
                w_up[l].astype(BF16), ffn_conv_w[l].astype(F32),
                ffn_conv_b[l].astype(F32).reshape(1, D_FF), w_down[l].astype(BF16))


def kernel(x, meta_tokens, attn_norm_w, w_in, q_a_norm_w, w_q_b, kv_a_norm_w, w_kv_b, q_norm_w, k_norm_w, mla_out_norm_w, dn_conv_w, dn_A_log, dn_dt_bias, dn_out_norm_w, w_out, ffn_norm_w, w_gate, w_up, ffn_conv_w, ffn_conv_b, w_down):
    assert x.shape[1:] == (SEQ, D_MODEL) and w_in.shape[0] == 1
    hp_meta = jnp.concatenate([jnp.zeros((META_PAD, D_MODEL), x.dtype), meta_tokens.astype(x.dtype)], axis=0)
    return _layer(x, hp_meta, 0, attn_norm_w, w_in, q_a_norm_w, w_q_b, kv_a_norm_w, w_kv_b, q_norm_w,
                  k_norm_w, mla_out_norm_w, dn_conv_w, dn_A_log, dn_dt_bias, dn_out_norm_w, w_out,
                  ffn_norm_w, w_gate, w_up, ffn_conv_w, ffn_conv_b, w_down)
```

```python
import functools
import math

import jax
import jax.numpy as jnp
from jax import lax
from jax.experimental import pallas as pl
from jax.experimental.pallas import tpu as pltpu

F32 = jnp.float32
BF16 = jnp.bfloat16

D_MODEL = 1024
SEQ = 2048
N_META = 16
META_ROWS = 64
META_PAD = META_ROWS - N_META

MLA_HEADS = 4
QK_NOPE = 128
QK_ROPE = 64
QK_HEAD = QK_NOPE + QK_ROPE
V_HEAD = 128
Q_LORA = 256
KV_LORA = 256
ROPE_THETA = 10000.0
HEAD_SLOT = 256

DN_HEADS = 4
DN_DIM = 128
DN_WIDTH = DN_HEADS * DN_DIM
DN_CHUNK = 64
DN_STACK = DN_HEADS * DN_CHUNK
DN_CONV = 4

D_FF = 2816
FF_BLK = 256
NORM_EPS = 1e-6

C_LAT = 0
N_LAT = Q_LORA + KV_LORA + 2 * QK_ROPE
C_DN = N_LAT
N_DN = 4 * DN_WIDTH
C_AB = C_DN + N_DN
N_AB = 128
N_PROJ = C_AB + N_AB

VMEM_LIMIT = 56 * 1024 * 1024


def _cparams(sem):
    return pltpu.CompilerParams(dimension_semantics=sem, vmem_limit_bytes=VMEM_LIMIT)


def _rms(x, w):
    return x * lax.rsqrt(jnp.mean(x * x, axis=-1, keepdims=True) + NORM_EPS) * w


def _dot(a, b):
    return jnp.dot(a, b, preferred_element_type=F32)


def _dot_nt(a, b):
    return lax.dot_general(a, b, (((1,), (1,)), ((), ())), preferred_element_type=F32)


def _dot_tn(a, b):
    return lax.dot_general(a, b, (((0,), (0,)), ((), ())), preferred_element_type=F32)


def _silu(x):
    return x * jax.nn.sigmoid(x)


assert DN_CONV == 4
CONV_HIST = 16
INPROJ_ROWS = 1024
INPROJ_SUB = 256


def _inproj_kernel(tiles_per_seq, x_ref, nw_ref, w_ref, qaw_ref, kvaw_ref, cw_ref, hist_in_ref,
                   lat_ref, dn_ref, ab_ref, tail_ref, hist_s, p0_s, p1_s):
    n = x_ref.shape[0]
    sub = min(INPROJ_SUB, n)
    p_bufs = (p0_s, p1_s)

    @pl.when(pl.program_id(0) % tiles_per_seq == 0)
    def _():
        hist_s[...] = hist_in_ref[...]

    def project(r):
        u = _rms(x_ref[r * sub:(r + 1) * sub, :], nw_ref[...]).astype(BF16)
        p_bufs[r % 2][0:sub, :] = _dot(u, w_ref[...])

    def post(r):
        p = p_bufs[r % 2]
        rows = slice(r * sub, (r + 1) * sub)
        lat_ref[rows, 0:Q_LORA] = _rms(p[0:sub, 0:Q_LORA], qaw_ref[...]).astype(BF16)
        lat_ref[rows, Q_LORA:Q_LORA + KV_LORA] = _rms(p[0:sub, Q_LORA:Q_LORA + KV_LORA],
                                                      kvaw_ref[...]).astype(BF16)
        lat_ref[rows, Q_LORA + KV_LORA:N_LAT] = p[0:sub, Q_LORA + KV_LORA:N_LAT].astype(BF16)
        ab_ref[rows, :] = p[0:sub, C_AB:C_AB + N_AB]
        pre = p[0:sub, C_DN:C_DN + 3 * DN_WIDTH]
        full = jnp.concatenate([hist_s[...], pre], axis=0)
        hist_s[...] = pre[sub - CONV_HIST:, :]
        full1 = pltpu.roll(full, 1, 0)
        near = cw_ref[3:4, :] * full + cw_ref[2:3, :] * full1
        far = cw_ref[1:2, :] * full + cw_ref[0:1, :] * full1
        act = _silu((near + pltpu.roll(far, 2, 0))[CONV_HIST:, :])
        for h in range(2 * DN_HEADS):
            a = act[:, h * DN_DIM:(h + 1) * DN_DIM]
            a = a * lax.rsqrt(jnp.sum(a * a, axis=-1, keepdims=True) + NORM_EPS)
            if h < DN_HEADS:
                a = a * (1.0 / math.sqrt(DN_DIM))
            dn_ref[rows, h * DN_DIM:(h + 1) * DN_DIM] = a.astype(BF16)
        dn_ref[rows, 2 * DN_WIDTH:3 * DN_WIDTH] = act[:, 2 * DN_WIDTH:].astype(BF16)
        dn_ref[rows, 3 * DN_WIDTH:] = _silu(p[0:sub, C_DN + 3 * DN_WIDTH:C_DN + N_DN]).astype(BF16)

    project(0)
    for r in range(n // sub):
        if r + 1 < n // sub:
            project(r + 1)
        post(r)
    tail_ref[...] = hist_s[...]


def _inproj(x2d, nw, w, qaw, kvaw, cw, hist_in, row_tile, tiles_per_seq):
    rows = x2d.shape[0]
    grid = (rows // row_tile,)
    const = lambda shape: pl.BlockSpec(shape, lambda i: (0,) * len(shape))
    return pl.pallas_call(
        functools.partial(_inproj_kernel, tiles_per_seq),
        grid=grid,
        in_specs=[
            pl.BlockSpec((row_tile, D_MODEL), lambda i: (i, 0)),
            const((1, D_MODEL)),
            const((D_MODEL, N_PROJ)),
            const((1, Q_LORA)),
            const((1, KV_LORA)),
            const((DN_CONV, 3 * DN_WIDTH)),
            const((CONV_HIST, 3 * DN_WIDTH)),
        ],
        out_specs=[
            pl.BlockSpec((row_tile, N_LAT), lambda i: (i, 0)),
            pl.BlockSpec((row_tile, N_DN), lambda i: (i, 0)),
            pl.BlockSpec((row_tile, N_AB), lambda i: (i, 0)),
            pl.BlockSpec((CONV_HIST, 3 * DN_WIDTH), lambda i: (i, 0)),
        ],
        out_shape=[
            jax.ShapeDtypeStruct((rows, N_LAT), BF16),
            jax.ShapeDtypeStruct((rows, N_DN), BF16),
            jax.ShapeDtypeStruct((rows, N_AB), F32),
            jax.ShapeDtypeStruct((grid[0] * CONV_HIST, 3 * DN_WIDTH), F32),
        ],
        scratch_shapes=[pltpu.VMEM((CONV_HIST, 3 * DN_WIDTH), F32),
                        pltpu.VMEM((min(INPROJ_SUB, row_tile), N_PROJ), F32),
                        pltpu.VMEM((min(INPROJ_SUB, row_tile), N_PROJ), F32)],
        compiler_params=_cparams(("arbitrary",)),
        name="inproj",
    )(x2d, nw, w, qaw, kvaw, cw, hist_in)


ATT_TQ = 256
V_SLOT = V_HEAD + 16
ATT_PROJ_ROWS = 512
ATT_META = 2 * META_ROWS
ATT_META_VALID = ATT_META - N_META
ATT_ROWS = ATT_META + SEQ
NEG_INF = float("-inf")


def _mla_kernel(latx_ref, latm_ref, cs_ref, cst_ref, wqt_ref, wk_ref, wvt_ref,
                qnn_ref, qrw_ref, knn_ref, krw_ref, onw_ref,
                yx_ref, ym_ref, km_s, qm_s, vm_s, kx_s, qx_s, vx_s, acc_s, stm_s, st0_s, st1_s, p_s):
    low = lax.broadcasted_iota(jnp.int32, (1, 128), 1) < QK_ROPE
    scale = 1.0 / math.sqrt(QK_HEAD)

    def project(lat, cs, cst):
        nrows = lat.shape[0]
        qn = lat[:, 0:Q_LORA]
        kvn = lat[:, Q_LORA:Q_LORA + KV_LORA]
        pe = lat[:, Q_LORA + KV_LORA:N_LAT]
        qt = _dot(wqt_ref[...], qn.T.astype(BF16))
        vt = _dot(wvt_ref[...], kvn.T.astype(BF16)).astype(BF16)
        kn = _dot(kvn.astype(BF16), wk_ref[...])
        a = pe * (cs * krw_ref[...])
        k_rope = jnp.where(low, a + pltpu.roll(a, QK_ROPE, 1), 0.0)
        pe_ss = jnp.sum(jnp.where(low, pe * pe, 0.0), axis=-1, keepdims=True)
        cos_t = cst[0:QK_ROPE, :]
        sin_t = cst[QK_ROPE:2 * QK_ROPE, :]
        k_parts = []
        q_parts = []
        for h in range(MLA_HEADS):
            nope = kn[:, h * QK_NOPE:(h + 1) * QK_NOPE]
            rs = lax.rsqrt((jnp.sum(nope * nope, axis=-1, keepdims=True) + pe_ss) * (1.0 / QK_HEAD) + NORM_EPS)
            k_parts += [(nope * rs * knn_ref[...]).astype(BF16), (k_rope * rs).astype(BF16)]
            r0 = h * HEAD_SLOT
            qnope = qt[r0:r0 + QK_NOPE, :]
            qrope = qt[r0 + QK_NOPE:r0 + QK_HEAD, :]
            qrot = qt[r0 + QK_HEAD:r0 + HEAD_SLOT, :]
            ssq = (jnp.sum(qnope * qnope, axis=0, keepdims=True)
                   + jnp.sum(qrope * qrope, axis=0, keepdims=True))
            rsq = lax.rsqrt(ssq * (1.0 / QK_HEAD) + NORM_EPS) * scale
            roped = (qrope * (qrw_ref[0:QK_ROPE, :] * cos_t)
                     + qrot * (qrw_ref[QK_ROPE:2 * QK_ROPE, :] * sin_t))
            q_parts += [(qnope * qnn_ref[...] * rsq).astype(BF16), (roped * rsq).astype(BF16),
                        jnp.zeros((HEAD_SLOT - QK_HEAD, nrows), BF16)]
        ones_row = (lax.broadcasted_iota(jnp.int32, (V_SLOT - V_HEAD, nrows), 0) == 0).astype(BF16)
        v_parts = []
        for h in range(MLA_HEADS):
            v_parts += [vt[h * V_HEAD:(h + 1) * V_HEAD, :], ones_row]
        return jnp.concatenate(k_parts, axis=1), jnp.concatenate(q_parts, axis=0), jnp.concatenate(v_parts, axis=0)

    latm = jnp.concatenate([jnp.zeros((META_ROWS, N_LAT), F32), latm_ref[...].astype(F32)], axis=0)
    km_s[...], qm_s[...], vm_s[...] = project(latm, cs_ref[0:ATT_META, :], cst_ref[:, 0:ATT_META])
    for c in range(SEQ // ATT_PROJ_ROWS):
        r0 = c * ATT_PROJ_ROWS
        k, qt, vt = project(latx_ref[0, r0:r0 + ATT_PROJ_ROWS, :].astype(F32),
                            cs_ref[ATT_META + r0:ATT_META + r0 + ATT_PROJ_ROWS, :],
                            cst_ref[:, ATT_META + r0:ATT_META + r0 + ATT_PROJ_ROWS])
        kx_s[r0:r0 + ATT_PROJ_ROWS, :] = k
        for t in range(ATT_PROJ_ROWS // ATT_TQ):
            qx_s[c * (ATT_PROJ_ROWS // ATT_TQ) + t] = qt[:, t * ATT_TQ:(t + 1) * ATT_TQ]
            vx_s[c * (ATT_PROJ_ROWS // ATT_TQ) + t] = vt[:, t * ATT_TQ:(t + 1) * ATT_TQ]

    def next_block(st, m):
        m_new = jnp.maximum(m, jnp.max(st, axis=0, keepdims=True))
        alpha = jnp.exp(m - m_new)
        p = jnp.exp((st - m_new).astype(BF16))
        return p, m_new, alpha

    def finish(acc_ext):
        o = acc_ext[0:V_HEAD, :] * (1.0 / acc_ext[V_HEAD:V_HEAD + 1, :])
        o = o * lax.rsqrt(jnp.mean(o * o, axis=0, keepdims=True) + NORM_EPS) * onw_ref[...]
        return o.T.astype(BF16)

    hs = lambda h: slice(h * HEAD_SLOT, (h + 1) * HEAD_SLOT)
    vs = lambda h: slice(h * V_SLOT, (h + 1) * V_SLOT)
    outs = lambda h: slice(h * V_HEAD, (h + 1) * V_HEAD)

    mkey = lax.broadcasted_iota(jnp.int32, (ATT_META, ATT_META), 0)
    mqry = lax.broadcasted_iota(jnp.int32, (ATT_META, ATT_META), 1)
    meta_mask = (mkey <= mqry) & ((mkey >= ATT_META_VALID) | (mkey == mqry))
    for h in range(MLA_HEADS):
        st = jnp.where(meta_mask, _dot(km_s[:, hs(h)], qm_s[hs(h), :]), NEG_INF)
        p, _, _ = next_block(st, jnp.full((1, ATT_META), NEG_INF, F32))
        o = finish(_dot(vm_s[vs(h), :], p))
        ym_ref[0, :, outs(h)] = o[META_ROWS:, :]

    meta_key_ok = lax.broadcasted_iota(jnp.int32, (ATT_META, ATT_TQ), 0) >= ATT_META_VALID
    diag_mask = (lax.broadcasted_iota(jnp.int32, (ATT_TQ, ATT_TQ), 0)
                 <= lax.broadcasted_iota(jnp.int32, (ATT_TQ, ATT_TQ), 1))

    def k_blk(j):
        k0 = j * ATT_TQ
        if not isinstance(j, int):
            k0 = pl.multiple_of(k0, ATT_TQ)
        return lambda h: kx_s[pl.ds(k0, ATT_TQ), hs(h)]

    v_blk = lambda j: (lambda h: vx_s[j, vs(h), :])

    acc_t, p_t, stm_t, st_t = acc_s, p_s, stm_s, (st0_s, st1_s)

    for i in range(SEQ // ATT_TQ):
        def scores(buf, k_of, nk):
            for h in range(MLA_HEADS):
                buf[h, 0:nk, :] = _dot(k_of(h), qx_s[i, hs(h), :])

        def absorb(buf, v_of, nk, mask, ms):
            ms2, alphas = [], []
            for h in range(MLA_HEADS):
                st = buf[h, 0:nk, :]
                if mask is not None:
                    st = jnp.where(mask, st, NEG_INF)
                p, m, alpha = next_block(st, ms[h])
                p_t[h, 0:nk, :] = p
                ms2.append(m)
                alphas.append(alpha)
            for h in range(MLA_HEADS):
                acc_t[h] = acc_t[h] * alphas[h] + _dot(v_of(h), p_t[h, 0:nk, :])
            return tuple(ms2)

        acc_t[...] = jnp.zeros_like(acc_t)
        scores(stm_t, lambda h: km_s[:, hs(h)], ATT_META)
        scores(st_t[0], k_blk(0), ATT_TQ)
        ms = (jnp.full((1, ATT_TQ), NEG_INF, F32),) * MLA_HEADS
        ms = absorb(stm_t, lambda h: vm_s[vs(h), :], ATT_META, meta_key_ok, ms)

        def pair(t, ms):
            scores(st_t[1], k_blk(2 * t + 1), ATT_TQ)
            ms = absorb(st_t[0], v_blk(2 * t), ATT_TQ, None, ms)
            scores(st_t[0], k_blk(2 * t + 2), ATT_TQ)
            return absorb(st_t[1], v_blk(2 * t + 1), ATT_TQ, None, ms)

        if i // 2 > 0:
            ms = lax.fori_loop(0, i // 2, pair, ms)
        if i % 2 == 1:
            scores(st_t[1], k_blk(i), ATT_TQ)
            ms = absorb(st_t[0], v_blk(i - 1), ATT_TQ, None, ms)
        absorb(st_t[i % 2], v_blk(i), ATT_TQ, diag_mask, ms)
        for h in range(MLA_HEADS):
            yx_ref[0, i * ATT_TQ:(i + 1) * ATT_TQ, outs(h)] = finish(acc_t[h])


def _mla(latx, latm, cs, cst, wqt, wk, wvt, qnn, qrw, knn, krw, onw):
    nb = latx.shape[0]
    const = lambda shape: pl.BlockSpec(shape, lambda b: (0,) * len(shape))
    nq = SEQ // ATT_TQ
    return pl.pallas_call(
        _mla_kernel,
        grid=(nb,),
        in_specs=[
            pl.BlockSpec((1, SEQ, N_LAT), lambda b: (b, 0, 0)),
            const((META_ROWS, N_LAT)),
            const((ATT_ROWS, 128)),
            const((128, ATT_ROWS)),
            const((MLA_HEADS * HEAD_SLOT, Q_LORA)),
            const((KV_LORA, MLA_HEADS * QK_NOPE)),
            const((MLA_HEADS * V_HEAD, KV_LORA)),
            const((QK_NOPE, 1)), const((2 * QK_ROPE, 1)), const((1, 128)), const((1, 128)),
            const((V_HEAD, 1)),
        ],
        out_specs=[
            pl.BlockSpec((1, SEQ, MLA_HEADS * V_HEAD), lambda b: (b, 0, 0)),
            pl.BlockSpec((1, META_ROWS, MLA_HEADS * V_HEAD), lambda b: (b, 0, 0)),
        ],
        out_shape=[
            jax.ShapeDtypeStruct((nb, SEQ, MLA_HEADS * V_HEAD), BF16),
            jax.ShapeDtypeStruct((nb, META_ROWS, MLA_HEADS * V_HEAD), BF16),
        ],
        scratch_shapes=[
            pltpu.VMEM((ATT_META, MLA_HEADS * HEAD_SLOT), BF16),
            pltpu.VMEM((MLA_HEADS * HEAD_SLOT, ATT_META), BF16),
            pltpu.VMEM((MLA_HEADS * V_SLOT, ATT_META), BF16),
            pltpu.VMEM((SEQ, MLA_HEADS * HEAD_SLOT), BF16),
            pltpu.VMEM((nq, MLA_HEADS * HEAD_SLOT, ATT_TQ), BF16),
            pltpu.VMEM((nq, MLA_HEADS * V_SLOT, ATT_TQ), BF16),
        ] + [
            pltpu.VMEM((MLA_HEADS, V_SLOT, ATT_TQ), F32),
            pltpu.VMEM((MLA_HEADS, ATT_META, ATT_TQ), F32),
            pltpu.VMEM((MLA_HEADS, ATT_TQ, ATT_TQ), F32),
            pltpu.VMEM((MLA_HEADS, ATT_TQ, ATT_TQ), F32),
            pltpu.VMEM((MLA_HEADS, ATT_TQ, ATT_TQ), BF16),
        ],
        compiler_params=_cparams(("arbitrary",)),
        name="mla",
    )(latx, latm, cs, cst, wqt, wk, wvt, qnn, qrw, knn, krw, onw)


DN_GROUP = 4
DN_GROUP_ROWS = DN_GROUP * DN_CHUNK


def _deltanet_kernel(dnx_ref, dnm_ref, abx_ref, abm_ref, alog_ref, dtb_ref, onw_ref,
                     yx_ref, ym_ref, s_s, af_s, t_s, pa_s, rhs_s,
                     uw0_s, qk0_s, qg0_s, kd0_s, el0_s, uw1_s, qk1_s, qg1_s, kd1_s, el1_s,
                     smeta_s, ymeta_s):
    C = DN_CHUNK
    R = DN_STACK
    row = lax.broadcasted_iota(jnp.int32, (R, R), 0)
    col = lax.broadcasted_iota(jnp.int32, (R, R), 1)
    same = lambda sh: jnp.right_shift(row, sh) == jnp.right_shift(col, sh)
    m_incl = same(6) & (col <= row)
    m_strict = same(6) & (col < row)
    m_d2 = m_strict & same(1)
    m_merges = [m_strict & same(sh + 1) & jnp.logical_not(same(sh)) for sh in range(1, 6)]
    eye = (row == col).astype(F32)
    neg_a = -jnp.exp(alog_ref[...])

    s_s[...] = jnp.zeros_like(s_s)

    def stack(x):
        return jnp.concatenate([x[:, h * DN_DIM:(h + 1) * DN_DIM] for h in range(DN_HEADS)], axis=0)

    def stack_col(x, c0):
        return jnp.concatenate(
            [jnp.broadcast_to(x[:, c0 + h:c0 + h + 1], (C, DN_DIM)) for h in range(DN_HEADS)], axis=0)

    def bdot(a, b):
        return _dot(a.astype(BF16), b.astype(BF16))

    def gates(ab, row_ok):
        xa = ab + dtb_ref[...]
        softplus = jnp.maximum(xa, 0.0) + jnp.log(1.0 + jnp.exp(-jnp.abs(xa)))
        g = neg_a * softplus
        beta = jax.nn.sigmoid(ab)
        if row_ok is not None:
            g = jnp.where(row_ok, g, 0.0)
            beta = jnp.where(row_ok, beta, 0.0)
        pos = lax.broadcasted_iota(jnp.int32, g.shape, 0) & (C - 1)
        gc = g
        for s in (1, 2, 4, 8, 16, 32):
            gc = gc + jnp.where(pos >= s, pltpu.roll(gc, s, 0), 0.0)
        return gc, beta

    def phase_a(acts, gcl, betal, buf):
        uw_b, qk_b, qg_b, kd_b, el_b = buf
        n = len(acts)
        for c in range(n):
            act, gc, beta = acts[c], gcl[c], betal[c]
            q = stack(act[:, 0:DN_WIDTH])
            k = stack(act[:, DN_WIDTH:2 * DN_WIDTH])
            v = stack(act[:, 2 * DN_WIDTH:3 * DN_WIDTH])
            gcs = stack_col(gc, 0)
            gls = stack_col(jnp.broadcast_to(gc[C - 1:C, :], (C, 128)), 0)
            bs = stack_col(beta, DN_HEADS)
            grow = gcs.T[0:1, :]
            dec = jnp.exp(jnp.where(m_incl, gcs[:, 0:1] - grow, NEG_INF))
            kb = k * bs
            kbf = k.astype(BF16)
            eg = jnp.exp(gcs)
            af_s[c] = _dot_nt(kb.astype(BF16), kbf) * dec
            qk_b[c] = jnp.where(m_incl, _dot_nt(q.astype(BF16), kbf) * dec, 0.0).astype(BF16)
            rhs_s[c] = jnp.concatenate([v * bs, kb * eg], axis=1).astype(BF16)
            qg_b[c] = (q * eg).astype(BF16)
            kd_b[c] = (k * jnp.exp(gls - gcs)).astype(BF16)
            el_b[c] = jnp.exp(gls)
            yield

        for c in range(n):
            t_s[c] = eye - jnp.where(m_d2, af_s[c], 0.0)
        for m_low in m_merges:
            for c in range(n):
                pa_s[c] = _dot(jnp.where(m_low, af_s[c], 0.0).astype(BF16), t_s[c].astype(BF16)).astype(BF16)
            yield
            for c in range(n):
                t_s[c] = t_s[c] - _dot(t_s[c].astype(BF16), pa_s[c])
            yield
        for c in range(n):
            uw_b[c] = _dot(t_s[c].astype(BF16), rhs_s[c])

    def phase_b(zs, buf, outs):
        uw_b, qk_b, qg_b, kd_b, el_b = buf
        for c in range(len(zs)):
            vnew = []
            o_inter = []
            for h in range(DN_HEADS):
                r0 = h * C
                s_h = s_s[h]
                sb = s_h.astype(BF16)
                vn = uw_b[c, r0:r0 + C, 0:DN_DIM] - _dot(uw_b[c, r0:r0 + C, DN_DIM:].astype(BF16), sb)
                o_inter.append(_dot(qg_b[c, r0:r0 + C, :], sb))
                s_s[h] = s_h * el_b[c, r0:r0 + 1, :] + _dot_tn(kd_b[c, r0:r0 + C, :], vn.astype(BF16))
                vnew.append(vn)
            o = jnp.concatenate(o_inter, axis=0) + _dot(qk_b[c], jnp.concatenate(vnew, axis=0).astype(BF16))
            o = _rms(o, onw_ref[...])
            out = jnp.concatenate([o[h * C:(h + 1) * C, :] for h in range(DN_HEADS)], axis=1) * zs[c]
            outs.append(out.astype(BF16))
            yield

    def run(*gens_and_steps):
        gens = [g for g, _ in gens_and_steps]
        lens = [s for _, s in gens_and_steps]
        done = [0] * len(gens)
        alive = [True] * len(gens)
        while any(alive):
            i = min((j for j in range(len(gens)) if alive[j]), key=lambda j: (done[j] + 0.5) / lens[j])
            try:
                next(gens[i])
                done[i] += 1
            except StopIteration:
                alive[i] = False

    bufs = ((uw0_s, qk0_s, qg0_s, kd0_s, el0_s), (uw1_s, qk1_s, qg1_s, kd1_s, el1_s))

    @pl.when(pl.program_id(0) == 0)
    def _():
        meta_ok = lax.broadcasted_iota(jnp.int32, (C, 128), 0) >= META_PAD
        dn0 = dnm_ref[...]
        act0 = dn0[:, 0:3 * DN_WIDTH].astype(F32)
        gc0, beta0 = gates(abm_ref[...], meta_ok)
        run((phase_a([act0], [gc0], [beta0], bufs[1]), 1))
        out0 = []
        run((phase_b([dn0[:, 3 * DN_WIDTH:].astype(F32)], bufs[1], out0), 1))
        ymeta_s[...] = out0[0]
        smeta_s[...] = s_s[...]

    s_s[...] = smeta_s[...]
    ym_ref[0] = ymeta_s[...]

    rows = [slice(c * C, (c + 1) * C) for c in range(DN_GROUP)]
    a_steps = DN_GROUP + 12
    b_steps = DN_GROUP + 1

    def group_a(p, buf):
        r0 = p * DN_GROUP_ROWS
        if not isinstance(p, int):
            r0 = pl.multiple_of(r0, DN_GROUP_ROWS)
        act = dnx_ref[0, pl.ds(r0, DN_GROUP_ROWS), 0:3 * DN_WIDTH].astype(F32)
        gc, beta = gates(abx_ref[0, pl.ds(r0, DN_GROUP_ROWS), :], None)
        yield
        yield from phase_a([act[r, :] for r in rows], [gc[r, :] for r in rows], [beta[r, :] for r in rows], buf)

    def group_b(p, buf):
        r0 = p * DN_GROUP_ROWS
        if not isinstance(p, int):
            r0 = pl.multiple_of(r0, DN_GROUP_ROWS)
        z = dnx_ref[0, pl.ds(r0, DN_GROUP_ROWS), 3 * DN_WIDTH:].astype(F32)
        outs = []
        yield from phase_b([z[r, :] for r in rows], buf, outs)
        yx_ref[0, pl.ds(r0, DN_GROUP_ROWS), :] = jnp.concatenate(outs, axis=0)

    n_groups = SEQ // DN_GROUP_ROWS
    run((group_a(0, bufs[0]), a_steps))

    def body(kk, carry):
        p = 2 * kk
        run((group_a(p + 1, bufs[1]), a_steps), (group_b(p, bufs[0]), b_steps))
        run((group_a(p + 2, bufs[0]), a_steps), (group_b(p + 1, bufs[1]), b_steps))
        return carry

    lax.fori_loop(0, n_groups // 2 - 1, body, 0)
    run((group_a(n_groups - 1, bufs[1]), a_steps), (group_b(n_groups - 2, bufs[0]), b_steps))
    run((group_b(n_groups - 1, bufs[1]), b_steps))


def _deltanet(dnx, dnm, abx, abm, alog, dtb, onw):
    nb = dnx.shape[0]
    const = lambda shape: pl.BlockSpec(shape, lambda b: (0,) * len(shape))
    return pl.pallas_call(
        _deltanet_kernel,
        grid=(nb,),
        in_specs=[
            pl.BlockSpec((1, SEQ, N_DN), lambda b: (b, 0, 0)),
            const((META_ROWS, N_DN)),
            pl.BlockSpec((1, SEQ, N_AB), lambda b: (b, 0, 0)),
            const((META_ROWS, N_AB)),
            const((1, 128)), const((1, 128)), const((1, 128)),
        ],
        out_specs=[
            pl.BlockSpec((1, SEQ, DN_WIDTH), lambda b: (b, 0, 0)),
            pl.BlockSpec((1, META_ROWS, DN_WIDTH), lambda b: (b, 0, 0)),
        ],
        out_shape=[
            jax.ShapeDtypeStruct((nb, SEQ, DN_WIDTH), BF16),
            jax.ShapeDtypeStruct((nb, META_ROWS, DN_WIDTH), BF16),
        ],
        scratch_shapes=[
            pltpu.VMEM((DN_HEADS, DN_DIM, DN_DIM), F32),
            pltpu.VMEM((DN_GROUP, DN_STACK, DN_STACK), F32),
            pltpu.VMEM((DN_GROUP, DN_STACK, DN_STACK), F32),
            pltpu.VMEM((DN_GROUP, DN_STACK, DN_STACK), BF16),
            pltpu.VMEM((DN_GROUP, DN_STACK, 2 * DN_DIM), BF16),
        ] + 2 * [
            pltpu.VMEM((DN_GROUP, DN_STACK, 2 * DN_DIM), F32),
            pltpu.VMEM((DN_GROUP, DN_STACK, DN_STACK), BF16),
            pltpu.VMEM((DN_GROUP, DN_STACK, DN_DIM), BF16),
            pltpu.VMEM((DN_GROUP, DN_STACK, DN_DIM), BF16),
            pltpu.VMEM((DN_GROUP, DN_STACK, DN_DIM), F32),
        ] + [
            pltpu.VMEM((DN_HEADS, DN_DIM, DN_DIM), F32),
            pltpu.VMEM((META_ROWS, DN_WIDTH), BF16),
        ],
        compiler_params=_cparams(("arbitrary",)),
        name="deltanet",
    )(dnx, dnm, abx, abm, alog, dtb, onw)


FFN_ROWS = 1024
FFN_HALO = 16
N_FF_BLK = D_FF // FF_BLK


def _ffn_kernel(x_ref, xh_ref, mh_ref, ya_ref, yah_ref, yam_ref, yd_ref, ydh_ref, ydm_ref,
                wo_ref, nw_ref, wg_ref, wu_ref, cw_ref, cb_ref, wd_ref,
                o_ref, u_s, g0_s, g1_s, up0_s, up1_s, act_s):
    r = pl.program_id(1)
    mixed = jnp.concatenate([ya_ref[0], yd_ref[0]], axis=1)
    h_mid = x_ref[0] + _dot(mixed, wo_ref[...])
    o_ref[0] = h_mid
    u_s[FFN_HALO:, :] = _rms(h_mid, nw_ref[...]).astype(BF16)
    first = r == 0
    mixed_h = jnp.concatenate([jnp.where(first, yam_ref[0], yah_ref[0]),
                               jnp.where(first, ydm_ref[0], ydh_ref[0])], axis=1)
    h_halo = jnp.where(first, mh_ref[...], xh_ref[0]) + _dot(mixed_h, wo_ref[...])
    u_s[0:FFN_HALO, :] = _rms(h_halo, nw_ref[...]).astype(BF16)

    g_bufs = (g0_s, g1_s)
    up_bufs = (up0_s, up1_s)

    def project(f):
        cols = slice(f * FF_BLK, (f + 1) * FF_BLK)
        g_bufs[f % 2][...] = _dot(u_s[...], wg_ref[:, cols])
        up_bufs[f % 2][...] = _dot(u_s[FFN_HALO:, :], wu_ref[:, cols])

    project(0)
    for f in range(N_FF_BLK):
        if f + 1 < N_FF_BLK:
            project(f + 1)
        g_s = g_bufs[f % 2]
        cols = slice(f * FF_BLK, (f + 1) * FF_BLK)
        gate = (cw_ref[2:3, cols] * g_s[FFN_HALO:, :]
                + cw_ref[1:2, cols] * g_s[FFN_HALO - 1:FFN_HALO - 1 + FFN_ROWS, :]
                + cw_ref[0:1, cols] * g_s[FFN_HALO - 2:FFN_HALO - 2 + FFN_ROWS, :]
                + cb_ref[:, cols])
        act_s[:, cols] = (_silu(gate) * up_bufs[f % 2][...]).astype(BF16)
    o_ref[0] += _dot(act_s[...], wd_ref[...])


def _ffn(x, hp_meta, yax, yam, ydx, ydm, wo, nw, wg, wu, cw, cb, wd):
    nb = x.shape[0]
    nr = SEQ // FFN_ROWS
    hb = FFN_ROWS // FFN_HALO
    halo_idx = lambda b, r: (b, jnp.maximum(r * hb - 1, 0), 0)
    meta_idx = lambda b, r: (b, META_ROWS // FFN_HALO - 1, 0)
    main_idx = lambda b, r: (b, r, 0)
    resident = lambda shape: pl.BlockSpec(shape, lambda b, r: (0,) * len(shape),
                                          pipeline_mode=pl.Buffered(1))
    return pl.pallas_call(
        _ffn_kernel,
        grid=(nb, nr),
        in_specs=[
            pl.BlockSpec((1, FFN_ROWS, D_MODEL), main_idx),
            pl.BlockSpec((1, FFN_HALO, D_MODEL), halo_idx),
            pl.BlockSpec((FFN_HALO, D_MODEL), lambda b, r: (META_ROWS // FFN_HALO - 1, 0)),
            pl.BlockSpec((1, FFN_ROWS, MLA_HEADS * V_HEAD), main_idx),
            pl.BlockSpec((1, FFN_HALO, MLA_HEADS * V_HEAD), halo_idx),
            pl.BlockSpec((1, FFN_HALO, MLA_HEADS * V_HEAD), meta_idx),
            pl.BlockSpec((1, FFN_ROWS, DN_WIDTH), main_idx),
            pl.BlockSpec((1, FFN_HALO, DN_WIDTH), halo_idx),
            pl.BlockSpec((1, FFN_HALO, DN_WIDTH), meta_idx),
            resident((D_MODEL, D_MODEL)),
            resident((1, D_MODEL)),
            resident((D_MODEL, D_FF)),
            resident((D_MODEL, D_FF)),
            resident((3, D_FF)),
            resident((1, D_FF)),
            resident((D_FF, D_MODEL)),
        ],
        out_specs=pl.BlockSpec((1, FFN_ROWS, D_MODEL), main_idx),
        out_shape=jax.ShapeDtypeStruct((nb, SEQ, D_MODEL), F32),
        scratch_shapes=[
            pltpu.VMEM((FFN_HALO + FFN_ROWS, D_MODEL), BF16),
            pltpu.VMEM((FFN_HALO + FFN_ROWS, FF_BLK), F32),
            pltpu.VMEM((FFN_HALO + FFN_ROWS, FF_BLK), F32),
            pltpu.VMEM((FFN_ROWS, FF_BLK), F32),
            pltpu.VMEM((FFN_ROWS, FF_BLK), F32),
            pltpu.VMEM((FFN_ROWS, D_FF), BF16),
        ],
        compiler_params=_cparams(("arbitrary", "arbitrary")),
        name="outproj_ffn",
    )(x, x, hp_meta, yax, yax, yam, ydx, ydx, ydm, wo, nw, wg, wu, cw, cb, wd)


def _rot_cols(w):
    half = QK_ROPE // 2
    return jnp.concatenate([-w[..., half:], w[..., :half]], axis=-1)


def _swap_halves(w):
    half = QK_ROPE // 2
    return jnp.concatenate([w[..., half:], w[..., :half]], axis=-1)


def _pad_lanes(v, n=128):
    return jnp.pad(v.astype(F32), (0, n - v.shape[0])).reshape(1, n)


def _layer(x, hp_meta, l, attn_norm_w, w_in, q_a_norm_w, w_q_b, kv_a_norm_w, w_kv_b, q_norm_w,
           k_norm_w, mla_out_norm_w, dn_conv_w, dn_A_log, dn_dt_bias, dn_out_norm_w, w_out,
           ffn_norm_w, w_gate, w_up, ffn_conv_w, ffn_conv_b, w_down):
    nb = x.shape[0]
    c1 = Q_LORA
    c2 = c1 + KV_LORA
    c3 = c2 + QK_ROPE
    c4 = c3 + 3 * DN_WIDTH
    c5 = c4 + DN_WIDTH
    wint = w_in[l].T.astype(BF16)
    k_pe_w = wint[c2:c3, :]
    half = QK_ROPE // 2
    w1 = jnp.concatenate(
        [wint[:c2, :], k_pe_w, -k_pe_w[half:, :], k_pe_w[:half, :], wint[c3:c5, :], wint[c5:, :],
         jnp.zeros((N_AB - 2 * DN_HEADS, D_MODEL), BF16)], axis=0).T

    wqb = w_q_b[l].reshape(Q_LORA, MLA_HEADS, QK_HEAD)
    wqt = jnp.concatenate([wqb[..., :QK_NOPE], wqb[..., QK_NOPE:], _rot_cols(wqb[..., QK_NOPE:])],
                          axis=-1).reshape(Q_LORA, MLA_HEADS * HEAD_SLOT).T.astype(BF16)
    wkvb = w_kv_b[l].reshape(KV_LORA, MLA_HEADS, QK_NOPE + V_HEAD)
    wk = wkvb[..., :QK_NOPE].reshape(KV_LORA, MLA_HEADS * QK_NOPE).astype(BF16)
    wvt = wkvb[..., QK_NOPE:].reshape(KV_LORA, MLA_HEADS * V_HEAD).T.astype(BF16)
    qn = q_norm_w[l].astype(F32)
    kn = k_norm_w[l].astype(F32)
    qnn = qn[:QK_NOPE].reshape(QK_NOPE, 1)
    knn = kn[:QK_NOPE].reshape(1, 128)
    qrw = jnp.concatenate([qn[QK_NOPE:], _swap_halves(qn[QK_NOPE:])]).reshape(2 * QK_ROPE, 1)
    krw = jnp.concatenate([kn[QK_NOPE:], _swap_halves(kn[QK_NOPE:])]).reshape(1, 128)

    half = QK_ROPE // 2
    inv_freq = ROPE_THETA ** (-jnp.arange(half, dtype=F32) / half)
    pos = (jnp.arange(ATT_ROWS, dtype=jnp.int32) - ATT_META_VALID).astype(F32)
    ang = pos[:, None] * inv_freq[None, :]
    cs = jnp.concatenate([jnp.cos(ang), jnp.cos(ang), jnp.sin(ang), jnp.sin(ang)], axis=1)
    cst = cs.T

    nw1 = attn_norm_w[l].astype(F32).reshape(1, D_MODEL)
    qaw = q_a_norm_w[l].astype(F32).reshape(1, Q_LORA)
    kvaw = kv_a_norm_w[l].astype(F32).reshape(1, KV_LORA)
    cw = dn_conv_w[l].astype(F32)
    no_hist = jnp.zeros((CONV_HIST, 3 * DN_WIDTH), F32)
    latm, dnm, abm, meta_tail = _inproj(hp_meta, nw1, w1, qaw, kvaw, cw, no_hist, META_ROWS, 1)
    latx, dnx, abx, _ = _inproj(x.reshape(nb * SEQ, D_MODEL), nw1, w1, qaw, kvaw, cw, meta_tail,
                                INPROJ_ROWS, SEQ // INPROJ_ROWS)
    latx = latx.reshape(nb, SEQ, N_LAT)
    dnx = dnx.reshape(nb, SEQ, N_DN)
    abx = abx.reshape(nb, SEQ, N_AB)

    yax, yam = _mla(latx, latm, cs, cst, wqt, wk, wvt,
                    qnn, qrw, knn, krw, mla_out_norm_w[l].astype(F32).reshape(V_HEAD, 1))
    ydx, ydm = _deltanet(dnx, dnm, abx, abm,
                         _pad_lanes(dn_A_log[l]), _pad_lanes(dn_dt_bias[l]),
                         dn_out_norm_w[l].astype(F32).reshape(1, DN_DIM))
    return _ffn(x, hp_meta, yax, yam, ydx, ydm, w_out[l].astype(BF16),
                ffn_norm_w[l].astype(F32).reshape(1, D_MODEL), w_gate[l].astype(BF16),
                w_up[l].astype(BF16), ffn_conv_w[l].astype(F32),
                ffn_conv_b[l].astype(F32).reshape(1, D_FF), w_down[l].astype(BF16))


def kernel(x, meta_tokens, attn_norm_w, w_in, q_a_norm_w, w_q_b, kv_a_norm_w, w_kv_b, q_norm_w, k_norm_w, mla_out_norm_w, dn_conv_w, dn_A_log, dn_dt_bias, dn_out_norm_w, w_out, ffn_norm_w, w_gate, w_up, ffn_conv_w, ffn_conv_b, w_down):
    assert x.shape[1:] == (SEQ, D_MODEL) and w_in.shape[0] == 1
    hp_meta = jnp.concatenate([jnp.zeros((META_PAD, D_MODEL), x.dtype), meta_tokens.astype(x.dtype)], axis=0)
    return _layer(x, hp_meta, 0, attn_norm_w, w_in, q_a_norm_w, w_q_b, kv_a_norm_w, w_kv_b, q_norm_w,
                  k_norm_w, mla_out_norm_w, dn_conv_w, dn_A_log, dn_dt_bias, dn_out_norm_w, w_out,
                  ffn_norm_w, w_gate, w_up, ffn_conv_w, ffn_conv_b, w_down)
```

```python
import functools
import math

import jax
import jax.numpy as jnp
from jax import lax
from jax.experimental import pallas as pl
from jax.experimental.pallas import tpu as pltpu

F32 = jnp.float32
BF16 = jnp.bfloat16

D_MODEL = 1024
SEQ = 2048
N_META = 16
META_ROWS = 64
META_PAD = META_ROWS - N_META

MLA_HEADS = 4
QK_NOPE = 128
QK_ROPE = 64
QK_HEAD = QK_NOPE + QK_ROPE
V_HEAD = 128
Q_LORA = 256
KV_LORA = 256
ROPE_THETA = 10000.0
HEAD_SLOT = 256

DN_HEADS = 4
DN_DIM = 128
DN_WIDTH = DN_HEADS * DN_DIM
DN_CHUNK = 64
DN_STACK = DN_HEADS * DN_CHUNK
DN_CONV = 4

D_FF = 2816
FF_BLK = 256
NORM_EPS = 1e-6

N_LAT = Q_LORA + KV_LORA + 2 * QK_ROPE
C_DN = N_LAT
N_DN = 4 * DN_WIDTH
C_AB = C_DN + N_DN
N_AB = 128
N_PROJ = C_AB + N_AB

VMEM_LIMIT = 56 * 1024 * 1024


def _cparams(sem):
    return pltpu.CompilerParams(dimension_semantics=sem, vmem_limit_bytes=VMEM_LIMIT)


def _rms(x, w):
    return x * lax.rsqrt(jnp.mean(x * x, axis=-1, keepdims=True) + NORM_EPS) * w


def _dot(a, b):
    return jnp.dot(a, b, preferred_element_type=F32)


def _dot_nt(a, b):
    return lax.dot_general(a, b, (((1,), (1,)), ((), ())), preferred_element_type=F32)


def _dot_tn(a, b):
    return lax.dot_general(a, b, (((0,), (0,)), ((), ())), preferred_element_type=F32)


def _silu(x):
    return x * jax.nn.sigmoid(x)


assert DN_CONV == 4
CONV_HIST = 16
INPROJ_ROWS = 1024
INPROJ_SUB = 256


def _inproj_kernel(tiles_per_seq, x_ref, nw_ref, w_ref, qaw_ref, kvaw_ref, cw_ref, hist_in_ref,
                   lat_ref, dn_ref, ab_ref, tail_ref, hist_s, p0_s, p1_s):
    n = x_ref.shape[0]
    sub = min(INPROJ_SUB, n)
    p_bufs = (p0_s, p1_s)

    @pl.when(pl.program_id(0) % tiles_per_seq == 0)
    def _():
        hist_s[...] = hist_in_ref[...]

    def project(r):
        u = _rms(x_ref[r * sub:(r + 1) * sub, :], nw_ref[...]).astype(BF16)
        p_bufs[r % 2][0:sub, :] = _dot(u, w_ref[...])

    def post(r):
        p = p_bufs[r % 2]
        rows = slice(r * sub, (r + 1) * sub)
        lat_ref[rows, 0:Q_LORA] = _rms(p[0:sub, 0:Q_LORA], qaw_ref[...]).astype(BF16)
        lat_ref[rows, Q_LORA:Q_LORA + KV_LORA] = _rms(p[0:sub, Q_LORA:Q_LORA + KV_LORA],
                                                      kvaw_ref[...]).astype(BF16)
        lat_ref[rows, Q_LORA + KV_LORA:N_LAT] = p[0:sub, Q_LORA + KV_LORA:N_LAT].astype(BF16)
        ab_ref[rows, :] = p[0:sub, C_AB:C_AB + N_AB]
        pre = p[0:sub, C_DN:C_DN + 3 * DN_WIDTH]
        full = jnp.concatenate([hist_s[...], pre], axis=0)
        hist_s[...] = pre[sub - CONV_HIST:, :]
        full1 = pltpu.roll(full, 1, 0)
        near = cw_ref[3:4, :] * full + cw_ref[2:3, :] * full1
        far = cw_ref[1:2, :] * full + cw_ref[0:1, :] * full1
        act = _silu((near + pltpu.roll(far, 2, 0))[CONV_HIST:, :])
        for h in range(2 * DN_HEADS):
            a = act[:, h * DN_DIM:(h + 1) * DN_DIM]
            a = a * lax.rsqrt(jnp.sum(a * a, axis=-1, keepdims=True) + NORM_EPS)
            if h < DN_HEADS:
                a = a * (1.0 / math.sqrt(DN_DIM))
            dn_ref[rows, h * DN_DIM:(h + 1) * DN_DIM] = a.astype(BF16)
        dn_ref[rows, 2 * DN_WIDTH:3 * DN_WIDTH] = act[:, 2 * DN_WIDTH:].astype(BF16)
        dn_ref[rows, 3 * DN_WIDTH:] = _silu(p[0:sub, C_DN + 3 * DN_WIDTH:C_DN + N_DN]).astype(BF16)

    project(0)
    for r in range(n // sub):
        if r + 1 < n // sub:
            project(r + 1)
        post(r)
    tail_ref[...] = hist_s[...]


def _inproj(x2d, nw, w, qaw, kvaw, cw, hist_in, row_tile, tiles_per_seq):
    rows = x2d.shape[0]
    grid = (rows // row_tile,)
    const = lambda shape: pl.BlockSpec(shape, lambda i: (0,) * len(shape))
    return pl.pallas_call(
        functools.partial(_inproj_kernel, tiles_per_seq),
        grid=grid,
        in_specs=[
            pl.BlockSpec((row_tile, D_MODEL), lambda i: (i, 0)),
            const((1, D_MODEL)),
            const((D_MODEL, N_PROJ)),
            const((1, Q_LORA)),
            const((1, KV_LORA)),
            const((DN_CONV, 3 * DN_WIDTH)),
            const((CONV_HIST, 3 * DN_WIDTH)),
        ],
        out_specs=[
            pl.BlockSpec((row_tile, N_LAT), lambda i: (i, 0)),
            pl.BlockSpec((row_tile, N_DN), lambda i: (i, 0)),
            pl.BlockSpec((row_tile, N_AB), lambda i: (i, 0)),
            pl.BlockSpec((CONV_HIST, 3 * DN_WIDTH), lambda i: (i, 0)),
        ],
        out_shape=[
            jax.ShapeDtypeStruct((rows, N_LAT), BF16),
            jax.ShapeDtypeStruct((rows, N_DN), BF16),
            jax.ShapeDtypeStruct((rows, N_AB), F32),
            jax.ShapeDtypeStruct((grid[0] * CONV_HIST, 3 * DN_WIDTH), F32),
        ],
        scratch_shapes=[pltpu.VMEM((CONV_HIST, 3 * DN_WIDTH), F32),
                        pltpu.VMEM((min(INPROJ_SUB, row_tile), N_PROJ), F32),
                        pltpu.VMEM((min(INPROJ_SUB, row_tile), N_PROJ), F32)],
        compiler_params=_cparams(("arbitrary",)),
        name="inproj",
    )(x2d, nw, w, qaw, kvaw, cw, hist_in)


ATT_TQ = 256
V_SLOT = V_HEAD + 16
ATT_PROJ_ROWS = 512
ATT_META = 2 * META_ROWS
ATT_META_VALID = ATT_META - N_META
ATT_ROWS = ATT_META + SEQ
NEG_INF = float("-inf")


def _mla_kernel(latx_ref, latm_ref, cs_ref, cst_ref, wqt_ref, wk_ref, wvt_ref,
                qnn_ref, qrw_ref, knn_ref, krw_ref, onw_ref,
                yx_ref, ym_ref, km_s, qm_s, vm_s, kx_s, qx_s, vx_s, acc_s, stm_s, st0_s, st1_s, p_s):
    low = lax.broadcasted_iota(jnp.int32, (1, 128), 1) < QK_ROPE
    scale = 1.0 / math.sqrt(QK_HEAD)

    def project(lat, cs, cst):
        nrows = lat.shape[0]
        qn = lat[:, 0:Q_LORA]
        kvn = lat[:, Q_LORA:Q_LORA + KV_LORA]
        pe = lat[:, Q_LORA + KV_LORA:N_LAT]
        qt = _dot(wqt_ref[...], qn.T.astype(BF16))
        vt = _dot(wvt_ref[...], kvn.T.astype(BF16)).astype(BF16)
        kn = _dot(kvn.astype(BF16), wk_ref[...])
        a = pe * (cs * krw_ref[...])
        k_rope = jnp.where(low, a + pltpu.roll(a, QK_ROPE, 1), 0.0)
        pe_ss = jnp.sum(jnp.where(low, pe * pe, 0.0), axis=-1, keepdims=True)
        cos_t = cst[0:QK_ROPE, :]
        sin_t = cst[QK_ROPE:2 * QK_ROPE, :]
        k_parts = []
        q_parts = []
        for h in range(MLA_HEADS):
            nope = kn[:, h * QK_NOPE:(h + 1) * QK_NOPE]
            rs = lax.rsqrt((jnp.sum(nope * nope, axis=-1, keepdims=True) + pe_ss) * (1.0 / QK_HEAD) + NORM_EPS)
            k_parts += [(nope * rs * knn_ref[...]).astype(BF16), (k_rope * rs).astype(BF16)]
            r0 = h * HEAD_SLOT
            qnope = qt[r0:r0 + QK_NOPE, :]
            qrope = qt[r0 + QK_NOPE:r0 + QK_HEAD, :]
            qrot = qt[r0 + QK_HEAD:r0 + HEAD_SLOT, :]
            ssq = (jnp.sum(qnope * qnope, axis=0, keepdims=True)
                   + jnp.sum(qrope * qrope, axis=0, keepdims=True))
            rsq = lax.rsqrt(ssq * (1.0 / QK_HEAD) + NORM_EPS) * scale
            roped = (qrope * (qrw_ref[0:QK_ROPE, :] * cos_t)
                     + qrot * (qrw_ref[QK_ROPE:2 * QK_ROPE, :] * sin_t))
            q_parts += [(qnope * qnn_ref[...] * rsq).astype(BF16), (roped * rsq).astype(BF16),
                        jnp.zeros((HEAD_SLOT - QK_HEAD, nrows), BF16)]
        ones_row = (lax.broadcasted_iota(jnp.int32, (V_SLOT - V_HEAD, nrows), 0) == 0).astype(BF16)
        v_parts = []
        for h in range(MLA_HEADS):
            v_parts += [vt[h * V_HEAD:(h + 1) * V_HEAD, :], ones_row]
        return jnp.concatenate(k_parts, axis=1), jnp.concatenate(q_parts, axis=0), jnp.concatenate(v_parts, axis=0)

    latm = jnp.concatenate([jnp.zeros((META_ROWS, N_LAT), F32), latm_ref[...].astype(F32)], axis=0)
    km_s[...], qm_s[...], vm_s[...] = project(latm, cs_ref[0:ATT_META, :], cst_ref[:, 0:ATT_META])
    for c in range(SEQ // ATT_PROJ_ROWS):
        r0 = c * ATT_PROJ_ROWS
        k, qt, vt = project(latx_ref[0, r0:r0 + ATT_PROJ_ROWS, :].astype(F32),
                            cs_ref[ATT_META + r0:ATT_META + r0 + ATT_PROJ_ROWS, :],
                            cst_ref[:, ATT_META + r0:ATT_META + r0 + ATT_PROJ_ROWS])
        kx_s[r0:r0 + ATT_PROJ_ROWS, :] = k
        for t in range(ATT_PROJ_ROWS // ATT_TQ):
            qx_s[c * (ATT_PROJ_ROWS // ATT_TQ) + t] = qt[:, t * ATT_TQ:(t + 1) * ATT_TQ]
            vx_s[c * (ATT_PROJ_ROWS // ATT_TQ) + t] = vt[:, t * ATT_TQ:(t + 1) * ATT_TQ]

    def next_block(st, m):
        m_new = jnp.maximum(m, jnp.max(st, axis=0, keepdims=True))
        alpha = jnp.exp(m - m_new)
        p = jnp.exp((st - m_new).astype(BF16))
        return p, m_new, alpha

    def finish(acc_ext):
        o = acc_ext[0:V_HEAD, :] * (1.0 / acc_ext[V_HEAD:V_HEAD + 1, :])
        o = o * lax.rsqrt(jnp.mean(o * o, axis=0, keepdims=True) + NORM_EPS) * onw_ref[...]
        return o.T.astype(BF16)

    hs = lambda h: slice(h * HEAD_SLOT, (h + 1) * HEAD_SLOT)
    vs = lambda h: slice(h * V_SLOT, (h + 1) * V_SLOT)
    outs = lambda h: slice(h * V_HEAD, (h + 1) * V_HEAD)

    mkey = lax.broadcasted_iota(jnp.int32, (ATT_META, ATT_META), 0)
    mqry = lax.broadcasted_iota(jnp.int32, (ATT_META, ATT_META), 1)
    meta_mask = (mkey <= mqry) & ((mkey >= ATT_META_VALID) | (mkey == mqry))
    for h in range(MLA_HEADS):
        st = jnp.where(meta_mask, _dot(km_s[:, hs(h)], qm_s[hs(h), :]), NEG_INF)
        p, _, _ = next_block(st, jnp.full((1, ATT_META), NEG_INF, F32))
        o = finish(_dot(vm_s[vs(h), :], p))
        ym_ref[0, :, outs(h)] = o[META_ROWS:, :]

    meta_key_ok = lax.broadcasted_iota(jnp.int32, (ATT_META, ATT_TQ), 0) >= ATT_META_VALID
    diag_mask = (lax.broadcasted_iota(jnp.int32, (ATT_TQ, ATT_TQ), 0)
                 <= lax.broadcasted_iota(jnp.int32, (ATT_TQ, ATT_TQ), 1))

    def k_blk(j):
        k0 = j * ATT_TQ
        if not isinstance(j, int):
            k0 = pl.multiple_of(k0, ATT_TQ)
        return lambda h: kx_s[pl.ds(k0, ATT_TQ), hs(h)]

    v_blk = lambda j: (lambda h: vx_s[j, vs(h), :])

    acc_t, p_t, stm_t, st_t = acc_s, p_s, stm_s, (st0_s, st1_s)

    for i in range(SEQ // ATT_TQ):
        def scores(buf, k_of, nk):
            for h in range(MLA_HEADS):
                buf[h, 0:nk, :] = _dot(k_of(h), qx_s[i, hs(h), :])

        def absorb(buf, v_of, nk, mask, ms):
            ms2, alphas = [], []
            for h in range(MLA_HEADS):
                st = buf[h, 0:nk, :]
                if mask is not None:
                    st = jnp.where(mask, st, NEG_INF)
                p, m, alpha = next_block(st, ms[h])
                p_t[h, 0:nk, :] = p
                ms2.append(m)
                alphas.append(alpha)
            for h in range(MLA_HEADS):
                acc_t[h] = acc_t[h] * alphas[h] + _dot(v_of(h), p_t[h, 0:nk, :])
            return tuple(ms2)

        acc_t[...] = jnp.zeros_like(acc_t)
        scores(stm_t, lambda h: km_s[:, hs(h)], ATT_META)
        scores(st_t[0], k_blk(0), ATT_TQ)
        ms = (jnp.full((1, ATT_TQ), NEG_INF, F32),) * MLA_HEADS
        ms = absorb(stm_t, lambda h: vm_s[vs(h), :], ATT_META, meta_key_ok, ms)

        def pair(t, ms):
            scores(st_t[1], k_blk(2 * t + 1), ATT_TQ)
            ms = absorb(st_t[0], v_blk(2 * t), ATT_TQ, None, ms)
            scores(st_t[0], k_blk(2 * t + 2), ATT_TQ)
            return absorb(st_t[1], v_blk(2 * t + 1), ATT_TQ, None, ms)

        if i // 2 > 0:
            ms = lax.fori_loop(0, i // 2, pair, ms)
        if i % 2 == 1:
            scores(st_t[1], k_blk(i), ATT_TQ)
            ms = absorb(st_t[0], v_blk(i - 1), ATT_TQ, None, ms)
        absorb(st_t[i % 2], v_blk(i), ATT_TQ, diag_mask, ms)
        for h in range(MLA_HEADS):
            yx_ref[0, i * ATT_TQ:(i + 1) * ATT_TQ, outs(h)] = finish(acc_t[h])


def _mla(latx, latm, cs, cst, wqt, wk, wvt, qnn, qrw, knn, krw, onw):
    nb = latx.shape[0]
    const = lambda shape: pl.BlockSpec(shape, lambda b: (0,) * len(shape))
    nq = SEQ // ATT_TQ
    return pl.pallas_call(
        _mla_kernel,
        grid=(nb,),
        in_specs=[
            pl.BlockSpec((1, SEQ, N_LAT), lambda b: (b, 0, 0)),
            const((META_ROWS, N_LAT)),
            const((ATT_ROWS, 128)),
            const((128, ATT_ROWS)),
            const((MLA_HEADS * HEAD_SLOT, Q_LORA)),
            const((KV_LORA, MLA_HEADS * QK_NOPE)),
            const((MLA_HEADS * V_HEAD, KV_LORA)),
            const((QK_NOPE, 1)), const((2 * QK_ROPE, 1)), const((1, 128)), const((1, 128)),
            const((V_HEAD, 1)),
        ],
        out_specs=[
            pl.BlockSpec((1, SEQ, MLA_HEADS * V_HEAD), lambda b: (b, 0, 0)),
            pl.BlockSpec((1, META_ROWS, MLA_HEADS * V_HEAD), lambda b: (b, 0, 0)),
        ],
        out_shape=[
            jax.ShapeDtypeStruct((nb, SEQ, MLA_HEADS * V_HEAD), BF16),
            jax.ShapeDtypeStruct((nb, META_ROWS, MLA_HEADS * V_HEAD), BF16),
        ],
        scratch_shapes=[
            pltpu.VMEM((ATT_META, MLA_HEADS * HEAD_SLOT), BF16),
            pltpu.VMEM((MLA_HEADS * HEAD_SLOT, ATT_META), BF16),
            pltpu.VMEM((MLA_HEADS * V_SLOT, ATT_META), BF16),
            pltpu.VMEM((SEQ, MLA_HEADS * HEAD_SLOT), BF16),
            pltpu.VMEM((nq, MLA_HEADS * HEAD_SLOT, ATT_TQ), BF16),
            pltpu.VMEM((nq, MLA_HEADS * V_SLOT, ATT_TQ), BF16),
        ] + [
            pltpu.VMEM((MLA_HEADS, V_SLOT, ATT_TQ), F32),
            pltpu.VMEM((MLA_HEADS, ATT_META, ATT_TQ), F32),
            pltpu.VMEM((MLA_HEADS, ATT_TQ, ATT_TQ), F32),
            pltpu.VMEM((MLA_HEADS, ATT_TQ, ATT_TQ), F32),
            pltpu.VMEM((MLA_HEADS, ATT_TQ, ATT_TQ), BF16),
        ],
        compiler_params=_cparams(("arbitrary",)),
        name="mla",
    )(latx, latm, cs, cst, wqt, wk, wvt, qnn, qrw, knn, krw, onw)


DN_GROUP = 4
DN_GROUP_ROWS = DN_GROUP * DN_CHUNK


def _deltanet_kernel(n_w, dnx_ref, dnm_ref, abx_ref, abm_ref, alog_ref, dtb_ref, onw_ref, *refs):
    w_in_refs, (yx_ref, ym_ref), w_out_refs = refs[:n_w], refs[n_w:n_w + 2], refs[n_w + 2:2 * n_w + 2]
    (s_s, af_s, t_s, pa_s, rhs_s,
     uw0_s, qk0_s, qg0_s, kd0_s, el0_s, uw1_s, qk1_s, qg1_s, kd1_s, el1_s,
     smeta_s, ymeta_s) = refs[2 * n_w + 2:]
    for w_f32, w_bf16 in zip(w_in_refs, w_out_refs):
        w_bf16[...] = w_f32[...].astype(BF16)
    C = DN_CHUNK
    R = DN_STACK
    row = lax.broadcasted_iota(jnp.int32, (R, R), 0)
    col = lax.broadcasted_iota(jnp.int32, (R, R), 1)
    same = lambda sh: jnp.right_shift(row, sh) == jnp.right_shift(col, sh)
    m_incl = same(6) & (col <= row)
    m_strict = same(6) & (col < row)
    m_d2 = m_strict & same(1)
    m_merges = [m_strict & same(sh + 1) & jnp.logical_not(same(sh)) for sh in range(1, 6)]
    eye = (row == col).astype(F32)
    neg_a = -jnp.exp(alog_ref[...])

    s_s[...] = jnp.zeros_like(s_s)

    def stack(x):
        return jnp.concatenate([x[:, h * DN_DIM:(h + 1) * DN_DIM] for h in range(DN_HEADS)], axis=0)

    def stack_col(x, c0):
        return jnp.concatenate(
            [jnp.broadcast_to(x[:, c0 + h:c0 + h + 1], (C, DN_DIM)) for h in range(DN_HEADS)], axis=0)

    def gates(ab, row_ok):
        xa = ab + dtb_ref[...]
        softplus = jnp.maximum(xa, 0.0) + jnp.log(1.0 + jnp.exp(-jnp.abs(xa)))
        g = neg_a * softplus
        beta = jax.nn.sigmoid(ab)
        if row_ok is not None:
            g = jnp.where(row_ok, g, 0.0)
            beta = jnp.where(row_ok, beta, 0.0)
        pos = lax.broadcasted_iota(jnp.int32, g.shape, 0) & (C - 1)
        gc = g
        for s in (1, 2, 4, 8, 16, 32):
            gc = gc + jnp.where(pos >= s, pltpu.roll(gc, s, 0), 0.0)
        return gc, beta

    def phase_a(acts, gcl, betal, buf):
        uw_b, qk_b, qg_b, kd_b, el_b = buf
        n = len(acts)
        for c in range(n):
            act, gc, beta = acts[c], gcl[c], betal[c]
            q = stack(act[:, 0:DN_WIDTH])
            k = stack(act[:, DN_WIDTH:2 * DN_WIDTH])
            v = stack(act[:, 2 * DN_WIDTH:3 * DN_WIDTH])
            gcs = stack_col(gc, 0)
            gls = stack_col(jnp.broadcast_to(gc[C - 1:C, :], (C, 128)), 0)
            bs = stack_col(beta, DN_HEADS)
            grow = gcs.T[0:1, :]
            dec = jnp.exp(jnp.where(m_incl, gcs[:, 0:1] - grow, NEG_INF))
            kb = k * bs
            kbf = k.astype(BF16)
            eg = jnp.exp(gcs)
            af_s[c] = _dot_nt(kb.astype(BF16), kbf) * dec
            qk_b[c] = jnp.where(m_incl, _dot_nt(q.astype(BF16), kbf) * dec, 0.0).astype(BF16)
            rhs_s[c] = jnp.concatenate([v * bs, kb * eg], axis=1).astype(BF16)
            qg_b[c] = (q * eg).astype(BF16)
            kd_b[c] = (k * jnp.exp(gls - gcs)).astype(BF16)
            el_b[c] = jnp.exp(gls)
            yield

        for c in range(n):
            t_s[c] = eye - jnp.where(m_d2, af_s[c], 0.0)
        for m_low in m_merges:
            for c in range(n):
                pa_s[c] = _dot(jnp.where(m_low, af_s[c], 0.0).astype(BF16), t_s[c].astype(BF16)).astype(BF16)
            yield
            for c in range(n):
                t_s[c] = t_s[c] - _dot(t_s[c].astype(BF16), pa_s[c])
            yield
        for c in range(n):
            uw_b[c] = _dot(t_s[c].astype(BF16), rhs_s[c])

    def phase_b(zs, buf, outs):
        uw_b, qk_b, qg_b, kd_b, el_b = buf
        for c in range(len(zs)):
            vnew = []
            o_inter = []
            for h in range(DN_HEADS):
                r0 = h * C
                s_h = s_s[h]
                sb = s_h.astype(BF16)
                vn = uw_b[c, r0:r0 + C, 0:DN_DIM] - _dot(uw_b[c, r0:r0 + C, DN_DIM:].astype(BF16), sb)
                o_inter.append(_dot(qg_b[c, r0:r0 + C, :], sb))
                s_s[h] = s_h * el_b[c, r0:r0 + 1, :] + _dot_tn(kd_b[c, r0:r0 + C, :], vn.astype(BF16))
                vnew.append(vn)
            o = jnp.concatenate(o_inter, axis=0) + _dot(qk_b[c], jnp.concatenate(vnew, axis=0).astype(BF16))
            o = _rms(o, onw_ref[...])
            out = jnp.concatenate([o[h * C:(h + 1) * C, :] for h in range(DN_HEADS)], axis=1) * zs[c]
            outs.append(out.astype(BF16))
            yield

    def run(*gens_and_steps):
        gens = [g for g, _ in gens_and_steps]
        lens = [s for _, s in gens_and_steps]
        done = [0] * len(gens)
        alive = [True] * len(gens)
        while any(alive):
            i = min((j for j in range(len(gens)) if alive[j]), key=lambda j: (done[j] + 0.5) / lens[j])
            try:
                next(gens[i])
                done[i] += 1
            except StopIteration:
                alive[i] = False

    bufs = ((uw0_s, qk0_s, qg0_s, kd0_s, el0_s), (uw1_s, qk1_s, qg1_s, kd1_s, el1_s))

    @pl.when(pl.program_id(0) == 0)
    def _():
        meta_ok = lax.broadcasted_iota(jnp.int32, (C, 128), 0) >= META_PAD
        dn0 = dnm_ref[...]
        act0 = dn0[:, 0:3 * DN_WIDTH].astype(F32)
        gc0, beta0 = gates(abm_ref[...], meta_ok)
        run((phase_a([act0], [gc0], [beta0], bufs[1]), 1))
        out0 = []
        run((phase_b([dn0[:, 3 * DN_WIDTH:].astype(F32)], bufs[1], out0), 1))
        ymeta_s[...] = out0[0]
        smeta_s[...] = s_s[...]

    s_s[...] = smeta_s[...]
    ym_ref[0] = ymeta_s[...]

    rows = [slice(c * C, (c + 1) * C) for c in range(DN_GROUP)]
    a_steps = DN_GROUP + 12
    b_steps = DN_GROUP + 1

    def group_a(p, buf):
        r0 = p * DN_GROUP_ROWS
        if not isinstance(p, int):
            r0 = pl.multiple_of(r0, DN_GROUP_ROWS)
        act = dnx_ref[0, pl.ds(r0, DN_GROUP_ROWS), 0:3 * DN_WIDTH].astype(F32)
        gc, beta = gates(abx_ref[0, pl.ds(r0, DN_GROUP_ROWS), :], None)
        yield
        yield from phase_a([act[r, :] for r in rows], [gc[r, :] for r in rows], [beta[r, :] for r in rows], buf)

    def group_b(p, buf):
        r0 = p * DN_GROUP_ROWS
        if not isinstance(p, int):
            r0 = pl.multiple_of(r0, DN_GROUP_ROWS)
        z = dnx_ref[0, pl.ds(r0, DN_GROUP_ROWS), 3 * DN_WIDTH:].astype(F32)
        outs = []
        yield from phase_b([z[r, :] for r in rows], buf, outs)
        yx_ref[0, pl.ds(r0, DN_GROUP_ROWS), :] = jnp.concatenate(outs, axis=0)

    n_groups = SEQ // DN_GROUP_ROWS
    run((group_a(0, bufs[0]), a_steps))

    def body(kk, carry):
        p = 2 * kk
        run((group_a(p + 1, bufs[1]), a_steps), (group_b(p, bufs[0]), b_steps))
        run((group_a(p + 2, bufs[0]), a_steps), (group_b(p + 1, bufs[1]), b_steps))
        return carry

    lax.fori_loop(0, n_groups // 2 - 1, body, 0)
    run((group_a(n_groups - 1, bufs[1]), a_steps), (group_b(n_groups - 2, bufs[0]), b_steps))
    run((group_b(n_groups - 1, bufs[1]), b_steps))


def _deltanet(dnx, dnm, abx, abm, alog, dtb, onw, ffn_weights):
    nb = dnx.shape[0]
    const = lambda shape: pl.BlockSpec(shape, lambda b: (0,) * len(shape))
    row_slice = lambda w: pl.BlockSpec((w.shape[0] // nb, w.shape[1]), lambda b: (b, 0))
    return pl.pallas_call(
        functools.partial(_deltanet_kernel, len(ffn_weights)),
        grid=(nb,),
        in_specs=[
            pl.BlockSpec((1, SEQ, N_DN), lambda b: (b, 0, 0)),
            const((META_ROWS, N_DN)),
            pl.BlockSpec((1, SEQ, N_AB), lambda b: (b, 0, 0)),
            const((META_ROWS, N_AB)),
            const((1, 128)), const((1, 128)), const((1, 128)),
        ] + [row_slice(w) for w in ffn_weights],
        out_specs=[
            pl.BlockSpec((1, SEQ, DN_WIDTH), lambda b: (b, 0, 0)),
            pl.BlockSpec((1, META_ROWS, DN_WIDTH), lambda b: (b, 0, 0)),
        ] + [row_slice(w) for w in ffn_weights],
        out_shape=[
            jax.ShapeDtypeStruct((nb, SEQ, DN_WIDTH), BF16),
            jax.ShapeDtypeStruct((nb, META_ROWS, DN_WIDTH), BF16),
        ] + [jax.ShapeDtypeStruct(w.shape, BF16) for w in ffn_weights],
        scratch_shapes=[
            pltpu.VMEM((DN_HEADS, DN_DIM, DN_DIM), F32),
            pltpu.VMEM((DN_GROUP, DN_STACK, DN_STACK), F32),
            pltpu.VMEM((DN_GROUP, DN_STACK, DN_STACK), F32),
            pltpu.VMEM((DN_GROUP, DN_STACK, DN_STACK), BF16),
            pltpu.VMEM((DN_GROUP, DN_STACK, 2 * DN_DIM), BF16),
        ] + 2 * [
            pltpu.VMEM((DN_GROUP, DN_STACK, 2 * DN_DIM), F32),
            pltpu.VMEM((DN_GROUP, DN_STACK, DN_STACK), BF16),
            pltpu.VMEM((DN_GROUP, DN_STACK, DN_DIM), BF16),
            pltpu.VMEM((DN_GROUP, DN_STACK, DN_DIM), BF16),
            pltpu.VMEM((DN_GROUP, DN_STACK, DN_DIM), F32),
        ] + [
            pltpu.VMEM((DN_HEADS, DN_DIM, DN_DIM), F32),
            pltpu.VMEM((META_ROWS, DN_WIDTH), BF16),
        ],
        compiler_params=_cparams(("arbitrary",)),
        name="deltanet",
    )(dnx, dnm, abx, abm, alog, dtb, onw, *ffn_weights)


FFN_ROWS = 1024
FFN_HALO = 16
N_FF_BLK = D_FF // FF_BLK


def _ffn_kernel(x_ref, xh_ref, mh_ref, ya_ref, yah_ref, yam_ref, yd_ref, ydh_ref, ydm_ref,
                wo_ref, nw_ref, wg_ref, wu_ref, cw_ref, cb_ref, wd_ref,
                o_ref, u_s, g0_s, g1_s, up0_s, up1_s, act_s):
    r = pl.program_id(1)
    mixed = jnp.concatenate([ya_ref[0], yd_ref[0]], axis=1)
    h_mid = x_ref[0] + _dot(mixed, wo_ref[...])
    o_ref[0] = h_mid
    u_s[FFN_HALO:, :] = _rms(h_mid, nw_ref[...]).astype(BF16)
    first = r == 0
    mixed_h = jnp.concatenate([jnp.where(first, yam_ref[0], yah_ref[0]),
                               jnp.where(first, ydm_ref[0], ydh_ref[0])], axis=1)
    h_halo = jnp.where(first, mh_ref[...], xh_ref[0]) + _dot(mixed_h, wo_ref[...])
    u_s[0:FFN_HALO, :] = _rms(h_halo, nw_ref[...]).astype(BF16)

    g_bufs = (g0_s, g1_s)
    up_bufs = (up0_s, up1_s)

    def project(f):
        cols = slice(f * FF_BLK, (f + 1) * FF_BLK)
        g_bufs[f % 2][...] = _dot(u_s[...], wg_ref[:, cols])
        up_bufs[f % 2][...] = _dot(u_s[FFN_HALO:, :], wu_ref[:, cols])

    project(0)
    for f in range(N_FF_BLK):
        if f + 1 < N_FF_BLK:
            project(f + 1)
        g_s = g_bufs[f % 2]
        cols = slice(f * FF_BLK, (f + 1) * FF_BLK)
        gate = (cw_ref[2:3, cols] * g_s[FFN_HALO:, :]
                + cw_ref[1:2, cols] * g_s[FFN_HALO - 1:FFN_HALO - 1 + FFN_ROWS, :]
                + cw_ref[0:1, cols] * g_s[FFN_HALO - 2:FFN_HALO - 2 + FFN_ROWS, :]
                + cb_ref[:, cols])
        act_s[:, cols] = (_silu(gate) * up_bufs[f % 2][...]).astype(BF16)
    o_ref[0] += _dot(act_s[...], wd_ref[...])


def _ffn(x, hp_meta, yax, yam, ydx, ydm, wo, nw, wg, wu, cw, cb, wd):
    nb = x.shape[0]
    nr = SEQ // FFN_ROWS
    hb = FFN_ROWS // FFN_HALO
    halo_idx = lambda b, r: (b, jnp.maximum(r * hb - 1, 0), 0)
    meta_idx = lambda b, r: (b, META_ROWS // FFN_HALO - 1, 0)
    main_idx = lambda b, r: (b, r, 0)
    resident = lambda shape: pl.BlockSpec(shape, lambda b, r: (0,) * len(shape),
                                          pipeline_mode=pl.Buffered(1))
    return pl.pallas_call(
        _ffn_kernel,
        grid=(nb, nr),
        in_specs=[
            pl.BlockSpec((1, FFN_ROWS, D_MODEL), main_idx),
            pl.BlockSpec((1, FFN_HALO, D_MODEL), halo_idx),
            pl.BlockSpec((FFN_HALO, D_MODEL), lambda b, r: (META_ROWS // FFN_HALO - 1, 0)),
            pl.BlockSpec((1, FFN_ROWS, MLA_HEADS * V_HEAD), main_idx),
            pl.BlockSpec((1, FFN_HALO, MLA_HEADS * V_HEAD), halo_idx),
            pl.BlockSpec((1, FFN_HALO, MLA_HEADS * V_HEAD), meta_idx),
            pl.BlockSpec((1, FFN_ROWS, DN_WIDTH), main_idx),
            pl.BlockSpec((1, FFN_HALO, DN_WIDTH), halo_idx),
            pl.BlockSpec((1, FFN_HALO, DN_WIDTH), meta_idx),
            resident((D_MODEL, D_MODEL)),
            resident((1, D_MODEL)),
            resident((D_MODEL, D_FF)),
            resident((D_MODEL, D_FF)),
            resident((3, D_FF)),
            resident((1, D_FF)),
            resident((D_FF, D_MODEL)),
        ],
        out_specs=pl.BlockSpec((1, FFN_ROWS, D_MODEL), main_idx),
        out_shape=jax.ShapeDtypeStruct((nb, SEQ, D_MODEL), F32),
        scratch_shapes=[
            pltpu.VMEM((FFN_HALO + FFN_ROWS, D_MODEL), BF16),
            pltpu.VMEM((FFN_HALO + FFN_ROWS, FF_BLK), F32),
            pltpu.VMEM((FFN_HALO + FFN_ROWS, FF_BLK), F32),
            pltpu.VMEM((FFN_ROWS, FF_BLK), F32),
            pltpu.VMEM((FFN_ROWS, FF_BLK), F32),
            pltpu.VMEM((FFN_ROWS, D_FF), BF16),
        ],
        compiler_params=_cparams(("arbitrary", "arbitrary")),
        name="outproj_ffn",
    )(x, x, hp_meta, yax, yax, yam, ydx, ydx, ydm, wo, nw, wg, wu, cw, cb, wd)


def _rot_cols(w):
    half = QK_ROPE // 2
    return jnp.concatenate([-w[..., half:], w[..., :half]], axis=-1)


def _swap_halves(w):
    half = QK_ROPE // 2
    return jnp.concatenate([w[..., half:], w[..., :half]], axis=-1)


def _pad_lanes(v, n=128):
    return jnp.pad(v.astype(F32), (0, n - v.shape[0])).reshape(1, n)


def _layer(x, hp_meta, l, attn_norm_w, w_in, q_a_norm_w, w_q_b, kv_a_norm_w, w_kv_b, q_norm_w,
           k_norm_w, mla_out_norm_w, dn_conv_w, dn_A_log, dn_dt_bias, dn_out_norm_w, w_out,
           ffn_norm_w, w_gate, w_up, ffn_conv_w, ffn_conv_b, w_down):
    nb = x.shape[0]
    c1 = Q_LORA
    c2 = c1 + KV_LORA
    c3 = c2 + QK_ROPE
    c4 = c3 + 3 * DN_WIDTH
    c5 = c4 + DN_WIDTH
    wint = w_in[l].T.astype(BF16)
    k_pe_w = wint[c2:c3, :]
    half = QK_ROPE // 2
    w1 = jnp.concatenate(
        [wint[:c2, :], k_pe_w, -k_pe_w[half:, :], k_pe_w[:half, :], wint[c3:c5, :], wint[c5:, :],
         jnp.zeros((N_AB - 2 * DN_HEADS, D_MODEL), BF16)], axis=0).T

    wqb = w_q_b[l].reshape(Q_LORA, MLA_HEADS, QK_HEAD)
    wqt = jnp.concatenate([wqb[..., :QK_NOPE], wqb[..., QK_NOPE:], _rot_cols(wqb[..., QK_NOPE:])],
                          axis=-1).reshape(Q_LORA, MLA_HEADS * HEAD_SLOT).T.astype(BF16)
    wkvb = w_kv_b[l].reshape(KV_LORA, MLA_HEADS, QK_NOPE + V_HEAD)
    wk = wkvb[..., :QK_NOPE].reshape(KV_LORA, MLA_HEADS * QK_NOPE).astype(BF16)
    wvt = wkvb[..., QK_NOPE:].reshape(KV_LORA, MLA_HEADS * V_HEAD).T.astype(BF16)
    qn = q_norm_w[l].astype(F32)
    kn = k_norm_w[l].astype(F32)
    qnn = qn[:QK_NOPE].reshape(QK_NOPE, 1)
    knn = kn[:QK_NOPE].reshape(1, 128)
    qrw = jnp.concatenate([qn[QK_NOPE:], _swap_halves(qn[QK_NOPE:])]).reshape(2 * QK_ROPE, 1)
    krw = jnp.concatenate([kn[QK_NOPE:], _swap_halves(kn[QK_NOPE:])]).reshape(1, 128)

    half = QK_ROPE // 2
    inv_freq = ROPE_THETA ** (-jnp.arange(half, dtype=F32) / half)
    pos = (jnp.arange(ATT_ROWS, dtype=jnp.int32) - ATT_META_VALID).astype(F32)
    ang = pos[:, None] * inv_freq[None, :]
    cs = jnp.concatenate([jnp.cos(ang), jnp.cos(ang), jnp.sin(ang), jnp.sin(ang)], axis=1)
    cst = cs.T

    nw1 = attn_norm_w[l].astype(F32).reshape(1, D_MODEL)
    qaw = q_a_norm_w[l].astype(F32).reshape(1, Q_LORA)
    kvaw = kv_a_norm_w[l].astype(F32).reshape(1, KV_LORA)
    cw = dn_conv_w[l].astype(F32)
    no_hist = jnp.zeros((CONV_HIST, 3 * DN_WIDTH), F32)
    latm, dnm, abm, meta_tail = _inproj(hp_meta, nw1, w1, qaw, kvaw, cw, no_hist, META_ROWS, 1)
    latx, dnx, abx, _ = _inproj(x.reshape(nb * SEQ, D_MODEL), nw1, w1, qaw, kvaw, cw, meta_tail,
                                INPROJ_ROWS, SEQ // INPROJ_ROWS)
    latx = latx.reshape(nb, SEQ, N_LAT)
    dnx = dnx.reshape(nb, SEQ, N_DN)
    abx = abx.reshape(nb, SEQ, N_AB)

    yax, yam = _mla(latx, latm, cs, cst, wqt, wk, wvt,
                    qnn, qrw, knn, krw, mla_out_norm_w[l].astype(F32).reshape(V_HEAD, 1))
    ydx, ydm, wo, wg, wu, wd = _deltanet(
        dnx, dnm, abx, abm, _pad_lanes(dn_A_log[l]), _pad_lanes(dn_dt_bias[l]),
        dn_out_norm_w[l].astype(F32).reshape(1, DN_DIM),
        [w_out[l].astype(F32), w_gate[l].astype(F32), w_up[l].astype(F32), w_down[l].astype(F32)])
    return _ffn(x, hp_meta, yax, yam, ydx, ydm, wo,
                ffn_norm_w[l].astype(F32).reshape(1, D_MODEL), wg, wu, ffn_conv_w[l].astype(F32),
                ffn_conv_b[l].astype(F32).reshape(1, D_FF), wd)


def kernel(x, meta_tokens, attn_norm_w, w_in, q_a_norm_w, w_q_b, kv_a_norm_w, w_kv_b, q_norm_w, k_norm_w, mla_out_norm_w, dn_conv_w, dn_A_log, dn_dt_bias, dn_out_norm_w, w_out, ffn_norm_w, w_gate, w_up, ffn_conv_w, ffn_conv_b, w_down):
    assert x.shape[1:] == (SEQ, D_MODEL) and w_in.shape[0] == 1
    hp_meta = jnp.concatenate([jnp.zeros((META_PAD, D_MODEL), x.dtype), meta_tokens.astype(x.dtype)], axis=0)
    return _layer(x, hp_meta, 0, attn_norm_w, w_in, q_a_norm_w, w_q_b, kv_a_norm_w, w_kv_b, q_norm_w,
                  k_norm_w, mla_out_norm_w, dn_conv_w, dn_A_log, dn_dt_bias, dn_out_norm_w, w_out,
                  ffn_norm_w, w_gate, w_up, ffn_conv_w, ffn_conv_b, w_down)
```

```python
import functools
import math

import jax
import jax.numpy as jnp
from jax import lax
from jax.experimental import pallas as pl
from jax.experimental.pallas import tpu as pltpu

F32 = jnp.float32
BF16 = jnp.bfloat16

D_MODEL = 1024
SEQ = 2048
N_META = 16
META_ROWS = 64
META_PAD = META_ROWS - N_META

MLA_HEADS = 4
QK_NOPE = 128
QK_ROPE = 64
QK_HEAD = QK_NOPE + QK_ROPE
V_HEAD = 128
Q_LORA = 256
KV_LORA = 256
ROPE_THETA = 10000.0
HEAD_SLOT = 256

DN_HEADS = 4
DN_DIM = 128
DN_WIDTH = DN_HEADS * DN_DIM
DN_CHUNK = 64
DN_STACK = DN_HEADS * DN_CHUNK
DN_CONV = 4

D_FF = 2816
FF_BLK = 256
NORM_EPS = 1e-6

N_LAT = Q_LORA + KV_LORA + 2 * QK_ROPE
C_DN = N_LAT
N_DN = 4 * DN_WIDTH
C_AB = C_DN + N_DN
N_AB = 128
N_PROJ = C_AB + N_AB

VMEM_LIMIT = 56 * 1024 * 1024


def _cparams(sem):
    return pltpu.CompilerParams(dimension_semantics=sem, vmem_limit_bytes=VMEM_LIMIT)


def _rms(x, w):
    return x * lax.rsqrt(jnp.mean(x * x, axis=-1, keepdims=True) + NORM_EPS) * w


def _dot(a, b):
    return jnp.dot(a, b, preferred_element_type=F32)


def _dot_nt(a, b):
    return lax.dot_general(a, b, (((1,), (1,)), ((), ())), preferred_element_type=F32)


def _dot_tn(a, b):
    return lax.dot_general(a, b, (((0,), (0,)), ((), ())), preferred_element_type=F32)


def _silu(x):
    return x * jax.nn.sigmoid(x)


assert DN_CONV == 4
CONV_HIST = 16
INPROJ_ROWS = 1024
INPROJ_SUB = 256


def _inproj_kernel(tiles_per_seq, x_ref, nw_ref, w_ref, qaw_ref, kvaw_ref, cw_ref, hist_in_ref,
                   lat_ref, dn_ref, ab_ref, tail_ref, hist_s, p0_s, p1_s):
    n = x_ref.shape[0]
    sub = min(INPROJ_SUB, n)
    p_bufs = (p0_s, p1_s)

    @pl.when(pl.program_id(0) % tiles_per_seq == 0)
    def _():
        hist_s[...] = hist_in_ref[...]

    def project(r):
        u = _rms(x_ref[r * sub:(r + 1) * sub, :], nw_ref[...]).astype(BF16)
        p_bufs[r % 2][0:sub, :] = _dot(u, w_ref[...])

    def post(r):
        p = p_bufs[r % 2]
        rows = slice(r * sub, (r + 1) * sub)
        lat_ref[rows, 0:Q_LORA] = _rms(p[0:sub, 0:Q_LORA], qaw_ref[...]).astype(BF16)
        lat_ref[rows, Q_LORA:Q_LORA + KV_LORA] = _rms(p[0:sub, Q_LORA:Q_LORA + KV_LORA],
                                                      kvaw_ref[...]).astype(BF16)
        lat_ref[rows, Q_LORA + KV_LORA:N_LAT] = p[0:sub, Q_LORA + KV_LORA:N_LAT].astype(BF16)
        ab_ref[rows, :] = p[0:sub, C_AB:C_AB + N_AB]
        pre = p[0:sub, C_DN:C_DN + 3 * DN_WIDTH]
        full = jnp.concatenate([hist_s[...], pre], axis=0)
        hist_s[...] = pre[sub - CONV_HIST:, :]
        full1 = pltpu.roll(full, 1, 0)
        near = cw_ref[3:4, :] * full + cw_ref[2:3, :] * full1
        far = cw_ref[1:2, :] * full + cw_ref[0:1, :] * full1
        act = _silu((near + pltpu.roll(far, 2, 0))[CONV_HIST:, :])
        for h in range(2 * DN_HEADS):
            a = act[:, h * DN_DIM:(h + 1) * DN_DIM]
            a = a * lax.rsqrt(jnp.sum(a * a, axis=-1, keepdims=True) + NORM_EPS)
            if h < DN_HEADS:
                a = a * (1.0 / math.sqrt(DN_DIM))
            dn_ref[rows, h * DN_DIM:(h + 1) * DN_DIM] = a.astype(BF16)
        dn_ref[rows, 2 * DN_WIDTH:3 * DN_WIDTH] = act[:, 2 * DN_WIDTH:].astype(BF16)
        dn_ref[rows, 3 * DN_WIDTH:] = _silu(p[0:sub, C_DN + 3 * DN_WIDTH:C_DN + N_DN]).astype(BF16)

    project(0)
    for r in range(n // sub):
        if r + 1 < n // sub:
            project(r + 1)
        post(r)
    tail_ref[...] = hist_s[...]


def _inproj(x2d, nw, w, qaw, kvaw, cw, hist_in, row_tile, tiles_per_seq):
    rows = x2d.shape[0]
    grid = (rows // row_tile,)
    const = lambda shape: pl.BlockSpec(shape, lambda i: (0,) * len(shape))
    return pl.pallas_call(
        functools.partial(_inproj_kernel, tiles_per_seq),
        grid=grid,
        in_specs=[
            pl.BlockSpec((row_tile, D_MODEL), lambda i: (i, 0)),
            const((1, D_MODEL)),
            const((D_MODEL, N_PROJ)),
            const((1, Q_LORA)),
            const((1, KV_LORA)),
            const((DN_CONV, 3 * DN_WIDTH)),
            const((CONV_HIST, 3 * DN_WIDTH)),
        ],
        out_specs=[
            pl.BlockSpec((row_tile, N_LAT), lambda i: (i, 0)),
            pl.BlockSpec((row_tile, N_DN), lambda i: (i, 0)),
            pl.BlockSpec((row_tile, N_AB), lambda i: (i, 0)),
            pl.BlockSpec((CONV_HIST, 3 * DN_WIDTH), lambda i: (i, 0)),
        ],
        out_shape=[
            jax.ShapeDtypeStruct((rows, N_LAT), BF16),
            jax.ShapeDtypeStruct((rows, N_DN), BF16),
            jax.ShapeDtypeStruct((rows, N_AB), F32),
            jax.ShapeDtypeStruct((grid[0] * CONV_HIST, 3 * DN_WIDTH), F32),
        ],
        scratch_shapes=[pltpu.VMEM((CONV_HIST, 3 * DN_WIDTH), F32),
                        pltpu.VMEM((min(INPROJ_SUB, row_tile), N_PROJ), F32),
                        pltpu.VMEM((min(INPROJ_SUB, row_tile), N_PROJ), F32)],
        compiler_params=_cparams(("arbitrary",)),
        name="inproj",
    )(x2d, nw, w, qaw, kvaw, cw, hist_in)


ATT_TQ = 256
V_SLOT = V_HEAD + 16
ATT_PROJ_ROWS = 512
ATT_META = 2 * META_ROWS
ATT_META_VALID = ATT_META - N_META
ATT_ROWS = ATT_META + SEQ
NEG_INF = float("-inf")


def _mla_kernel(n_w, latx_ref, latm_ref, cs_ref, cst_ref, wqt_ref, wk_ref, wvt_ref,
                qnn_ref, qrw_ref, knn_ref, krw_ref, onw_ref, *refs):
    w_f32_refs, (yx_ref, ym_ref), w_bf16_refs = refs[:n_w], refs[n_w:n_w + 2], refs[n_w + 2:2 * n_w + 2]
    km_s, qm_s, vm_s, kx_s, qx_s, vx_s, acc_s, stm_s, st0_s, st1_s, p_s = refs[2 * n_w + 2:]
    for w_f32, w_bf16 in zip(w_f32_refs, w_bf16_refs):
        w_bf16[...] = w_f32[...].astype(BF16)
    low = lax.broadcasted_iota(jnp.int32, (1, 128), 1) < QK_ROPE
    scale = 1.0 / math.sqrt(QK_HEAD)

    def project(lat, cs, cst):
        nrows = lat.shape[0]
        qn = lat[:, 0:Q_LORA]
        kvn = lat[:, Q_LORA:Q_LORA + KV_LORA]
        pe = lat[:, Q_LORA + KV_LORA:N_LAT]
        qt = _dot(wqt_ref[...], qn.T.astype(BF16))
        vt = _dot(wvt_ref[...], kvn.T.astype(BF16)).astype(BF16)
        kn = _dot(kvn.astype(BF16), wk_ref[...])
        a = pe * (cs * krw_ref[...])
        k_rope = jnp.where(low, a + pltpu.roll(a, QK_ROPE, 1), 0.0)
        pe_ss = jnp.sum(jnp.where(low, pe * pe, 0.0), axis=-1, keepdims=True)
        cos_t = cst[0:QK_ROPE, :]
        sin_t = cst[QK_ROPE:2 * QK_ROPE, :]
        k_parts = []
        q_parts = []
        for h in range(MLA_HEADS):
            nope = kn[:, h * QK_NOPE:(h + 1) * QK_NOPE]
            rs = lax.rsqrt((jnp.sum(nope * nope, axis=-1, keepdims=True) + pe_ss) * (1.0 / QK_HEAD) + NORM_EPS)
            k_parts += [(nope * rs * knn_ref[...]).astype(BF16), (k_rope * rs).astype(BF16)]
            r0 = h * HEAD_SLOT
            qnope = qt[r0:r0 + QK_NOPE, :]
            qrope = qt[r0 + QK_NOPE:r0 + QK_HEAD, :]
            qrot = qt[r0 + QK_HEAD:r0 + HEAD_SLOT, :]
            ssq = (jnp.sum(qnope * qnope, axis=0, keepdims=True)
                   + jnp.sum(qrope * qrope, axis=0, keepdims=True))
            rsq = lax.rsqrt(ssq * (1.0 / QK_HEAD) + NORM_EPS) * scale
            roped = (qrope * (qrw_ref[0:QK_ROPE, :] * cos_t)
                     + qrot * (qrw_ref[QK_ROPE:2 * QK_ROPE, :] * sin_t))
            q_parts += [(qnope * qnn_ref[...] * rsq).astype(BF16), (roped * rsq).astype(BF16),
                        jnp.zeros((HEAD_SLOT - QK_HEAD, nrows), BF16)]
        ones_row = (lax.broadcasted_iota(jnp.int32, (V_SLOT - V_HEAD, nrows), 0) == 0).astype(BF16)
        v_parts = []
        for h in range(MLA_HEADS):
            v_parts += [vt[h * V_HEAD:(h + 1) * V_HEAD, :], ones_row]
        return jnp.concatenate(k_parts, axis=1), jnp.concatenate(q_parts, axis=0), jnp.concatenate(v_parts, axis=0)

    latm = jnp.concatenate([jnp.zeros((META_ROWS, N_LAT), F32), latm_ref[...].astype(F32)], axis=0)
    km_s[...], qm_s[...], vm_s[...] = project(latm, cs_ref[0:ATT_META, :], cst_ref[:, 0:ATT_META])
    for c in range(SEQ // ATT_PROJ_ROWS):
        r0 = c * ATT_PROJ_ROWS
        k, qt, vt = project(latx_ref[0, r0:r0 + ATT_PROJ_ROWS, :].astype(F32),
                            cs_ref[ATT_META + r0:ATT_META + r0 + ATT_PROJ_ROWS, :],
                            cst_ref[:, ATT_META + r0:ATT_META + r0 + ATT_PROJ_ROWS])
        kx_s[r0:r0 + ATT_PROJ_ROWS, :] = k
        for t in range(ATT_PROJ_ROWS // ATT_TQ):
            qx_s[c * (ATT_PROJ_ROWS // ATT_TQ) + t] = qt[:, t * ATT_TQ:(t + 1) * ATT_TQ]
            vx_s[c * (ATT_PROJ_ROWS // ATT_TQ) + t] = vt[:, t * ATT_TQ:(t + 1) * ATT_TQ]

    def next_block(st, m):
        m_new = jnp.maximum(m, jnp.max(st, axis=0, keepdims=True))
        alpha = jnp.exp(m - m_new)
        p = jnp.exp((st - m_new).astype(BF16))
        return p, m_new, alpha

    def finish(acc_ext):
        o = acc_ext[0:V_HEAD, :] * (1.0 / acc_ext[V_HEAD:V_HEAD + 1, :])
        o = o * lax.rsqrt(jnp.mean(o * o, axis=0, keepdims=True) + NORM_EPS) * onw_ref[...]
        return o.T.astype(BF16)

    hs = lambda h: slice(h * HEAD_SLOT, (h + 1) * HEAD_SLOT)
    vs = lambda h: slice(h * V_SLOT, (h + 1) * V_SLOT)
    outs = lambda h: slice(h * V_HEAD, (h + 1) * V_HEAD)

    mkey = lax.broadcasted_iota(jnp.int32, (ATT_META, ATT_META), 0)
    mqry = lax.broadcasted_iota(jnp.int32, (ATT_META, ATT_META), 1)
    meta_mask = (mkey <= mqry) & ((mkey >= ATT_META_VALID) | (mkey == mqry))
    for h in range(MLA_HEADS):
        st = jnp.where(meta_mask, _dot(km_s[:, hs(h)], qm_s[hs(h), :]), NEG_INF)
        p, _, _ = next_block(st, jnp.full((1, ATT_META), NEG_INF, F32))
        o = finish(_dot(vm_s[vs(h), :], p))
        ym_ref[0, :, outs(h)] = o[META_ROWS:, :]

    meta_key_ok = lax.broadcasted_iota(jnp.int32, (ATT_META, ATT_TQ), 0) >= ATT_META_VALID
    diag_mask = (lax.broadcasted_iota(jnp.int32, (ATT_TQ, ATT_TQ), 0)
                 <= lax.broadcasted_iota(jnp.int32, (ATT_TQ, ATT_TQ), 1))

    def k_blk(j):
        k0 = j * ATT_TQ
        if not isinstance(j, int):
            k0 = pl.multiple_of(k0, ATT_TQ)
        return lambda h: kx_s[pl.ds(k0, ATT_TQ), hs(h)]

    v_blk = lambda j: (lambda h: vx_s[j, vs(h), :])

    acc_t, p_t, stm_t, st_t = acc_s, p_s, stm_s, (st0_s, st1_s)

    for i in range(SEQ // ATT_TQ):
        def scores(buf, k_of, nk):
            for h in range(MLA_HEADS):
                buf[h, 0:nk, :] = _dot(k_of(h), qx_s[i, hs(h), :])

        def absorb(buf, v_of, nk, mask, ms):
            ms2, alphas = [], []
            for h in range(MLA_HEADS):
                st = buf[h, 0:nk, :]
                if mask is not None:
                    st = jnp.where(mask, st, NEG_INF)
                p, m, alpha = next_block(st, ms[h])
                p_t[h, 0:nk, :] = p
                ms2.append(m)
                alphas.append(alpha)
            for h in range(MLA_HEADS):
                acc_t[h] = acc_t[h] * alphas[h] + _dot(v_of(h), p_t[h, 0:nk, :])
            return tuple(ms2)

        acc_t[...] = jnp.zeros_like(acc_t)
        scores(stm_t, lambda h: km_s[:, hs(h)], ATT_META)
        scores(st_t[0], k_blk(0), ATT_TQ)
        ms = (jnp.full((1, ATT_TQ), NEG_INF, F32),) * MLA_HEADS
        ms = absorb(stm_t, lambda h: vm_s[vs(h), :], ATT_META, meta_key_ok, ms)

        def pair(t, ms):
            scores(st_t[1], k_blk(2 * t + 1), ATT_TQ)
            ms = absorb(st_t[0], v_blk(2 * t), ATT_TQ, None, ms)
            scores(st_t[0], k_blk(2 * t + 2), ATT_TQ)
            return absorb(st_t[1], v_blk(2 * t + 1), ATT_TQ, None, ms)

        if i // 2 > 0:
            ms = lax.fori_loop(0, i // 2, pair, ms)
        if i % 2 == 1:
            scores(st_t[1], k_blk(i), ATT_TQ)
            ms = absorb(st_t[0], v_blk(i - 1), ATT_TQ, None, ms)
        absorb(st_t[i % 2], v_blk(i), ATT_TQ, diag_mask, ms)
        for h in range(MLA_HEADS):
            yx_ref[0, i * ATT_TQ:(i + 1) * ATT_TQ, outs(h)] = finish(acc_t[h])


def _mla(latx, latm, cs, cst, wqt, wk, wvt, qnn, qrw, knn, krw, onw, cast_weights):
    nb = latx.shape[0]
    const = lambda shape: pl.BlockSpec(shape, lambda b: (0,) * len(shape))
    row_slice = lambda w: pl.BlockSpec((w.shape[0] // nb, w.shape[1]), lambda b: (b, 0))
    nq = SEQ // ATT_TQ
    return pl.pallas_call(
        functools.partial(_mla_kernel, len(cast_weights)),
        grid=(nb,),
        in_specs=[
            pl.BlockSpec((1, SEQ, N_LAT), lambda b: (b, 0, 0)),
            const((META_ROWS, N_LAT)),
            const((ATT_ROWS, 128)),
            const((128, ATT_ROWS)),
            const((MLA_HEADS * HEAD_SLOT, Q_LORA)),
            const((KV_LORA, MLA_HEADS * QK_NOPE)),
            const((MLA_HEADS * V_HEAD, KV_LORA)),
            const((QK_NOPE, 1)), const((2 * QK_ROPE, 1)), const((1, 128)), const((1, 128)),
            const((V_HEAD, 1)),
        ] + [row_slice(w) for w in cast_weights],
        out_specs=[
            pl.BlockSpec((1, SEQ, MLA_HEADS * V_HEAD), lambda b: (b, 0, 0)),
            pl.BlockSpec((1, META_ROWS, MLA_HEADS * V_HEAD), lambda b: (b, 0, 0)),
        ] + [row_slice(w) for w in cast_weights],
        out_shape=[
            jax.ShapeDtypeStruct((nb, SEQ, MLA_HEADS * V_HEAD), BF16),
            jax.ShapeDtypeStruct((nb, META_ROWS, MLA_HEADS * V_HEAD), BF16),
        ] + [jax.ShapeDtypeStruct(w.shape, BF16) for w in cast_weights],
        scratch_shapes=[
            pltpu.VMEM((ATT_META, MLA_HEADS * HEAD_SLOT), BF16),
            pltpu.VMEM((MLA_HEADS * HEAD_SLOT, ATT_META), BF16),
            pltpu.VMEM((MLA_HEADS * V_SLOT, ATT_META), BF16),
            pltpu.VMEM((SEQ, MLA_HEADS * HEAD_SLOT), BF16),
            pltpu.VMEM((nq, MLA_HEADS * HEAD_SLOT, ATT_TQ), BF16),
            pltpu.VMEM((nq, MLA_HEADS * V_SLOT, ATT_TQ), BF16),
        ] + [
            pltpu.VMEM((MLA_HEADS, V_SLOT, ATT_TQ), F32),
            pltpu.VMEM((MLA_HEADS, ATT_META, ATT_TQ), F32),
            pltpu.VMEM((MLA_HEADS, ATT_TQ, ATT_TQ), F32),
            pltpu.VMEM((MLA_HEADS, ATT_TQ, ATT_TQ), F32),
            pltpu.VMEM((MLA_HEADS, ATT_TQ, ATT_TQ), BF16),
        ],
        compiler_params=_cparams(("arbitrary",)),
        name="mla",
    )(latx, latm, cs, cst, wqt, wk, wvt, qnn, qrw, knn, krw, onw, *cast_weights)


DN_GROUP = 4
DN_GROUP_ROWS = DN_GROUP * DN_CHUNK


def _deltanet_kernel(dnx_ref, dnm_ref, abx_ref, abm_ref, alog_ref, dtb_ref, onw_ref,
                     yx_ref, ym_ref, s_s, af_s, t_s, pa_s, rhs_s,
                     uw0_s, qk0_s, qg0_s, kd0_s, el0_s, uw1_s, qk1_s, qg1_s, kd1_s, el1_s,
                     smeta_s, ymeta_s):
    C = DN_CHUNK
    R = DN_STACK
    row = lax.broadcasted_iota(jnp.int32, (R, R), 0)
    col = lax.broadcasted_iota(jnp.int32, (R, R), 1)
    same = lambda sh: jnp.right_shift(row, sh) == jnp.right_shift(col, sh)
    m_incl = same(6) & (col <= row)
    m_strict = same(6) & (col < row)
    m_d2 = m_strict & same(1)
    m_merges = [m_strict & same(sh + 1) & jnp.logical_not(same(sh)) for sh in range(1, 6)]
    eye = (row == col).astype(F32)
    neg_a = -jnp.exp(alog_ref[...])

    s_s[...] = jnp.zeros_like(s_s)

    def stack(x):
        return jnp.concatenate([x[:, h * DN_DIM:(h + 1) * DN_DIM] for h in range(DN_HEADS)], axis=0)

    def stack_col(x, c0):
        return jnp.concatenate(
            [jnp.broadcast_to(x[:, c0 + h:c0 + h + 1], (C, DN_DIM)) for h in range(DN_HEADS)], axis=0)

    def gates(ab, row_ok):
        xa = ab + dtb_ref[...]
        softplus = jnp.maximum(xa, 0.0) + jnp.log(1.0 + jnp.exp(-jnp.abs(xa)))
        g = neg_a * softplus
        beta = jax.nn.sigmoid(ab)
        if row_ok is not None:
            g = jnp.where(row_ok, g, 0.0)
            beta = jnp.where(row_ok, beta, 0.0)
        pos = lax.broadcasted_iota(jnp.int32, g.shape, 0) & (C - 1)
        gc = g
        for s in (1, 2, 4, 8, 16, 32):
            gc = gc + jnp.where(pos >= s, pltpu.roll(gc, s, 0), 0.0)
        return gc, beta

    def phase_a(acts, gcl, betal, buf):
        uw_b, qk_b, qg_b, kd_b, el_b = buf
        n = len(acts)
        for c in range(n):
            act, gc, beta = acts[c], gcl[c], betal[c]
            q = stack(act[:, 0:DN_WIDTH])
            k = stack(act[:, DN_WIDTH:2 * DN_WIDTH])
            v = stack(act[:, 2 * DN_WIDTH:3 * DN_WIDTH])
            gcs = stack_col(gc, 0)
            gls = stack_col(jnp.broadcast_to(gc[C - 1:C, :], (C, 128)), 0)
            bs = stack_col(beta, DN_HEADS)
            grow = gcs.T[0:1, :]
            dec = jnp.exp(jnp.where(m_incl, gcs[:, 0:1] - grow, NEG_INF))
            kb = k * bs
            kbf = k.astype(BF16)
            eg = jnp.exp(gcs)
            af_s[c] = _dot_nt(kb.astype(BF16), kbf) * dec
            qk_b[c] = jnp.where(m_incl, _dot_nt(q.astype(BF16), kbf) * dec, 0.0).astype(BF16)
            rhs_s[c] = jnp.concatenate([v * bs, kb * eg], axis=1).astype(BF16)
            qg_b[c] = (q * eg).astype(BF16)
            kd_b[c] = (k * jnp.exp(gls - gcs)).astype(BF16)
            el_b[c] = jnp.exp(gls)
            yield

        for c in range(n):
            t_s[c] = eye - jnp.where(m_d2, af_s[c], 0.0)
        for m_low in m_merges:
            for c in range(n):
                pa_s[c] = _dot(jnp.where(m_low, af_s[c], 0.0).astype(BF16), t_s[c].astype(BF16)).astype(BF16)
            yield
            for c in range(n):
                t_s[c] = t_s[c] - _dot(t_s[c].astype(BF16), pa_s[c])
            yield
        for c in range(n):
            uw_b[c] = _dot(t_s[c].astype(BF16), rhs_s[c])

    def phase_b(zs, buf, outs):
        uw_b, qk_b, qg_b, kd_b, el_b = buf
        for c in range(len(zs)):
            vnew = []
            o_inter = []
            for h in range(DN_HEADS):
                r0 = h * C
                s_h = s_s[h]
                sb = s_h.astype(BF16)
                vn = uw_b[c, r0:r0 + C, 0:DN_DIM] - _dot(uw_b[c, r0:r0 + C, DN_DIM:].astype(BF16), sb)
                o_inter.append(_dot(qg_b[c, r0:r0 + C, :], sb))
                s_s[h] = s_h * el_b[c, r0:r0 + 1, :] + _dot_tn(kd_b[c, r0:r0 + C, :], vn.astype(BF16))
                vnew.append(vn)
            o = jnp.concatenate(o_inter, axis=0) + _dot(qk_b[c], jnp.concatenate(vnew, axis=0).astype(BF16))
            o = _rms(o, onw_ref[...])
            out = jnp.concatenate([o[h * C:(h + 1) * C, :] for h in range(DN_HEADS)], axis=1) * zs[c]
            outs.append(out.astype(BF16))
            yield

    def run(*gens_and_steps):
        gens = [g for g, _ in gens_and_steps]
        lens = [s for _, s in gens_and_steps]
        done = [0] * len(gens)
        alive = [True] * len(gens)
        while any(alive):
            i = min((j for j in range(len(gens)) if alive[j]), key=lambda j: (done[j] + 0.5) / lens[j])
            try:
                next(gens[i])
                done[i] += 1
            except StopIteration:
                alive[i] = False

    bufs = ((uw0_s, qk0_s, qg0_s, kd0_s, el0_s), (uw1_s, qk1_s, qg1_s, kd1_s, el1_s))

    @pl.when(pl.program_id(0) == 0)
    def _():
        meta_ok = lax.broadcasted_iota(jnp.int32, (C, 128), 0) >= META_PAD
        dn0 = dnm_ref[...]
        act0 = dn0[:, 0:3 * DN_WIDTH].astype(F32)
        gc0, beta0 = gates(abm_ref[...], meta_ok)
        run((phase_a([act0], [gc0], [beta0], bufs[1]), 1))
        out0 = []
        run((phase_b([dn0[:, 3 * DN_WIDTH:].astype(F32)], bufs[1], out0), 1))
        ymeta_s[...] = out0[0]
        smeta_s[...] = s_s[...]

    s_s[...] = smeta_s[...]
    ym_ref[0] = ymeta_s[...]

    rows = [slice(c * C, (c + 1) * C) for c in range(DN_GROUP)]
    a_steps = DN_GROUP + 12
    b_steps = DN_GROUP + 1

    def group_a(p, buf):
        r0 = p * DN_GROUP_ROWS
        if not isinstance(p, int):
            r0 = pl.multiple_of(r0, DN_GROUP_ROWS)
        act = dnx_ref[0, pl.ds(r0, DN_GROUP_ROWS), 0:3 * DN_WIDTH].astype(F32)
        gc, beta = gates(abx_ref[0, pl.ds(r0, DN_GROUP_ROWS), :], None)
        yield
        yield from phase_a([act[r, :] for r in rows], [gc[r, :] for r in rows], [beta[r, :] for r in rows], buf)

    def group_b(p, buf):
        r0 = p * DN_GROUP_ROWS
        if not isinstance(p, int):
            r0 = pl.multiple_of(r0, DN_GROUP_ROWS)
        z = dnx_ref[0, pl.ds(r0, DN_GROUP_ROWS), 3 * DN_WIDTH:].astype(F32)
        outs = []
        yield from phase_b([z[r, :] for r in rows], buf, outs)
        yx_ref[0, pl.ds(r0, DN_GROUP_ROWS), :] = jnp.concatenate(outs, axis=0)

    n_groups = SEQ // DN_GROUP_ROWS
    run((group_a(0, bufs[0]), a_steps))

    def body(kk, carry):
        p = 2 * kk
        run((group_a(p + 1, bufs[1]), a_steps), (group_b(p, bufs[0]), b_steps))
        run((group_a(p + 2, bufs[0]), a_steps), (group_b(p + 1, bufs[1]), b_steps))
        return carry

    lax.fori_loop(0, n_groups // 2 - 1, body, 0)
    run((group_a(n_groups - 1, bufs[1]), a_steps), (group_b(n_groups - 2, bufs[0]), b_steps))
    run((group_b(n_groups - 1, bufs[1]), b_steps))


def _deltanet(dnx, dnm, abx, abm, alog, dtb, onw):
    nb = dnx.shape[0]
    const = lambda shape: pl.BlockSpec(shape, lambda b: (0,) * len(shape))
    return pl.pallas_call(
        _deltanet_kernel,
        grid=(nb,),
        in_specs=[
            pl.BlockSpec((1, SEQ, N_DN), lambda b: (b, 0, 0)),
            const((META_ROWS, N_DN)),
            pl.BlockSpec((1, SEQ, N_AB), lambda b: (b, 0, 0)),
            const((META_ROWS, N_AB)),
            const((1, 128)), const((1, 128)), const((1, 128)),
        ],
        out_specs=[
            pl.BlockSpec((1, SEQ, DN_WIDTH), lambda b: (b, 0, 0)),
            pl.BlockSpec((1, META_ROWS, DN_WIDTH), lambda b: (b, 0, 0)),
        ],
        out_shape=[
            jax.ShapeDtypeStruct((nb, SEQ, DN_WIDTH), BF16),
            jax.ShapeDtypeStruct((nb, META_ROWS, DN_WIDTH), BF16),
        ],
        scratch_shapes=[
            pltpu.VMEM((DN_HEADS, DN_DIM, DN_DIM), F32),
            pltpu.VMEM((DN_GROUP, DN_STACK, DN_STACK), F32),
            pltpu.VMEM((DN_GROUP, DN_STACK, DN_STACK), F32),
            pltpu.VMEM((DN_GROUP, DN_STACK, DN_STACK), BF16),
            pltpu.VMEM((DN_GROUP, DN_STACK, 2 * DN_DIM), BF16),
        ] + 2 * [
            pltpu.VMEM((DN_GROUP, DN_STACK, 2 * DN_DIM), F32),
            pltpu.VMEM((DN_GROUP, DN_STACK, DN_STACK), BF16),
            pltpu.VMEM((DN_GROUP, DN_STACK, DN_DIM), BF16),
            pltpu.VMEM((DN_GROUP, DN_STACK, DN_DIM), BF16),
            pltpu.VMEM((DN_GROUP, DN_STACK, DN_DIM), F32),
        ] + [
            pltpu.VMEM((DN_HEADS, DN_DIM, DN_DIM), F32),
            pltpu.VMEM((META_ROWS, DN_WIDTH), BF16),
        ],
        compiler_params=_cparams(("arbitrary",)),
        name="deltanet",
    )(dnx, dnm, abx, abm, alog, dtb, onw)


FFN_ROWS = 1024
FFN_HALO = 16
N_FF_BLK = D_FF // FF_BLK


def _ffn_kernel(x_ref, xh_ref, mh_ref, ya_ref, yah_ref, yam_ref, yd_ref, ydh_ref, ydm_ref,
                wo_ref, nw_ref, wg_ref, wu_ref, cw_ref, cb_ref, wd_ref,
                o_ref, u_s, g0_s, g1_s, up0_s, up1_s, act_s):
    r = pl.program_id(1)
    mixed = jnp.concatenate([ya_ref[0], yd_ref[0]], axis=1)
    h_mid = x_ref[0] + _dot(mixed, wo_ref[...])
    o_ref[0] = h_mid
    u_s[FFN_HALO:, :] = _rms(h_mid, nw_ref[...]).astype(BF16)
    first = r == 0
    mixed_h = jnp.concatenate([jnp.where(first, yam_ref[0], yah_ref[0]),
                               jnp.where(first, ydm_ref[0], ydh_ref[0])], axis=1)
    h_halo = jnp.where(first, mh_ref[...], xh_ref[0]) + _dot(mixed_h, wo_ref[...])
    u_s[0:FFN_HALO, :] = _rms(h_halo, nw_ref[...]).astype(BF16)

    g_bufs = (g0_s, g1_s)
    up_bufs = (up0_s, up1_s)

    def project(f):
        cols = slice(f * FF_BLK, (f + 1) * FF_BLK)
        g_bufs[f % 2][...] = _dot(u_s[...], wg_ref[:, cols])
        up_bufs[f % 2][...] = _dot(u_s[FFN_HALO:, :], wu_ref[:, cols])

    project(0)
    for f in range(N_FF_BLK):
        if f + 1 < N_FF_BLK:
            project(f + 1)
        g_s = g_bufs[f % 2]
        cols = slice(f * FF_BLK, (f + 1) * FF_BLK)
        gate = (cw_ref[2:3, cols] * g_s[FFN_HALO:, :]
                + cw_ref[1:2, cols] * g_s[FFN_HALO - 1:FFN_HALO - 1 + FFN_ROWS, :]
                + cw_ref[0:1, cols] * g_s[FFN_HALO - 2:FFN_HALO - 2 + FFN_ROWS, :]
                + cb_ref[:, cols])
        act_s[:, cols] = (_silu(gate) * up_bufs[f % 2][...]).astype(BF16)
    o_ref[0] += _dot(act_s[...], wd_ref[...])


def _ffn(x, hp_meta, yax, yam, ydx, ydm, wo, nw, wg, wu, cw, cb, wd):
    nb = x.shape[0]
    nr = SEQ // FFN_ROWS
    hb = FFN_ROWS // FFN_HALO
    halo_idx = lambda b, r: (b, jnp.maximum(r * hb - 1, 0), 0)
    meta_idx = lambda b, r: (b, META_ROWS // FFN_HALO - 1, 0)
    main_idx = lambda b, r: (b, r, 0)
    resident = lambda shape: pl.BlockSpec(shape, lambda b, r: (0,) * len(shape),
                                          pipeline_mode=pl.Buffered(1))
    return pl.pallas_call(
        _ffn_kernel,
        grid=(nb, nr),
        in_specs=[
            pl.BlockSpec((1, FFN_ROWS, D_MODEL), main_idx),
            pl.BlockSpec((1, FFN_HALO, D_MODEL), halo_idx),
            pl.BlockSpec((FFN_HALO, D_MODEL), lambda b, r: (META_ROWS // FFN_HALO - 1, 0)),
            pl.BlockSpec((1, FFN_ROWS, MLA_HEADS * V_HEAD), main_idx),
            pl.BlockSpec((1, FFN_HALO, MLA_HEADS * V_HEAD), halo_idx),
            pl.BlockSpec((1, FFN_HALO, MLA_HEADS * V_HEAD), meta_idx),
            pl.BlockSpec((1, FFN_ROWS, DN_WIDTH), main_idx),
            pl.BlockSpec((1, FFN_HALO, DN_WIDTH), halo_idx),
            pl.BlockSpec((1, FFN_HALO, DN_WIDTH), meta_idx),
            resident((D_MODEL, D_MODEL)),
            resident((1, D_MODEL)),
            resident((D_MODEL, D_FF)),
            resident((D_MODEL, D_FF)),
            resident((3, D_FF)),
            resident((1, D_FF)),
            resident((D_FF, D_MODEL)),
        ],
        out_specs=pl.BlockSpec((1, FFN_ROWS, D_MODEL), main_idx),
        out_shape=jax.ShapeDtypeStruct((nb, SEQ, D_MODEL), F32),
        scratch_shapes=[
            pltpu.VMEM((FFN_HALO + FFN_ROWS, D_MODEL), BF16),
            pltpu.VMEM((FFN_HALO + FFN_ROWS, FF_BLK), F32),
            pltpu.VMEM((FFN_HALO + FFN_ROWS, FF_BLK), F32),
            pltpu.VMEM((FFN_ROWS, FF_BLK), F32),
            pltpu.VMEM((FFN_ROWS, FF_BLK), F32),
            pltpu.VMEM((FFN_ROWS, D_FF), BF16),
        ],
        compiler_params=_cparams(("arbitrary", "arbitrary")),
        name="outproj_ffn",
    )(x, x, hp_meta, yax, yax, yam, ydx, ydx, ydm, wo, nw, wg, wu, cw, cb, wd)


def _rot_cols(w):
    half = QK_ROPE // 2
    return jnp.concatenate([-w[..., half:], w[..., :half]], axis=-1)


def _swap_halves(w):
    half = QK_ROPE // 2
    return jnp.concatenate([w[..., half:], w[..., :half]], axis=-1)


def _pad_lanes(v, n=128):
    return jnp.pad(v.astype(F32), (0, n - v.shape[0])).reshape(1, n)


def _layer(x, hp_meta, l, attn_norm_w, w_in, q_a_norm_w, w_q_b, kv_a_norm_w, w_kv_b, q_norm_w,
           k_norm_w, mla_out_norm_w, dn_conv_w, dn_A_log, dn_dt_bias, dn_out_norm_w, w_out,
           ffn_norm_w, w_gate, w_up, ffn_conv_w, ffn_conv_b, w_down):
    nb = x.shape[0]
    c1 = Q_LORA
    c2 = c1 + KV_LORA
    c3 = c2 + QK_ROPE
    c4 = c3 + 3 * DN_WIDTH
    c5 = c4 + DN_WIDTH
    wint = w_in[l].T.astype(BF16)
    k_pe_w = wint[c2:c3, :]
    half = QK_ROPE // 2
    w1 = jnp.concatenate(
        [wint[:c2, :], k_pe_w, -k_pe_w[half:, :], k_pe_w[:half, :], wint[c3:c5, :], wint[c5:, :],
         jnp.zeros((N_AB - 2 * DN_HEADS, D_MODEL), BF16)], axis=0).T

    wqb = w_q_b[l].reshape(Q_LORA, MLA_HEADS, QK_HEAD)
    wqt = jnp.concatenate([wqb[..., :QK_NOPE], wqb[..., QK_NOPE:], _rot_cols(wqb[..., QK_NOPE:])],
                          axis=-1).reshape(Q_LORA, MLA_HEADS * HEAD_SLOT).T.astype(BF16)
    wkvb = w_kv_b[l].reshape(KV_LORA, MLA_HEADS, QK_NOPE + V_HEAD)
    wk = wkvb[..., :QK_NOPE].reshape(KV_LORA, MLA_HEADS * QK_NOPE).astype(BF16)
    wvt = wkvb[..., QK_NOPE:].reshape(KV_LORA, MLA_HEADS * V_HEAD).T.astype(BF16)
    qn = q_norm_w[l].astype(F32)
    kn = k_norm_w[l].astype(F32)
    qnn = qn[:QK_NOPE].reshape(QK_NOPE, 1)
    knn = kn[:QK_NOPE].reshape(1, 128)
    qrw = jnp.concatenate([qn[QK_NOPE:], _swap_halves(qn[QK_NOPE:])]).reshape(2 * QK_ROPE, 1)
    krw = jnp.concatenate([kn[QK_NOPE:], _swap_halves(kn[QK_NOPE:])]).reshape(1, 128)

    half = QK_ROPE // 2
    inv_freq = ROPE_THETA ** (-jnp.arange(half, dtype=F32) / half)
    pos = (jnp.arange(ATT_ROWS, dtype=jnp.int32) - ATT_META_VALID).astype(F32)
    ang = pos[:, None] * inv_freq[None, :]
    cs = jnp.concatenate([jnp.cos(ang), jnp.cos(ang), jnp.sin(ang), jnp.sin(ang)], axis=1)
    cst = cs.T

    nw1 = attn_norm_w[l].astype(F32).reshape(1, D_MODEL)
    qaw = q_a_norm_w[l].astype(F32).reshape(1, Q_LORA)
    kvaw = kv_a_norm_w[l].astype(F32).reshape(1, KV_LORA)
    cw = dn_conv_w[l].astype(F32)
    no_hist = jnp.zeros((CONV_HIST, 3 * DN_WIDTH), F32)
    latm, dnm, abm, meta_tail = _inproj(hp_meta, nw1, w1, qaw, kvaw, cw, no_hist, META_ROWS, 1)
    latx, dnx, abx, _ = _inproj(x.reshape(nb * SEQ, D_MODEL), nw1, w1, qaw, kvaw, cw, meta_tail,
                                INPROJ_ROWS, SEQ // INPROJ_ROWS)
    latx = latx.reshape(nb, SEQ, N_LAT)
    dnx = dnx.reshape(nb, SEQ, N_DN)
    abx = abx.reshape(nb, SEQ, N_AB)

    yax, yam, wo, wg, wu, wd = _mla(
        latx, latm, cs, cst, wqt, wk, wvt,
        qnn, qrw, knn, krw, mla_out_norm_w[l].astype(F32).reshape(V_HEAD, 1),
        [w_out[l].astype(F32), w_gate[l].astype(F32), w_up[l].astype(F32), w_down[l].astype(F32)])
    ydx, ydm = _deltanet(dnx, dnm, abx, abm, _pad_lanes(dn_A_log[l]), _pad_lanes(dn_dt_bias[l]),
                         dn_out_norm_w[l].astype(F32).reshape(1, DN_DIM))
    return _ffn(x, hp_meta, yax, yam, ydx, ydm, wo,
                ffn_norm_w[l].astype(F32).reshape(1, D_MODEL), wg, wu, ffn_conv_w[l].astype(F32),
                ffn_conv_b[l].astype(F32).reshape(1, D_FF), wd)


def kernel(x, meta_tokens, attn_norm_w, w_in, q_a_norm_w, w_q_b, kv_a_norm_w, w_kv_b, q_norm_w, k_norm_w, mla_out_norm_w, dn_conv_w, dn_A_log, dn_dt_bias, dn_out_norm_w, w_out, ffn_norm_w, w_gate, w_up, ffn_conv_w, ffn_conv_b, w_down):
    assert x.shape[1:] == (SEQ, D_MODEL) and w_in.shape[0] == 1
    hp_meta = jnp.concatenate([jnp.zeros((META_PAD, D_MODEL), x.dtype), meta_tokens.astype(x.dtype)], axis=0)
    return _layer(x, hp_meta, 0, attn_norm_w, w_in, q_a_norm_w, w_q_b, kv_a_norm_w, w_kv_b, q_norm_w,
                  k_norm_w, mla_out_norm_w, dn_conv_w, dn_A_log, dn_dt_bias, dn_out_norm_w, w_out,
                  ffn_norm_w, w_gate, w_up, ffn_conv_w, ffn_conv_b, w_down)
```

```python
import functools
import math

import jax
import jax.numpy as jnp
from jax import lax
from jax.experimental import pallas as pl
from jax.experimental.pallas import tpu as pltpu

F32 = jnp.float32
BF16 = jnp.bfloat16

D_MODEL = 1024
SEQ = 2048
N_META = 16
META_ROWS = 64
META_PAD = META_ROWS - N_META

MLA_HEADS = 4
QK_NOPE = 128
QK_ROPE = 64
QK_HEAD = QK_NOPE + QK_ROPE
V_HEAD = 128
Q_LORA = 256
KV_LORA = 256
ROPE_THETA = 10000.0
HEAD_SLOT = 256

DN_HEADS = 4
DN_DIM = 128
DN_WIDTH = DN_HEADS * DN_DIM
DN_CHUNK = 64
DN_STACK = DN_HEADS * DN_CHUNK
DN_CONV = 4

D_FF = 2816
FF_BLK = 256
NORM_EPS = 1e-6

N_LAT = Q_LORA + KV_LORA + 2 * QK_ROPE
C_DN = N_LAT
N_DN = 4 * DN_WIDTH
C_AB = C_DN + N_DN
N_AB = 128
N_PROJ = C_AB + N_AB

VMEM_LIMIT = 56 * 1024 * 1024


def _cparams(sem):
    return pltpu.CompilerParams(dimension_semantics=sem, vmem_limit_bytes=VMEM_LIMIT)


def _rms(x, w):
    return x * lax.rsqrt(jnp.mean(x * x, axis=-1, keepdims=True) + NORM_EPS) * w


def _dot(a, b):
    return jnp.dot(a, b, preferred_element_type=F32)


def _dot_nt(a, b):
    return lax.dot_general(a, b, (((1,), (1,)), ((), ())), preferred_element_type=F32)


def _dot_tn(a, b):
    return lax.dot_general(a, b, (((0,), (0,)), ((), ())), preferred_element_type=F32)


def _silu(x):
    return x * jax.nn.sigmoid(x)


assert DN_CONV == 4
CONV_HIST = 16
INPROJ_ROWS = 1024
INPROJ_SUB = 256


def _inproj_kernel(tiles_per_seq, x_ref, nw_ref, w_ref, qaw_ref, kvaw_ref, cw_ref, hist_in_ref,
                   lat_ref, dn_ref, ab_ref, tail_ref, hist_s, p0_s, p1_s):
    n = x_ref.shape[0]
    sub = min(INPROJ_SUB, n)
    p_bufs = (p0_s, p1_s)

    @pl.when(pl.program_id(0) % tiles_per_seq == 0)
    def _():
        hist_s[...] = hist_in_ref[...]

    def project(r):
        u = _rms(x_ref[r * sub:(r + 1) * sub, :], nw_ref[...]).astype(BF16)
        p_bufs[r % 2][0:sub, :] = _dot(u, w_ref[...])

    def post(r):
        p = p_bufs[r % 2]
        rows = slice(r * sub, (r + 1) * sub)
        lat_ref[rows, 0:Q_LORA] = _rms(p[0:sub, 0:Q_LORA], qaw_ref[...]).astype(BF16)
        lat_ref[rows, Q_LORA:Q_LORA + KV_LORA] = _rms(p[0:sub, Q_LORA:Q_LORA + KV_LORA],
                                                      kvaw_ref[...]).astype(BF16)
        lat_ref[rows, Q_LORA + KV_LORA:N_LAT] = p[0:sub, Q_LORA + KV_LORA:N_LAT].astype(BF16)
        ab_ref[rows, :] = p[0:sub, C_AB:C_AB + N_AB]
        pre = p[0:sub, C_DN:C_DN + 3 * DN_WIDTH]
        full = jnp.concatenate([hist_s[...], pre], axis=0)
        hist_s[...] = pre[sub - CONV_HIST:, :]
        full1 = pltpu.roll(full, 1, 0)
        near = cw_ref[3:4, :] * full + cw_ref[2:3, :] * full1
        far = cw_ref[1:2, :] * full + cw_ref[0:1, :] * full1
        act = _silu((near + pltpu.roll(far, 2, 0))[CONV_HIST:, :])
        for h in range(2 * DN_HEADS):
            a = act[:, h * DN_DIM:(h + 1) * DN_DIM]
            a = a * lax.rsqrt(jnp.sum(a * a, axis=-1, keepdims=True) + NORM_EPS)
            if h < DN_HEADS:
                a = a * (1.0 / math.sqrt(DN_DIM))
            dn_ref[rows, h * DN_DIM:(h + 1) * DN_DIM] = a.astype(BF16)
        dn_ref[rows, 2 * DN_WIDTH:3 * DN_WIDTH] = act[:, 2 * DN_WIDTH:].astype(BF16)
        dn_ref[rows, 3 * DN_WIDTH:] = _silu(p[0:sub, C_DN + 3 * DN_WIDTH:C_DN + N_DN]).astype(BF16)

    project(0)
    for r in range(n // sub):
        if r + 1 < n // sub:
            project(r + 1)
        post(r)
    tail_ref[...] = hist_s[...]


def _inproj(x2d, nw, w, qaw, kvaw, cw, hist_in, row_tile, tiles_per_seq):
    rows = x2d.shape[0]
    grid = (rows // row_tile,)
    const = lambda shape: pl.BlockSpec(shape, lambda i: (0,) * len(shape))
    return pl.pallas_call(
        functools.partial(_inproj_kernel, tiles_per_seq),
        grid=grid,
        in_specs=[
            pl.BlockSpec((row_tile, D_MODEL), lambda i: (i, 0)),
            const((1, D_MODEL)),
            const((D_MODEL, N_PROJ)),
            const((1, Q_LORA)),
            const((1, KV_LORA)),
            const((DN_CONV, 3 * DN_WIDTH)),
            const((CONV_HIST, 3 * DN_WIDTH)),
        ],
        out_specs=[
            pl.BlockSpec((row_tile, N_LAT), lambda i: (i, 0)),
            pl.BlockSpec((row_tile, N_DN), lambda i: (i, 0)),
            pl.BlockSpec((row_tile, N_AB), lambda i: (i, 0)),
            pl.BlockSpec((CONV_HIST, 3 * DN_WIDTH), lambda i: (i, 0)),
        ],
        out_shape=[
            jax.ShapeDtypeStruct((rows, N_LAT), BF16),
            jax.ShapeDtypeStruct((rows, N_DN), BF16),
            jax.ShapeDtypeStruct((rows, N_AB), F32),
            jax.ShapeDtypeStruct((grid[0] * CONV_HIST, 3 * DN_WIDTH), F32),
        ],
        scratch_shapes=[pltpu.VMEM((CONV_HIST, 3 * DN_WIDTH), F32),
                        pltpu.VMEM((min(INPROJ_SUB, row_tile), N_PROJ), F32),
                        pltpu.VMEM((min(INPROJ_SUB, row_tile), N_PROJ), F32)],
        compiler_params=_cparams(("arbitrary",)),
        name="inproj",
    )(x2d, nw, w, qaw, kvaw, cw, hist_in)


ATT_TQ = 256
V_SLOT = V_HEAD + 16
ATT_PROJ_ROWS = 512
ATT_META = 2 * META_ROWS
ATT_META_VALID = ATT_META - N_META
ATT_ROWS = ATT_META + SEQ
NEG_INF = float("-inf")


def _mla_kernel(n_w, latx_ref, latm_ref, cs_ref, cst_ref, wqt_ref, wk_ref, wvt_ref,
                qnn_ref, qrw_ref, knn_ref, krw_ref, onw_ref, *refs):
    w_f32_refs, (yx_ref, ym_ref), w_bf16_refs = refs[:n_w], refs[n_w:n_w + 2], refs[n_w + 2:2 * n_w + 2]
    km_s, qm_s, vm_s, kx_s, qx_s, vx_s, acc_s, stm_s, st0_s, st1_s, p_s = refs[2 * n_w + 2:]
    for w_f32, w_bf16 in zip(w_f32_refs, w_bf16_refs):
        w_bf16[...] = w_f32[...].astype(BF16)
    low = lax.broadcasted_iota(jnp.int32, (1, 128), 1) < QK_ROPE
    scale = 1.0 / math.sqrt(QK_HEAD)

    def project(lat, cs, cst):
        nrows = lat.shape[0]
        qn = lat[:, 0:Q_LORA]
        kvn = lat[:, Q_LORA:Q_LORA + KV_LORA]
        pe = lat[:, Q_LORA + KV_LORA:N_LAT]
        qt = _dot(wqt_ref[...], qn.T.astype(BF16))
        vt = _dot(wvt_ref[...], kvn.T.astype(BF16)).astype(BF16)
        kn = _dot(kvn.astype(BF16), wk_ref[...])
        a = pe * (cs * krw_ref[...])
        k_rope = jnp.where(low, a + pltpu.roll(a, QK_ROPE, 1), 0.0)
        pe_ss = jnp.sum(jnp.where(low, pe * pe, 0.0), axis=-1, keepdims=True)
        cos_t = cst[0:QK_ROPE, :]
        sin_t = cst[QK_ROPE:2 * QK_ROPE, :]
        k_parts = []
        q_parts = []
        for h in range(MLA_HEADS):
            nope = kn[:, h * QK_NOPE:(h + 1) * QK_NOPE]
            rs = lax.rsqrt((jnp.sum(nope * nope, axis=-1, keepdims=True) + pe_ss) * (1.0 / QK_HEAD) + NORM_EPS)
            k_parts += [(nope * rs * knn_ref[...]).astype(BF16), (k_rope * rs).astype(BF16)]
            r0 = h * HEAD_SLOT
            qnope = qt[r0:r0 + QK_NOPE, :]
            qrope = qt[r0 + QK_NOPE:r0 + QK_HEAD, :]
            qrot = qt[r0 + QK_HEAD:r0 + HEAD_SLOT, :]
            ssq = (jnp.sum(qnope * qnope, axis=0, keepdims=True)
                   + jnp.sum(qrope * qrope, axis=0, keepdims=True))
            rsq = lax.rsqrt(ssq * (1.0 / QK_HEAD) + NORM_EPS) * scale
            roped = (qrope * (qrw_ref[0:QK_ROPE, :] * cos_t)
                     + qrot * (qrw_ref[QK_ROPE:2 * QK_ROPE, :] * sin_t))
            q_parts += [(qnope * qnn_ref[...] * rsq).astype(BF16), (roped * rsq).astype(BF16),
                        jnp.zeros((HEAD_SLOT - QK_HEAD, nrows), BF16)]
        ones_row = (lax.broadcasted_iota(jnp.int32, (V_SLOT - V_HEAD, nrows), 0) == 0).astype(BF16)
        v_parts = []
        for h in range(MLA_HEADS):
            v_parts += [vt[h * V_HEAD:(h + 1) * V_HEAD, :], ones_row]
        return jnp.concatenate(k_parts, axis=1), jnp.concatenate(q_parts, axis=0), jnp.concatenate(v_parts, axis=0)

    latm = jnp.concatenate([jnp.zeros((META_ROWS, N_LAT), F32), latm_ref[...].astype(F32)], axis=0)
    km_s[...], qm_s[...], vm_s[...] = project(latm, cs_ref[0:ATT_META, :], cst_ref[:, 0:ATT_META])
    for c in range(SEQ // ATT_PROJ_ROWS):
        r0 = c * ATT_PROJ_ROWS
        k, qt, vt = project(latx_ref[0, r0:r0 + ATT_PROJ_ROWS, :].astype(F32),
                            cs_ref[ATT_META + r0:ATT_META + r0 + ATT_PROJ_ROWS, :],
                            cst_ref[:, ATT_META + r0:ATT_META + r0 + ATT_PROJ_ROWS])
        kx_s[r0:r0 + ATT_PROJ_ROWS, :] = k
        for t in range(ATT_PROJ_ROWS // ATT_TQ):
            qx_s[c * (ATT_PROJ_ROWS // ATT_TQ) + t] = qt[:, t * ATT_TQ:(t + 1) * ATT_TQ]
            vx_s[c * (ATT_PROJ_ROWS // ATT_TQ) + t] = vt[:, t * ATT_TQ:(t + 1) * ATT_TQ]

    def next_block(st, m):
        m_new = jnp.maximum(m, jnp.max(st, axis=0, keepdims=True))
        alpha = jnp.exp(m - m_new)
        p = jnp.exp((st - m_new).astype(BF16))
        return p, m_new, alpha

    def finish(acc_ext):
        o = acc_ext[0:V_HEAD, :] * (1.0 / acc_ext[V_HEAD:V_HEAD + 1, :])
        o = o * lax.rsqrt(jnp.mean(o * o, axis=0, keepdims=True) + NORM_EPS) * onw_ref[...]
        return o.T.astype(BF16)

    hs = lambda h: slice(h * HEAD_SLOT, (h + 1) * HEAD_SLOT)
    vs = lambda h: slice(h * V_SLOT, (h + 1) * V_SLOT)
    outs = lambda h: slice(h * V_HEAD, (h + 1) * V_HEAD)

    mkey = lax.broadcasted_iota(jnp.int32, (ATT_META, ATT_META), 0)
    mqry = lax.broadcasted_iota(jnp.int32, (ATT_META, ATT_META), 1)
    meta_mask = (mkey <= mqry) & ((mkey >= ATT_META_VALID) | (mkey == mqry))
    for h in range(MLA_HEADS):
        st = jnp.where(meta_mask, _dot(km_s[:, hs(h)], qm_s[hs(h), :]), NEG_INF)
        p, _, _ = next_block(st, jnp.full((1, ATT_META), NEG_INF, F32))
        o = finish(_dot(vm_s[vs(h), :], p))
        ym_ref[0, :, outs(h)] = o[META_ROWS:, :]

    meta_key_ok = lax.broadcasted_iota(jnp.int32, (ATT_META, ATT_TQ), 0) >= ATT_META_VALID
    diag_mask = (lax.broadcasted_iota(jnp.int32, (ATT_TQ, ATT_TQ), 0)
                 <= lax.broadcasted_iota(jnp.int32, (ATT_TQ, ATT_TQ), 1))

    def k_blk(j):
        k0 = j * ATT_TQ
        if not isinstance(j, int):
            k0 = pl.multiple_of(k0, ATT_TQ)
        return lambda h: kx_s[pl.ds(k0, ATT_TQ), hs(h)]

    v_blk = lambda j: (lambda h: vx_s[j, vs(h), :])

    acc_t, p_t, stm_t, st_t = acc_s, p_s, stm_s, (st0_s, st1_s)

    for i in range(SEQ // ATT_TQ):
        def scores(buf, k_of, nk):
            for h in range(MLA_HEADS):
                buf[h, 0:nk, :] = _dot(k_of(h), qx_s[i, hs(h), :])

        def absorb(buf, v_of, nk, mask, ms):
            ms2, alphas = [], []
            for h in range(MLA_HEADS):
                st = buf[h, 0:nk, :]
                if mask is not None:
                    st = jnp.where(mask, st, NEG_INF)
                p, m, alpha = next_block(st, ms[h])
                p_t[h, 0:nk, :] = p
                ms2.append(m)
                alphas.append(alpha)
            for h in range(MLA_HEADS):
                acc_t[h] = acc_t[h] * alphas[h] + _dot(v_of(h), p_t[h, 0:nk, :])
            return tuple(ms2)

        acc_t[...] = jnp.zeros_like(acc_t)
        scores(stm_t, lambda h: km_s[:, hs(h)], ATT_META)
        scores(st_t[0], k_blk(0), ATT_TQ)
        ms = (jnp.full((1, ATT_TQ), NEG_INF, F32),) * MLA_HEADS
        ms = absorb(stm_t, lambda h: vm_s[vs(h), :], ATT_META, meta_key_ok, ms)

        def pair(t, ms):
            scores(st_t[1], k_blk(2 * t + 1), ATT_TQ)
            ms = absorb(st_t[0], v_blk(2 * t), ATT_TQ, None, ms)
            scores(st_t[0], k_blk(2 * t + 2), ATT_TQ)
            return absorb(st_t[1], v_blk(2 * t + 1), ATT_TQ, None, ms)

        if i // 2 > 0:
            ms = lax.fori_loop(0, i // 2, pair, ms)
        if i % 2 == 1:
            scores(st_t[1], k_blk(i), ATT_TQ)
            ms = absorb(st_t[0], v_blk(i - 1), ATT_TQ, None, ms)
        absorb(st_t[i % 2], v_blk(i), ATT_TQ, diag_mask, ms)
        for h in range(MLA_HEADS):
            yx_ref[0, i * ATT_TQ:(i + 1) * ATT_TQ, outs(h)] = finish(acc_t[h])


def _mla(latx, latm, cs, cst, wqt, wk, wvt, qnn, qrw, knn, krw, onw, cast_weights):
    nb = latx.shape[0]
    const = lambda shape: pl.BlockSpec(shape, lambda b: (0,) * len(shape))
    row_slice = lambda w: pl.BlockSpec((w.shape[0] // nb, w.shape[1]), lambda b: (b, 0))
    nq = SEQ // ATT_TQ
    return pl.pallas_call(
        functools.partial(_mla_kernel, len(cast_weights)),
        grid=(nb,),
        in_specs=[
            pl.BlockSpec((1, SEQ, N_LAT), lambda b: (b, 0, 0)),
            const((META_ROWS, N_LAT)),
            const((ATT_ROWS, 128)),
            const((128, ATT_ROWS)),
            const((MLA_HEADS * HEAD_SLOT, Q_LORA)),
            const((KV_LORA, MLA_HEADS * QK_NOPE)),
            const((MLA_HEADS * V_HEAD, KV_LORA)),
            const((QK_NOPE, 1)), const((2 * QK_ROPE, 1)), const((1, 128)), const((1, 128)),
            const((V_HEAD, 1)),
        ] + [row_slice(w) for w in cast_weights],
        out_specs=[
            pl.BlockSpec((1, SEQ, MLA_HEADS * V_HEAD), lambda b: (b, 0, 0)),
            pl.BlockSpec((1, META_ROWS, MLA_HEADS * V_HEAD), lambda b: (b, 0, 0)),
        ] + [row_slice(w) for w in cast_weights],
        out_shape=[
            jax.ShapeDtypeStruct((nb, SEQ, MLA_HEADS * V_HEAD), BF16),
            jax.ShapeDtypeStruct((nb, META_ROWS, MLA_HEADS * V_HEAD), BF16),
        ] + [jax.ShapeDtypeStruct(w.shape, BF16) for w in cast_weights],
        scratch_shapes=[
            pltpu.VMEM((ATT_META, MLA_HEADS * HEAD_SLOT), BF16),
            pltpu.VMEM((MLA_HEADS * HEAD_SLOT, ATT_META), BF16),
            pltpu.VMEM((MLA_HEADS * V_SLOT, ATT_META), BF16),
            pltpu.VMEM((SEQ, MLA_HEADS * HEAD_SLOT), BF16),
            pltpu.VMEM((nq, MLA_HEADS * HEAD_SLOT, ATT_TQ), BF16),
            pltpu.VMEM((nq, MLA_HEADS * V_SLOT, ATT_TQ), BF16),
        ] + [
            pltpu.VMEM((MLA_HEADS, V_SLOT, ATT_TQ), F32),
            pltpu.VMEM((MLA_HEADS, ATT_META, ATT_TQ), F32),
            pltpu.VMEM((MLA_HEADS, ATT_TQ, ATT_TQ), F32),
            pltpu.VMEM((MLA_HEADS, ATT_TQ, ATT_TQ), F32),
            pltpu.VMEM((MLA_HEADS, ATT_TQ, ATT_TQ), BF16),
        ],
        compiler_params=_cparams(("arbitrary",)),
        name="mla",
    )(latx, latm, cs, cst, wqt, wk, wvt, qnn, qrw, knn, krw, onw, *cast_weights)


DN_GROUP = 4
DN_GROUP_ROWS = DN_GROUP * DN_CHUNK


def _deltanet_kernel(dnx_ref, dnm_ref, abx_ref, abm_ref, alog_ref, dtb_ref, onw_ref,
                     yx_ref, ym_ref, s_s, af_s, t_s, pa_s, rhs_s,
                     uw0_s, qk0_s, qg0_s, kd0_s, el0_s, uw1_s, qk1_s, qg1_s, kd1_s, el1_s,
                     smeta_s, ymeta_s):
    C = DN_CHUNK
    R = DN_STACK
    row = lax.broadcasted_iota(jnp.int32, (R, R), 0)
    col = lax.broadcasted_iota(jnp.int32, (R, R), 1)
    same = lambda sh: jnp.right_shift(row, sh) == jnp.right_shift(col, sh)
    m_incl = same(6) & (col <= row)
    m_strict = same(6) & (col < row)
    m_d2 = m_strict & same(1)
    m_merges = [m_strict & same(sh + 1) & jnp.logical_not(same(sh)) for sh in range(1, 6)]
    eye = (row == col).astype(F32)
    neg_a = -jnp.exp(alog_ref[...])

    s_s[...] = jnp.zeros_like(s_s)

    def stack(x):
        return jnp.concatenate([x[:, h * DN_DIM:(h + 1) * DN_DIM] for h in range(DN_HEADS)], axis=0)

    def stack_col(x, c0):
        return jnp.concatenate(
            [jnp.broadcast_to(x[:, c0 + h:c0 + h + 1], (C, DN_DIM)) for h in range(DN_HEADS)], axis=0)

    def gates(ab, row_ok):
        xa = ab + dtb_ref[...]
        softplus = jnp.maximum(xa, 0.0) + jnp.log(1.0 + jnp.exp(-jnp.abs(xa)))
        g = neg_a * softplus
        beta = jax.nn.sigmoid(ab)
        if row_ok is not None:
            g = jnp.where(row_ok, g, 0.0)
            beta = jnp.where(row_ok, beta, 0.0)
        pos = lax.broadcasted_iota(jnp.int32, g.shape, 0) & (C - 1)
        gc = g
        for s in (1, 2, 4, 8, 16, 32):
            gc = gc + jnp.where(pos >= s, pltpu.roll(gc, s, 0), 0.0)
        return gc, beta

    def phase_a(acts, gcl, betal, buf):
        uw_b, qk_b, qg_b, kd_b, el_b = buf
        n = len(acts)
        for c in range(n):
            act, gc, beta = acts[c], gcl[c], betal[c]
            q = stack(act[:, 0:DN_WIDTH])
            k = stack(act[:, DN_WIDTH:2 * DN_WIDTH])
            v = stack(act[:, 2 * DN_WIDTH:3 * DN_WIDTH])
            gcs = stack_col(gc, 0)
            gls = stack_col(jnp.broadcast_to(gc[C - 1:C, :], (C, 128)), 0)
            bs = stack_col(beta, DN_HEADS)
            grow = gcs.T[0:1, :]
            dec = jnp.exp(jnp.where(m_incl, gcs[:, 0:1] - grow, NEG_INF))
            kb = k * bs
            kbf = k.astype(BF16)
            eg = jnp.exp(gcs)
            af_s[c] = _dot_nt(kb.astype(BF16), kbf) * dec
            qk_b[c] = jnp.where(m_incl, _dot_nt(q.astype(BF16), kbf) * dec, 0.0).astype(BF16)
            rhs_s[c] = jnp.concatenate([v * bs, kb * eg], axis=1).astype(BF16)
            qg_b[c] = (q * eg).astype(BF16)
            kd_b[c] = (k * jnp.exp(gls - gcs)).astype(BF16)
            el_b[c] = jnp.exp(gls)
            yield

        for c in range(n):
            t_s[c] = eye - jnp.where(m_d2, af_s[c], 0.0)
        for m_low in m_merges:
            for c in range(n):
                pa_s[c] = _dot(jnp.where(m_low, af_s[c], 0.0).astype(BF16), t_s[c].astype(BF16)).astype(BF16)
            yield
            for c in range(n):
                t_s[c] = t_s[c] - _dot(t_s[c].astype(BF16), pa_s[c])
            yield
        for c in range(n):
            uw_b[c] = _dot(t_s[c].astype(BF16), rhs_s[c])

    def phase_b(zs, buf, outs):
        uw_b, qk_b, qg_b, kd_b, el_b = buf
        for c in range(len(zs)):
            vnew = []
            o_inter = []
            for h in range(DN_HEADS):
                r0 = h * C
                s_h = s_s[h]
                sb = s_h.astype(BF16)
                vn = uw_b[c, r0:r0 + C, 0:DN_DIM] - _dot(uw_b[c, r0:r0 + C, DN_DIM:].astype(BF16), sb)
                o_inter.append(_dot(qg_b[c, r0:r0 + C, :], sb))
                s_s[h] = s_h * el_b[c, r0:r0 + 1, :] + _dot_tn(kd_b[c, r0:r0 + C, :], vn.astype(BF16))
                vnew.append(vn)
                yield
            o = jnp.concatenate(o_inter, axis=0) + _dot(qk_b[c], jnp.concatenate(vnew, axis=0).astype(BF16))
            o = _rms(o, onw_ref[...])
            out = jnp.concatenate([o[h * C:(h + 1) * C, :] for h in range(DN_HEADS)], axis=1) * zs[c]
            outs.append(out.astype(BF16))
            yield

    def run(*gens_and_steps):
        gens = [g for g, _ in gens_and_steps]
        lens = [s for _, s in gens_and_steps]
        done = [0] * len(gens)
        alive = [True] * len(gens)
        while any(alive):
            i = min((j for j in range(len(gens)) if alive[j]), key=lambda j: (done[j] + 0.5) / lens[j])
            try:
                next(gens[i])
                done[i] += 1
            except StopIteration:
                alive[i] = False

    bufs = ((uw0_s, qk0_s, qg0_s, kd0_s, el0_s), (uw1_s, qk1_s, qg1_s, kd1_s, el1_s))

    @pl.when(pl.program_id(0) == 0)
    def _():
        meta_ok = lax.broadcasted_iota(jnp.int32, (C, 128), 0) >= META_PAD
        dn0 = dnm_ref[...]
        act0 = dn0[:, 0:3 * DN_WIDTH].astype(F32)
        gc0, beta0 = gates(abm_ref[...], meta_ok)
        run((phase_a([act0], [gc0], [beta0], bufs[1]), 1))
        out0 = []
        run((phase_b([dn0[:, 3 * DN_WIDTH:].astype(F32)], bufs[1], out0), 1))
        ymeta_s[...] = out0[0]
        smeta_s[...] = s_s[...]

    s_s[...] = smeta_s[...]
    ym_ref[0] = ymeta_s[...]

    rows = [slice(c * C, (c + 1) * C) for c in range(DN_GROUP)]
    a_steps = DN_GROUP + 2 + 2 * len(m_merges)
    b_steps = (DN_HEADS + 1) * DN_GROUP + 1

    def group_a(p, buf):
        r0 = p * DN_GROUP_ROWS
        if not isinstance(p, int):
            r0 = pl.multiple_of(r0, DN_GROUP_ROWS)
        act = dnx_ref[0, pl.ds(r0, DN_GROUP_ROWS), 0:3 * DN_WIDTH].astype(F32)
        gc, beta = gates(abx_ref[0, pl.ds(r0, DN_GROUP_ROWS), :], None)
        yield
        yield from phase_a([act[r, :] for r in rows], [gc[r, :] for r in rows], [beta[r, :] for r in rows], buf)

    def group_b(p, buf):
        r0 = p * DN_GROUP_ROWS
        if not isinstance(p, int):
            r0 = pl.multiple_of(r0, DN_GROUP_ROWS)
        z = dnx_ref[0, pl.ds(r0, DN_GROUP_ROWS), 3 * DN_WIDTH:].astype(F32)
        outs = []
        yield from phase_b([z[r, :] for r in rows], buf, outs)
        yx_ref[0, pl.ds(r0, DN_GROUP_ROWS), :] = jnp.concatenate(outs, axis=0)

    n_groups = SEQ // DN_GROUP_ROWS
    run((group_a(0, bufs[0]), a_steps))

    def body(kk, carry):
        p = 2 * kk
        run((group_a(p + 1, bufs[1]), a_steps), (group_b(p, bufs[0]), b_steps))
        run((group_a(p + 2, bufs[0]), a_steps), (group_b(p + 1, bufs[1]), b_steps))
        return carry

    lax.fori_loop(0, n_groups // 2 - 1, body, 0)
    run((group_a(n_groups - 1, bufs[1]), a_steps), (group_b(n_groups - 2, bufs[0]), b_steps))
    run((group_b(n_groups - 1, bufs[1]), b_steps))


def _deltanet(dnx, dnm, abx, abm, alog, dtb, onw):
    nb = dnx.shape[0]
    const = lambda shape: pl.BlockSpec(shape, lambda b: (0,) * len(shape))
    return pl.pallas_call(
        _deltanet_kernel,
        grid=(nb,),
        in_specs=[
            pl.BlockSpec((1, SEQ, N_DN), lambda b: (b, 0, 0)),
            const((META_ROWS, N_DN)),
            pl.BlockSpec((1, SEQ, N_AB), lambda b: (b, 0, 0)),
            const((META_ROWS, N_AB)),
            const((1, 128)), const((1, 128)), const((1, 128)),
        ],
        out_specs=[
            pl.BlockSpec((1, SEQ, DN_WIDTH), lambda b: (b, 0, 0)),
            pl.BlockSpec((1, META_ROWS, DN_WIDTH), lambda b: (b, 0, 0)),
        ],
        out_shape=[
            jax.ShapeDtypeStruct((nb, SEQ, DN_WIDTH), BF16),
            jax.ShapeDtypeStruct((nb, META_ROWS, DN_WIDTH), BF16),
        ],
        scratch_shapes=[
            pltpu.VMEM((DN_HEADS, DN_DIM, DN_DIM), F32),
            pltpu.VMEM((DN_GROUP, DN_STACK, DN_STACK), F32),
            pltpu.VMEM((DN_GROUP, DN_STACK, DN_STACK), F32),
            pltpu.VMEM((DN_GROUP, DN_STACK, DN_STACK), BF16),
            pltpu.VMEM((DN_GROUP, DN_STACK, 2 * DN_DIM), BF16),
        ] + 2 * [
            pltpu.VMEM((DN_GROUP, DN_STACK, 2 * DN_DIM), F32),
            pltpu.VMEM((DN_GROUP, DN_STACK, DN_STACK), BF16),
            pltpu.VMEM((DN_GROUP, DN_STACK, DN_DIM), BF16),
            pltpu.VMEM((DN_GROUP, DN_STACK, DN_DIM), BF16),
            pltpu.VMEM((DN_GROUP, DN_STACK, DN_DIM), F32),
        ] + [
            pltpu.VMEM((DN_HEADS, DN_DIM, DN_DIM), F32),
            pltpu.VMEM((META_ROWS, DN_WIDTH), BF16),
        ],
        compiler_params=_cparams(("arbitrary",)),
        name="deltanet",
    )(dnx, dnm, abx, abm, alog, dtb, onw)


FFN_ROWS = 1024
FFN_HALO = 16
N_FF_BLK = D_FF // FF_BLK


def _ffn_kernel(x_ref, xh_ref, mh_ref, ya_ref, yah_ref, yam_ref, yd_ref, ydh_ref, ydm_ref,
                wo_ref, nw_ref, wg_ref, wu_ref, cw_ref, cb_ref, wd_ref,
                o_ref, u_s, g0_s, g1_s, up0_s, up1_s, act_s):
    r = pl.program_id(1)
    mixed = jnp.concatenate([ya_ref[0], yd_ref[0]], axis=1)
    h_mid = x_ref[0] + _dot(mixed, wo_ref[...])
    o_ref[0] = h_mid
    u_s[FFN_HALO:, :] = _rms(h_mid, nw_ref[...]).astype(BF16)
    first = r == 0
    mixed_h = jnp.concatenate([jnp.where(first, yam_ref[0], yah_ref[0]),
                               jnp.where(first, ydm_ref[0], ydh_ref[0])], axis=1)
    h_halo = jnp.where(first, mh_ref[...], xh_ref[0]) + _dot(mixed_h, wo_ref[...])
    u_s[0:FFN_HALO, :] = _rms(h_halo, nw_ref[...]).astype(BF16)

    g_bufs = (g0_s, g1_s)
    up_bufs = (up0_s, up1_s)

    def project(f):
        cols = slice(f * FF_BLK, (f + 1) * FF_BLK)
        g_bufs[f % 2][...] = _dot(u_s[...], wg_ref[:, cols])
        up_bufs[f % 2][...] = _dot(u_s[FFN_HALO:, :], wu_ref[:, cols])

    project(0)
    for f in range(N_FF_BLK):
        if f + 1 < N_FF_BLK:
            project(f + 1)
        g_s = g_bufs[f % 2]
        cols = slice(f * FF_BLK, (f + 1) * FF_BLK)
        gate = (cw_ref[2:3, cols] * g_s[FFN_HALO:, :]
                + cw_ref[1:2, cols] * g_s[FFN_HALO - 1:FFN_HALO - 1 + FFN_ROWS, :]
                + cw_ref[0:1, cols] * g_s[FFN_HALO - 2:FFN_HALO - 2 + FFN_ROWS, :]
                + cb_ref[:, cols])
        act_s[:, cols] = (_silu(gate) * up_bufs[f % 2][...]).astype(BF16)
    o_ref[0] += _dot(act_s[...], wd_ref[...])


def _ffn(x, hp_meta, yax, yam, ydx, ydm, wo, nw, wg, wu, cw, cb, wd):
    nb = x.shape[0]
    nr = SEQ // FFN_ROWS
    hb = FFN_ROWS // FFN_HALO
    halo_idx = lambda b, r: (b, jnp.maximum(r * hb - 1, 0), 0)
    meta_idx = lambda b, r: (b, META_ROWS // FFN_HALO - 1, 0)
    main_idx = lambda b, r: (b, r, 0)
    resident = lambda shape: pl.BlockSpec(shape, lambda b, r: (0,) * len(shape),
                                          pipeline_mode=pl.Buffered(1))
    return pl.pallas_call(
        _ffn_kernel,
        grid=(nb, nr),
        in_specs=[
            pl.BlockSpec((1, FFN_ROWS, D_MODEL), main_idx),
            pl.BlockSpec((1, FFN_HALO, D_MODEL), halo_idx),
            pl.BlockSpec((FFN_HALO, D_MODEL), lambda b, r: (META_ROWS // FFN_HALO - 1, 0)),
            pl.BlockSpec((1, FFN_ROWS, MLA_HEADS * V_HEAD), main_idx),
            pl.BlockSpec((1, FFN_HALO, MLA_HEADS * V_HEAD), halo_idx),
            pl.BlockSpec((1, FFN_HALO, MLA_HEADS * V_HEAD), meta_idx),
            pl.BlockSpec((1, FFN_ROWS, DN_WIDTH), main_idx),
            pl.BlockSpec((1, FFN_HALO, DN_WIDTH), halo_idx),
            pl.BlockSpec((1, FFN_HALO, DN_WIDTH), meta_idx),
            resident((D_MODEL, D_MODEL)),
            resident((1, D_MODEL)),
            resident((D_MODEL, D_FF)),
            resident((D_MODEL, D_FF)),
            resident((3, D_FF)),
            resident((1, D_FF)),
            resident((D_FF, D_MODEL)),
        ],
        out_specs=pl.BlockSpec((1, FFN_ROWS, D_MODEL), main_idx),
        out_shape=jax.ShapeDtypeStruct((nb, SEQ, D_MODEL), F32),
        scratch_shapes=[
            pltpu.VMEM((FFN_HALO + FFN_ROWS, D_MODEL), BF16),
            pltpu.VMEM((FFN_HALO + FFN_ROWS, FF_BLK), F32),
            pltpu.VMEM((FFN_HALO + FFN_ROWS, FF_BLK), F32),
            pltpu.VMEM((FFN_ROWS, FF_BLK), F32),
            pltpu.VMEM((FFN_ROWS, FF_BLK), F32),
            pltpu.VMEM((FFN_ROWS, D_FF), BF16),
        ],
        compiler_params=_cparams(("arbitrary", "arbitrary")),
        name="outproj_ffn",
    )(x, x, hp_meta, yax, yax, yam, ydx, ydx, ydm, wo, nw, wg, wu, cw, cb, wd)


def _rot_cols(w):
    half = QK_ROPE // 2
    return jnp.concatenate([-w[..., half:], w[..., :half]], axis=-1)


def _swap_halves(w):
    half = QK_ROPE // 2
    return jnp.concatenate([w[..., half:], w[..., :half]], axis=-1)


def _pad_lanes(v, n=128):
    return jnp.pad(v.astype(F32), (0, n - v.shape[0])).reshape(1, n)


def _layer(x, hp_meta, l, attn_norm_w, w_in, q_a_norm_w, w_q_b, kv_a_norm_w, w_kv_b, q_norm_w,
           k_norm_w, mla_out_norm_w, dn_conv_w, dn_A_log, dn_dt_bias, dn_out_norm_w, w_out,
           ffn_norm_w, w_gate, w_up, ffn_conv_w, ffn_conv_b, w_down):
    nb = x.shape[0]
    c1 = Q_LORA
    c2 = c1 + KV_LORA
    c3 = c2 + QK_ROPE
    c4 = c3 + 3 * DN_WIDTH
    c5 = c4 + DN_WIDTH
    wint = w_in[l].T.astype(BF16)
    k_pe_w = wint[c2:c3, :]
    half = QK_ROPE // 2
    w1 = jnp.concatenate(
        [wint[:c2, :], k_pe_w, -k_pe_w[half:, :], k_pe_w[:half, :], wint[c3:c5, :], wint[c5:, :],
         jnp.zeros((N_AB - 2 * DN_HEADS, D_MODEL), BF16)], axis=0).T

    wqb = w_q_b[l].reshape(Q_LORA, MLA_HEADS, QK_HEAD)
    wqt = jnp.concatenate([wqb[..., :QK_NOPE], wqb[..., QK_NOPE:], _rot_cols(wqb[..., QK_NOPE:])],
                          axis=-1).reshape(Q_LORA, MLA_HEADS * HEAD_SLOT).T.astype(BF16)
    wkvb = w_kv_b[l].reshape(KV_LORA, MLA_HEADS, QK_NOPE + V_HEAD)
    wk = wkvb[..., :QK_NOPE].reshape(KV_LORA, MLA_HEADS * QK_NOPE).astype(BF16)
    wvt = wkvb[..., QK_NOPE:].reshape(KV_LORA, MLA_HEADS * V_HEAD).T.astype(BF16)
    qn = q_norm_w[l].astype(F32)
    kn = k_norm_w[l].astype(F32)
    qnn = qn[:QK_NOPE].reshape(QK_NOPE, 1)
    knn = kn[:QK_NOPE].reshape(1, 128)
    qrw = jnp.concatenate([qn[QK_NOPE:], _swap_halves(qn[QK_NOPE:])]).reshape(2 * QK_ROPE, 1)
    krw = jnp.concatenate([kn[QK_NOPE:], _swap_halves(kn[QK_NOPE:])]).reshape(1, 128)

    half = QK_ROPE // 2
    inv_freq = ROPE_THETA ** (-jnp.arange(half, dtype=F32) / half)
    pos = (jnp.arange(ATT_ROWS, dtype=jnp.int32) - ATT_META_VALID).astype(F32)
    ang = pos[:, None] * inv_freq[None, :]
    cs = jnp.concatenate([jnp.cos(ang), jnp.cos(ang), jnp.sin(ang), jnp.sin(ang)], axis=1)
    cst = cs.T

    nw1 = attn_norm_w[l].astype(F32).reshape(1, D_MODEL)
    qaw = q_a_norm_w[l].astype(F32).reshape(1, Q_LORA)
    kvaw = kv_a_norm_w[l].astype(F32).reshape(1, KV_LORA)
    cw = dn_conv_w[l].astype(F32)
    no_hist = jnp.zeros((CONV_HIST, 3 * DN_WIDTH), F32)
    latm, dnm, abm, meta_tail = _inproj(hp_meta, nw1, w1, qaw, kvaw, cw, no_hist, META_ROWS, 1)
    latx, dnx, abx, _ = _inproj(x.reshape(nb * SEQ, D_MODEL), nw1, w1, qaw, kvaw, cw, meta_tail,
                                INPROJ_ROWS, SEQ // INPROJ_ROWS)
    latx = latx.reshape(nb, SEQ, N_LAT)
    dnx = dnx.reshape(nb, SEQ, N_DN)
    abx = abx.reshape(nb, SEQ, N_AB)

    yax, yam, wo, wg, wu, wd = _mla(
        latx, latm, cs, cst, wqt, wk, wvt,
        qnn, qrw, knn, krw, mla_out_norm_w[l].astype(F32).reshape(V_HEAD, 1),
        [w_out[l].astype(F32), w_gate[l].astype(F32), w_up[l].astype(F32), w_down[l].astype(F32)])
    ydx, ydm = _deltanet(dnx, dnm, abx, abm, _pad_lanes(dn_A_log[l]), _pad_lanes(dn_dt_bias[l]),
                         dn_out_norm_w[l].astype(F32).reshape(1, DN_DIM))
    return _ffn(x, hp_meta, yax, yam, ydx, ydm, wo,
                ffn_norm_w[l].astype(F32).reshape(1, D_MODEL), wg, wu, ffn_conv_w[l].astype(F32),
                ffn_conv_b[l].astype(F32).reshape(1, D_FF), wd)


def kernel(x, meta_tokens, attn_norm_w, w_in, q_a_norm_w, w_q_b, kv_a_norm_w, w_kv_b, q_norm_w, k_norm_w, mla_out_norm_w, dn_conv_w, dn_A_log, dn_dt_bias, dn_out_norm_w, w_out, ffn_norm_w, w_gate, w_up, ffn_conv_w, ffn_conv_b, w_down):
    assert x.shape[1:] == (SEQ, D_MODEL) and w_in.shape[0] == 1
    hp_meta = jnp.concatenate([jnp.zeros((META_PAD, D_MODEL), x.dtype), meta_tokens.astype(x.dtype)], axis=0)
    return _layer(x, hp_meta, 0, attn_norm_w, w_in, q_a_norm_w, w_q_b, kv_a_norm_w, w_kv_b, q_norm_w,
                  k_norm_w, mla_out_norm_w, dn_conv_w, dn_A_log, dn_dt_bias, dn_out_norm_w, w_out,
                  ffn_norm_w, w_gate, w_up, ffn_conv_w, ffn_conv_b, w_down)
```

```python
import functools
import math

import jax
import jax.numpy as jnp
from jax import lax
from jax.experimental import pallas as pl
from jax.experimental.pallas import tpu as pltpu

F32 = jnp.float32
BF16 = jnp.bfloat16

D_MODEL = 1024
SEQ = 2048
N_META = 16
META_ROWS = 64
META_PAD = META_ROWS - N_META

MLA_HEADS = 4
QK_NOPE = 128
QK_ROPE = 64
QK_HEAD = QK_NOPE + QK_ROPE
V_HEAD = 128
Q_LORA = 256
KV_LORA = 256
ROPE_THETA = 10000.0
HEAD_SLOT = 256

DN_HEADS = 4
DN_DIM = 128
DN_WIDTH = DN_HEADS * DN_DIM
DN_CHUNK = 64
DN_STACK = DN_HEADS * DN_CHUNK
DN_CONV = 4

D_FF = 2816
FF_BLK = 256
NORM_EPS = 1e-6

N_LAT = Q_LORA + KV_LORA + 2 * QK_ROPE
C_DN = N_LAT
N_DN = 4 * DN_WIDTH
C_AB = C_DN + N_DN
N_AB = 128
N_PROJ = C_AB + N_AB

VMEM_LIMIT = 56 * 1024 * 1024


def _cparams(sem):
    return pltpu.CompilerParams(dimension_semantics=sem, vmem_limit_bytes=VMEM_LIMIT)


def _rms(x, w):
    return x * lax.rsqrt(jnp.mean(x * x, axis=-1, keepdims=True) + NORM_EPS) * w


def _dot(a, b):
    return jnp.dot(a, b, preferred_element_type=F32)


def _dot_nt(a, b):
    return lax.dot_general(a, b, (((1,), (1,)), ((), ())), preferred_element_type=F32)


def _dot_tn(a, b):
    return lax.dot_general(a, b, (((0,), (0,)), ((), ())), preferred_element_type=F32)


def _silu(x):
    return x * jax.nn.sigmoid(x)


assert DN_CONV == 4
CONV_HIST = 16
INPROJ_ROWS = 1024
INPROJ_SUB = 256


def _inproj_kernel(tiles_per_seq, x_ref, nw_ref, w_ref, qaw_ref, kvaw_ref, cw_ref, hist_in_ref,
                   lat_ref, dn_ref, ab_ref, tail_ref, hist_s, p0_s, p1_s):
    n = x_ref.shape[0]
    sub = min(INPROJ_SUB, n)
    p_bufs = (p0_s, p1_s)

    @pl.when(pl.program_id(0) % tiles_per_seq == 0)
    def _():
        hist_s[...] = hist_in_ref[...]

    def project(r):
        u = _rms(x_ref[r * sub:(r + 1) * sub, :], nw_ref[...]).astype(BF16)
        p_bufs[r % 2][0:sub, :] = _dot(u, w_ref[...])

    def post(r):
        p = p_bufs[r % 2]
        rows = slice(r * sub, (r + 1) * sub)
        lat_ref[rows, 0:Q_LORA] = _rms(p[0:sub, 0:Q_LORA], qaw_ref[...]).astype(BF16)
        lat_ref[rows, Q_LORA:Q_LORA + KV_LORA] = _rms(p[0:sub, Q_LORA:Q_LORA + KV_LORA],
                                                      kvaw_ref[...]).astype(BF16)
        lat_ref[rows, Q_LORA + KV_LORA:N_LAT] = p[0:sub, Q_LORA + KV_LORA:N_LAT].astype(BF16)
        ab_ref[rows, :] = p[0:sub, C_AB:C_AB + N_AB]
        pre = p[0:sub, C_DN:C_DN + 3 * DN_WIDTH]
        full = jnp.concatenate([hist_s[...], pre], axis=0)
        hist_s[...] = pre[sub - CONV_HIST:, :]
        full1 = pltpu.roll(full, 1, 0)
        near = cw_ref[3:4, :] * full + cw_ref[2:3, :] * full1
        far = cw_ref[1:2, :] * full + cw_ref[0:1, :] * full1
        act = _silu((near + pltpu.roll(far, 2, 0))[CONV_HIST:, :])
        for h in range(2 * DN_HEADS):
            a = act[:, h * DN_DIM:(h + 1) * DN_DIM]
            a = a * lax.rsqrt(jnp.sum(a * a, axis=-1, keepdims=True) + NORM_EPS)
            if h < DN_HEADS:
                a = a * (1.0 / math.sqrt(DN_DIM))
            dn_ref[rows, h * DN_DIM:(h + 1) * DN_DIM] = a.astype(BF16)
        dn_ref[rows, 2 * DN_WIDTH:3 * DN_WIDTH] = act[:, 2 * DN_WIDTH:].astype(BF16)
        dn_ref[rows, 3 * DN_WIDTH:] = _silu(p[0:sub, C_DN + 3 * DN_WIDTH:C_DN + N_DN]).astype(BF16)

    project(0)
    for r in range(n // sub):
        if r + 1 < n // sub:
            project(r + 1)
        post(r)
    tail_ref[...] = hist_s[...]


def _inproj(x2d, nw, w, qaw, kvaw, cw, hist_in, row_tile, tiles_per_seq):
    rows = x2d.shape[0]
    grid = (rows // row_tile,)
    const = lambda shape: pl.BlockSpec(shape, lambda i: (0,) * len(shape))
    return pl.pallas_call(
        functools.partial(_inproj_kernel, tiles_per_seq),
        grid=grid,
        in_specs=[
            pl.BlockSpec((row_tile, D_MODEL), lambda i: (i, 0)),
            const((1, D_MODEL)),
            const((D_MODEL, N_PROJ)),
            const((1, Q_LORA)),
            const((1, KV_LORA)),
            const((DN_CONV, 3 * DN_WIDTH)),
            const((CONV_HIST, 3 * DN_WIDTH)),
        ],
        out_specs=[
            pl.BlockSpec((row_tile, N_LAT), lambda i: (i, 0)),
            pl.BlockSpec((row_tile, N_DN), lambda i: (i, 0)),
            pl.BlockSpec((row_tile, N_AB), lambda i: (i, 0)),
            pl.BlockSpec((CONV_HIST, 3 * DN_WIDTH), lambda i: (i, 0)),
        ],
        out_shape=[
            jax.ShapeDtypeStruct((rows, N_LAT), BF16),
            jax.ShapeDtypeStruct((rows, N_DN), BF16),
            jax.ShapeDtypeStruct((rows, N_AB), F32),
            jax.ShapeDtypeStruct((grid[0] * CONV_HIST, 3 * DN_WIDTH), F32),
        ],
        scratch_shapes=[pltpu.VMEM((CONV_HIST, 3 * DN_WIDTH), F32),
                        pltpu.VMEM((min(INPROJ_SUB, row_tile), N_PROJ), F32),
                        pltpu.VMEM((min(INPROJ_SUB, row_tile), N_PROJ), F32)],
        compiler_params=_cparams(("arbitrary",)),
        name="inproj",
    )(x2d, nw, w, qaw, kvaw, cw, hist_in)


ATT_TQ = 256
V_SLOT = V_HEAD + 16
ATT_PROJ_ROWS = 512
ATT_META = 2 * META_ROWS
ATT_META_VALID = ATT_META - N_META
ATT_ROWS = ATT_META + SEQ
NEG_INF = float("-inf")


def _mla_kernel(n_w, latx_ref, latm_ref, cs_ref, cst_ref, wqt_ref, wk_ref, wvt_ref,
                qnn_ref, qrw_ref, knn_ref, krw_ref, onw_ref, *refs):
    w_f32_refs, (yx_ref, ym_ref), w_bf16_refs = refs[:n_w], refs[n_w:n_w + 2], refs[n_w + 2:2 * n_w + 2]
    km_s, qm_s, vm_s, kx_s, qx_s, vx_s, acc_s, stm_s, st0_s, st1_s, p_s = refs[2 * n_w + 2:]
    for w_f32, w_bf16 in zip(w_f32_refs, w_bf16_refs):
        w_bf16[...] = w_f32[...].astype(BF16)
    low = lax.broadcasted_iota(jnp.int32, (1, 128), 1) < QK_ROPE
    scale = 1.0 / math.sqrt(QK_HEAD)

    def project(lat, cs, cst):
        nrows = lat.shape[0]
        qn = lat[:, 0:Q_LORA]
        kvn = lat[:, Q_LORA:Q_LORA + KV_LORA]
        pe = lat[:, Q_LORA + KV_LORA:N_LAT]
        qt = _dot(wqt_ref[...], qn.T.astype(BF16))
        vt = _dot(wvt_ref[...], kvn.T.astype(BF16)).astype(BF16)
        kn = _dot(kvn.astype(BF16), wk_ref[...])
        a = pe * (cs * krw_ref[...])
        k_rope = jnp.where(low, a + pltpu.roll(a, QK_ROPE, 1), 0.0)
        pe_ss = jnp.sum(jnp.where(low, pe * pe, 0.0), axis=-1, keepdims=True)
        cos_t = cst[0:QK_ROPE, :]
        sin_t = cst[QK_ROPE:2 * QK_ROPE, :]
        k_parts = []
        q_parts = []
        for h in range(MLA_HEADS):
            nope = kn[:, h * QK_NOPE:(h + 1) * QK_NOPE]
            rs = lax.rsqrt((jnp.sum(nope * nope, axis=-1, keepdims=True) + pe_ss) * (1.0 / QK_HEAD) + NORM_EPS)
            k_parts += [(nope * rs * knn_ref[...]).astype(BF16), (k_rope * rs).astype(BF16)]
            r0 = h * HEAD_SLOT
            qnope = qt[r0:r0 + QK_NOPE, :]
            qrope = qt[r0 + QK_NOPE:r0 + QK_HEAD, :]
            qrot = qt[r0 + QK_HEAD:r0 + HEAD_SLOT, :]
            ssq = (jnp.sum(qnope * qnope, axis=0, keepdims=True)
                   + jnp.sum(qrope * qrope, axis=0, keepdims=True))
            rsq = lax.rsqrt(ssq * (1.0 / QK_HEAD) + NORM_EPS) * scale
            roped = (qrope * (qrw_ref[0:QK_ROPE, :] * cos_t)
                     + qrot * (qrw_ref[QK_ROPE:2 * QK_ROPE, :] * sin_t))
            q_parts += [(qnope * qnn_ref[...] * rsq).astype(BF16), (roped * rsq).astype(BF16),
                        jnp.zeros((HEAD_SLOT - QK_HEAD, nrows), BF16)]
        ones_row = (lax.broadcasted_iota(jnp.int32, (V_SLOT - V_HEAD, nrows), 0) == 0).astype(BF16)
        v_parts = []
        for h in range(MLA_HEADS):
            v_parts += [vt[h * V_HEAD:(h + 1) * V_HEAD, :], ones_row]
        return jnp.concatenate(k_parts, axis=1), jnp.concatenate(q_parts, axis=0), jnp.concatenate(v_parts, axis=0)

    latm = jnp.concatenate([jnp.zeros((META_ROWS, N_LAT), F32), latm_ref[...].astype(F32)], axis=0)
    km_s[...], qm_s[...], vm_s[...] = project(latm, cs_ref[0:ATT_META, :], cst_ref[:, 0:ATT_META])
    for c in range(SEQ // ATT_PROJ_ROWS):
        r0 = c * ATT_PROJ_ROWS
        k, qt, vt = project(latx_ref[0, r0:r0 + ATT_PROJ_ROWS, :].astype(F32),
                            cs_ref[ATT_META + r0:ATT_META + r0 + ATT_PROJ_ROWS, :],
                            cst_ref[:, ATT_META + r0:ATT_META + r0 + ATT_PROJ_ROWS])
        kx_s[r0:r0 + ATT_PROJ_ROWS, :] = k
        for t in range(ATT_PROJ_ROWS // ATT_TQ):
            qx_s[c * (ATT_PROJ_ROWS // ATT_TQ) + t] = qt[:, t * ATT_TQ:(t + 1) * ATT_TQ]
            vx_s[c * (ATT_PROJ_ROWS // ATT_TQ) + t] = vt[:, t * ATT_TQ:(t + 1) * ATT_TQ]

    def next_block(st, m):
        m_new = jnp.maximum(m, jnp.max(st, axis=0, keepdims=True))
        alpha = jnp.exp(m - m_new)
        p = jnp.exp((st - m_new).astype(BF16))
        return p, m_new, alpha

    def finish(acc_ext):
        o = acc_ext[0:V_HEAD, :] * (1.0 / acc_ext[V_HEAD:V_HEAD + 1, :])
        o = o * lax.rsqrt(jnp.mean(o * o, axis=0, keepdims=True) + NORM_EPS) * onw_ref[...]
        return o.T.astype(BF16)

    hs = lambda h: slice(h * HEAD_SLOT, (h + 1) * HEAD_SLOT)
    vs = lambda h: slice(h * V_SLOT, (h + 1) * V_SLOT)
    outs = lambda h: slice(h * V_HEAD, (h + 1) * V_HEAD)

    mkey = lax.broadcasted_iota(jnp.int32, (ATT_META, ATT_META), 0)
    mqry = lax.broadcasted_iota(jnp.int32, (ATT_META, ATT_META), 1)
    meta_mask = (mkey <= mqry) & ((mkey >= ATT_META_VALID) | (mkey == mqry))
    for h in range(MLA_HEADS):
        st = jnp.where(meta_mask, _dot(km_s[:, hs(h)], qm_s[hs(h), :]), NEG_INF)
        p, _, _ = next_block(st, jnp.full((1, ATT_META), NEG_INF, F32))
        o = finish(_dot(vm_s[vs(h), :], p))
        ym_ref[0, :, outs(h)] = o[META_ROWS:, :]

    meta_key_ok = lax.broadcasted_iota(jnp.int32, (ATT_META, ATT_TQ), 0) >= ATT_META_VALID
    diag_mask = (lax.broadcasted_iota(jnp.int32, (ATT_TQ, ATT_TQ), 0)
                 <= lax.broadcasted_iota(jnp.int32, (ATT_TQ, ATT_TQ), 1))

    def k_blk(j):
        k0 = j * ATT_TQ
        if not isinstance(j, int):
            k0 = pl.multiple_of(k0, ATT_TQ)
        return lambda h: kx_s[pl.ds(k0, ATT_TQ), hs(h)]

    v_blk = lambda j: (lambda h: vx_s[j, vs(h), :])

    acc_t, p_t, stm_t, st_t = acc_s, p_s, stm_s, (st0_s, st1_s)

    for i in range(SEQ // ATT_TQ):
        def scores(buf, k_of, nk):
            for h in range(MLA_HEADS):
                buf[h, 0:nk, :] = _dot(k_of(h), qx_s[i, hs(h), :])

        def absorb(buf, v_of, nk, mask, ms):
            ms2, alphas = [], []
            for h in range(MLA_HEADS):
                st = buf[h, 0:nk, :]
                if mask is not None:
                    st = jnp.where(mask, st, NEG_INF)
                p, m, alpha = next_block(st, ms[h])
                p_t[h, 0:nk, :] = p
                ms2.append(m)
                alphas.append(alpha)
            for h in range(MLA_HEADS):
                acc_t[h] = acc_t[h] * alphas[h] + _dot(v_of(h), p_t[h, 0:nk, :])
            return tuple(ms2)

        acc_t[...] = jnp.zeros_like(acc_t)
        scores(stm_t, lambda h: km_s[:, hs(h)], ATT_META)
        scores(st_t[0], k_blk(0), ATT_TQ)
        ms = (jnp.full((1, ATT_TQ), NEG_INF, F32),) * MLA_HEADS
        ms = absorb(stm_t, lambda h: vm_s[vs(h), :], ATT_META, meta_key_ok, ms)

        def pair(t, ms):
            scores(st_t[1], k_blk(2 * t + 1), ATT_TQ)
            ms = absorb(st_t[0], v_blk(2 * t), ATT_TQ, None, ms)
            scores(st_t[0], k_blk(2 * t + 2), ATT_TQ)
            return absorb(st_t[1], v_blk(2 * t + 1), ATT_TQ, None, ms)

        if i // 2 > 0:
            ms = lax.fori_loop(0, i // 2, pair, ms)
        if i % 2 == 1:
            scores(st_t[1], k_blk(i), ATT_TQ)
            ms = absorb(st_t[0], v_blk(i - 1), ATT_TQ, None, ms)
        absorb(st_t[i % 2], v_blk(i), ATT_TQ, diag_mask, ms)
        for h in range(MLA_HEADS):
            yx_ref[0, i * ATT_TQ:(i + 1) * ATT_TQ, outs(h)] = finish(acc_t[h])


def _mla(latx, latm, cs, cst, wqt, wk, wvt, qnn, qrw, knn, krw, onw, cast_weights):
    nb = latx.shape[0]
    const = lambda shape: pl.BlockSpec(shape, lambda b: (0,) * len(shape))
    row_slice = lambda w: pl.BlockSpec((w.shape[0] // nb, w.shape[1]), lambda b: (b, 0))
    nq = SEQ // ATT_TQ
    return pl.pallas_call(
        functools.partial(_mla_kernel, len(cast_weights)),
        grid=(nb,),
        in_specs=[
            pl.BlockSpec((1, SEQ, N_LAT), lambda b: (b, 0, 0)),
            const((META_ROWS, N_LAT)),
            const((ATT_ROWS, 128)),
            const((128, ATT_ROWS)),
            const((MLA_HEADS * HEAD_SLOT, Q_LORA)),
            const((KV_LORA, MLA_HEADS * QK_NOPE)),
            const((MLA_HEADS * V_HEAD, KV_LORA)),
            const((QK_NOPE, 1)), const((2 * QK_ROPE, 1)), const((1, 128)), const((1, 128)),
            const((V_HEAD, 1)),
        ] + [row_slice(w) for w in cast_weights],
        out_specs=[
            pl.BlockSpec((1, SEQ, MLA_HEADS * V_HEAD), lambda b: (b, 0, 0)),
            pl.BlockSpec((1, META_ROWS, MLA_HEADS * V_HEAD), lambda b: (b, 0, 0)),
        ] + [row_slice(w) for w in cast_weights],
        out_shape=[
            jax.ShapeDtypeStruct((nb, SEQ, MLA_HEADS * V_HEAD), BF16),
            jax.ShapeDtypeStruct((nb, META_ROWS, MLA_HEADS * V_HEAD), BF16),
        ] + [jax.ShapeDtypeStruct(w.shape, BF16) for w in cast_weights],
        scratch_shapes=[
            pltpu.VMEM((ATT_META, MLA_HEADS * HEAD_SLOT), BF16),
            pltpu.VMEM((MLA_HEADS * HEAD_SLOT, ATT_META), BF16),
            pltpu.VMEM((MLA_HEADS * V_SLOT, ATT_META), BF16),
            pltpu.VMEM((SEQ, MLA_HEADS * HEAD_SLOT), BF16),
            pltpu.VMEM((nq, MLA_HEADS * HEAD_SLOT, ATT_TQ), BF16),
            pltpu.VMEM((nq, MLA_HEADS * V_SLOT, ATT_TQ), BF16),
        ] + [
            pltpu.VMEM((MLA_HEADS, V_SLOT, ATT_TQ), F32),
            pltpu.VMEM((MLA_HEADS, ATT_META, ATT_TQ), F32),
            pltpu.VMEM((MLA_HEADS, ATT_TQ, ATT_TQ), F32),
            pltpu.VMEM((MLA_HEADS, ATT_TQ, ATT_TQ), F32),
            pltpu.VMEM((MLA_HEADS, ATT_TQ, ATT_TQ), BF16),
        ],
        compiler_params=_cparams(("arbitrary",)),
        name="mla",
    )(latx, latm, cs, cst, wqt, wk, wvt, qnn, qrw, knn, krw, onw, *cast_weights)


DN_GROUP = 4
DN_GROUP_ROWS = DN_GROUP * DN_CHUNK


def _deltanet_kernel(dnx_ref, dnm_ref, abx_ref, abm_ref, alog_ref, dtb_ref, onw_ref,
                     yx_ref, ym_ref, s_s, af_s, t_s, pa_s, rhs_s,
                     uw0_s, qk0_s, qg0_s, kd0_s, el0_s, uw1_s, qk1_s, qg1_s, kd1_s, el1_s,
                     smeta_s, ymeta_s):
    C = DN_CHUNK
    R = DN_STACK
    row = lax.broadcasted_iota(jnp.int32, (R, R), 0)
    col = lax.broadcasted_iota(jnp.int32, (R, R), 1)
    same = lambda sh: jnp.right_shift(row, sh) == jnp.right_shift(col, sh)
    m_incl = same(6) & (col <= row)
    m_strict = same(6) & (col < row)
    m_d2 = m_strict & same(1)
    m_merges = [m_strict & same(sh + 1) & jnp.logical_not(same(sh)) for sh in range(1, 6)]
    eye = (row == col).astype(F32)
    neg_a = -jnp.exp(alog_ref[...])

    s_s[...] = jnp.zeros_like(s_s)

    def stack(x):
        return jnp.concatenate([x[:, h * DN_DIM:(h + 1) * DN_DIM] for h in range(DN_HEADS)], axis=0)

    def stack_col(x, c0):
        return jnp.concatenate(
            [jnp.broadcast_to(x[:, c0 + h:c0 + h + 1], (C, DN_DIM)) for h in range(DN_HEADS)], axis=0)

    def gates(ab, row_ok):
        xa = ab + dtb_ref[...]
        softplus = jnp.maximum(xa, 0.0) + jnp.log(1.0 + jnp.exp(-jnp.abs(xa)))
        g = neg_a * softplus
        beta = jax.nn.sigmoid(ab)
        if row_ok is not None:
            g = jnp.where(row_ok, g, 0.0)
            beta = jnp.where(row_ok, beta, 0.0)
        pos = lax.broadcasted_iota(jnp.int32, g.shape, 0) & (C - 1)
        gc = g
        for s in (1, 2, 4, 8, 16, 32):
            gc = gc + jnp.where(pos >= s, pltpu.roll(gc, s, 0), 0.0)
        return gc, beta

    def phase_a(acts, gcl, betal, buf):
        uw_b, qk_b, qg_b, kd_b, el_b = buf
        n = len(acts)
        for c in range(n):
            act, gc, beta = acts[c], gcl[c], betal[c]
            q = stack(act[:, 0:DN_WIDTH])
            k = stack(act[:, DN_WIDTH:2 * DN_WIDTH])
            v = stack(act[:, 2 * DN_WIDTH:3 * DN_WIDTH])
            gcs = stack_col(gc, 0)
            gls = stack_col(jnp.broadcast_to(gc[C - 1:C, :], (C, 128)), 0)
            bs = stack_col(beta, DN_HEADS)
            grow = gcs.T[0:1, :]
            dec = jnp.exp(jnp.where(m_incl, gcs[:, 0:1] - grow, NEG_INF))
            kb = k * bs
            kbf = k.astype(BF16)
            eg = jnp.exp(gcs)
            af_s[c] = _dot_nt(kb.astype(BF16), kbf) * dec
            qk_b[c] = jnp.where(m_incl, _dot_nt(q.astype(BF16), kbf) * dec, 0.0).astype(BF16)
            rhs_s[c] = jnp.concatenate([v * bs, kb * eg], axis=1).astype(BF16)
            qg_b[c] = (q * eg).astype(BF16)
            kd_b[c] = (k * jnp.exp(gls - gcs)).astype(BF16)
            el_b[c] = jnp.exp(gls)
            yield

        for c in range(n):
            t_s[c] = eye - jnp.where(m_d2, af_s[c], 0.0)
        for m_low in m_merges:
            for c in range(n):
                pa_s[c] = _dot(jnp.where(m_low, af_s[c], 0.0).astype(BF16), t_s[c].astype(BF16)).astype(BF16)
            yield
            for c in range(n):
                t_s[c] = t_s[c] - _dot(t_s[c].astype(BF16), pa_s[c])
            yield
        for c in range(n):
            uw_b[c] = _dot(t_s[c].astype(BF16), rhs_s[c])

    def phase_b(zs, buf, outs):
        uw_b, qk_b, qg_b, kd_b, el_b = buf
        for c in range(len(zs)):
            vnew = []
            o_inter = []
            for h in range(DN_HEADS):
                r0 = h * C
                s_h = s_s[h]
                sb = s_h.astype(BF16)
                vn = uw_b[c, r0:r0 + C, 0:DN_DIM] - _dot(uw_b[c, r0:r0 + C, DN_DIM:].astype(BF16), sb)
                yield
                o_inter.append(_dot(qg_b[c, r0:r0 + C, :], sb))
                yield
                s_s[h] = s_h * el_b[c, r0:r0 + 1, :] + _dot_tn(kd_b[c, r0:r0 + C, :], vn.astype(BF16))
                vnew.append(vn)
                yield
            o = jnp.concatenate(o_inter, axis=0) + _dot(qk_b[c], jnp.concatenate(vnew, axis=0).astype(BF16))
            o = _rms(o, onw_ref[...])
            out = jnp.concatenate([o[h * C:(h + 1) * C, :] for h in range(DN_HEADS)], axis=1) * zs[c]
            outs.append(out.astype(BF16))
            yield

    def run(*gens_and_steps):
        gens = [g for g, _ in gens_and_steps]
        lens = [s for _, s in gens_and_steps]
        done = [0] * len(gens)
        alive = [True] * len(gens)
        while any(alive):
            i = min((j for j in range(len(gens)) if alive[j]), key=lambda j: (done[j] + 0.5) / lens[j])
            try:
                next(gens[i])
                done[i] += 1
            except StopIteration:
                alive[i] = False

    bufs = ((uw0_s, qk0_s, qg0_s, kd0_s, el0_s), (uw1_s, qk1_s, qg1_s, kd1_s, el1_s))

    @pl.when(pl.program_id(0) == 0)
    def _():
        meta_ok = lax.broadcasted_iota(jnp.int32, (C, 128), 0) >= META_PAD
        dn0 = dnm_ref[...]
        act0 = dn0[:, 0:3 * DN_WIDTH].astype(F32)
        gc0, beta0 = gates(abm_ref[...], meta_ok)
        run((phase_a([act0], [gc0], [beta0], bufs[1]), 1))
        out0 = []
        run((phase_b([dn0[:, 3 * DN_WIDTH:].astype(F32)], bufs[1], out0), 1))
        ymeta_s[...] = out0[0]
        smeta_s[...] = s_s[...]

    s_s[...] = smeta_s[...]
    ym_ref[0] = ymeta_s[...]

    rows = [slice(c * C, (c + 1) * C) for c in range(DN_GROUP)]
    a_steps = DN_GROUP + 2 + 2 * len(m_merges)
    b_steps = (3 * DN_HEADS + 1) * DN_GROUP + 1

    def group_a(p, buf):
        r0 = p * DN_GROUP_ROWS
        if not isinstance(p, int):
            r0 = pl.multiple_of(r0, DN_GROUP_ROWS)
        act = dnx_ref[0, pl.ds(r0, DN_GROUP_ROWS), 0:3 * DN_WIDTH].astype(F32)
        gc, beta = gates(abx_ref[0, pl.ds(r0, DN_GROUP_ROWS), :], None)
        yield
        yield from phase_a([act[r, :] for r in rows], [gc[r, :] for r in rows], [beta[r, :] for r in rows], buf)

    def group_b(p, buf):
        r0 = p * DN_GROUP_ROWS
        if not isinstance(p, int):
            r0 = pl.multiple_of(r0, DN_GROUP_ROWS)
        z = dnx_ref[0, pl.ds(r0, DN_GROUP_ROWS), 3 * DN_WIDTH:].astype(F32)
        outs = []
        yield from phase_b([z[r, :] for r in rows], buf, outs)
        yx_ref[0, pl.ds(r0, DN_GROUP_ROWS), :] = jnp.concatenate(outs, axis=0)

    n_groups = SEQ // DN_GROUP_ROWS
    run((group_a(0, bufs[0]), a_steps))

    def body(kk, carry):
        p = 2 * kk
        run((group_a(p + 1, bufs[1]), a_steps), (group_b(p, bufs[0]), b_steps))
        run((group_a(p + 2, bufs[0]), a_steps), (group_b(p + 1, bufs[1]), b_steps))
        return carry

    lax.fori_loop(0, n_groups // 2 - 1, body, 0)
    run((group_a(n_groups - 1, bufs[1]), a_steps), (group_b(n_groups - 2, bufs[0]), b_steps))
    run((group_b(n_groups - 1, bufs[1]), b_steps))


def _deltanet(dnx, dnm, abx, abm, alog, dtb, onw):
    nb = dnx.shape[0]
    const = lambda shape: pl.BlockSpec(shape, lambda b: (0,) * len(shape))
    return pl.pallas_call(
        _deltanet_kernel,
        grid=(nb,),
        in_specs=[
            pl.BlockSpec((1, SEQ, N_DN), lambda b: (b, 0, 0)),
            const((META_ROWS, N_DN)),
            pl.BlockSpec((1, SEQ, N_AB), lambda b: (b, 0, 0)),
            const((META_ROWS, N_AB)),
            const((1, 128)), const((1, 128)), const((1, 128)),
        ],
        out_specs=[
            pl.BlockSpec((1, SEQ, DN_WIDTH), lambda b: (b, 0, 0)),
            pl.BlockSpec((1, META_ROWS, DN_WIDTH), lambda b: (b, 0, 0)),
        ],
        out_shape=[
            jax.ShapeDtypeStruct((nb, SEQ, DN_WIDTH), BF16),
            jax.ShapeDtypeStruct((nb, META_ROWS, DN_WIDTH), BF16),
        ],
        scratch_shapes=[
            pltpu.VMEM((DN_HEADS, DN_DIM, DN_DIM), F32),
            pltpu.VMEM((DN_GROUP, DN_STACK, DN_STACK), F32),
            pltpu.VMEM((DN_GROUP, DN_STACK, DN_STACK), F32),
            pltpu.VMEM((DN_GROUP, DN_STACK, DN_STACK), BF16),
            pltpu.VMEM((DN_GROUP, DN_STACK, 2 * DN_DIM), BF16),
        ] + 2 * [
            pltpu.VMEM((DN_GROUP, DN_STACK, 2 * DN_DIM), F32),
            pltpu.VMEM((DN_GROUP, DN_STACK, DN_STACK), BF16),
            pltpu.VMEM((DN_GROUP, DN_STACK, DN_DIM), BF16),
            pltpu.VMEM((DN_GROUP, DN_STACK, DN_DIM), BF16),
            pltpu.VMEM((DN_GROUP, DN_STACK, DN_DIM), F32),
        ] + [
            pltpu.VMEM((DN_HEADS, DN_DIM, DN_DIM), F32),
            pltpu.VMEM((META_ROWS, DN_WIDTH), BF16),
        ],
        compiler_params=_cparams(("arbitrary",)),
        name="deltanet",
    )(dnx, dnm, abx, abm, alog, dtb, onw)


FFN_ROWS = 1024
FFN_HALO = 16
N_FF_BLK = D_FF // FF_BLK


def _ffn_kernel(x_ref, xh_ref, mh_ref, ya_ref, yah_ref, yam_ref, yd_ref, ydh_ref, ydm_ref,
                wo_ref, nw_ref, wg_ref, wu_ref, cw_ref, cb_ref, wd_ref,
                o_ref, u_s, g0_s, g1_s, up0_s, up1_s, act_s):
    r = pl.program_id(1)
    mixed = jnp.concatenate([ya_ref[0], yd_ref[0]], axis=1)
    h_mid = x_ref[0] + _dot(mixed, wo_ref[...])
    o_ref[0] = h_mid
    u_s[FFN_HALO:, :] = _rms(h_mid, nw_ref[...]).astype(BF16)
    first = r == 0
    mixed_h = jnp.concatenate([jnp.where(first, yam_ref[0], yah_ref[0]),
                               jnp.where(first, ydm_ref[0], ydh_ref[0])], axis=1)
    h_halo = jnp.where(first, mh_ref[...], xh_ref[0]) + _dot(mixed_h, wo_ref[...])
    u_s[0:FFN_HALO, :] = _rms(h_halo, nw_ref[...]).astype(BF16)

    g_bufs = (g0_s, g1_s)
    up_bufs = (up0_s, up1_s)

    def project(f):
        cols = slice(f * FF_BLK, (f + 1) * FF_BLK)
        g_bufs[f % 2][...] = _dot(u_s[...], wg_ref[:, cols])
        up_bufs[f % 2][...] = _dot(u_s[FFN_HALO:, :], wu_ref[:, cols])

    project(0)
    for f in range(N_FF_BLK):
        if f + 1 < N_FF_BLK:
            project(f + 1)
        g_s = g_bufs[f % 2]
        cols = slice(f * FF_BLK, (f + 1) * FF_BLK)
        gate = (cw_ref[2:3, cols] * g_s[FFN_HALO:, :]
                + cw_ref[1:2, cols] * g_s[FFN_HALO - 1:FFN_HALO - 1 + FFN_ROWS, :]
                + cw_ref[0:1, cols] * g_s[FFN_HALO - 2:FFN_HALO - 2 + FFN_ROWS, :]
                + cb_ref[:, cols])
        act_s[:, cols] = (_silu(gate) * up_bufs[f % 2][...]).astype(BF16)
    o_ref[0] += _dot(act_s[...], wd_ref[...])


def _ffn(x, hp_meta, yax, yam, ydx, ydm, wo, nw, wg, wu, cw, cb, wd):
    nb = x.shape[0]
    nr = SEQ // FFN_ROWS
    hb = FFN_ROWS // FFN_HALO
    halo_idx = lambda b, r: (b, jnp.maximum(r * hb - 1, 0), 0)
    meta_idx = lambda b, r: (b, META_ROWS // FFN_HALO - 1, 0)
    main_idx = lambda b, r: (b, r, 0)
    resident = lambda shape: pl.BlockSpec(shape, lambda b, r: (0,) * len(shape),
                                          pipeline_mode=pl.Buffered(1))
    return pl.pallas_call(
        _ffn_kernel,
        grid=(nb, nr),
        in_specs=[
            pl.BlockSpec((1, FFN_ROWS, D_MODEL), main_idx),
            pl.BlockSpec((1, FFN_HALO, D_MODEL), halo_idx),
            pl.BlockSpec((FFN_HALO, D_MODEL), lambda b, r: (META_ROWS // FFN_HALO - 1, 0)),
            pl.BlockSpec((1, FFN_ROWS, MLA_HEADS * V_HEAD), main_idx),
            pl.BlockSpec((1, FFN_HALO, MLA_HEADS * V_HEAD), halo_idx),
            pl.BlockSpec((1, FFN_HALO, MLA_HEADS * V_HEAD), meta_idx),
            pl.BlockSpec((1, FFN_ROWS, DN_WIDTH), main_idx),
            pl.BlockSpec((1, FFN_HALO, DN_WIDTH), halo_idx),
            pl.BlockSpec((1, FFN_HALO, DN_WIDTH), meta_idx),
            resident((D_MODEL, D_MODEL)),
            resident((1, D_MODEL)),
            resident((D_MODEL, D_FF)),
            resident((D_MODEL, D_FF)),
            resident((3, D_FF)),
            resident((1, D_FF)),
            resident((D_FF, D_MODEL)),
        ],
        out_specs=pl.BlockSpec((1, FFN_ROWS, D_MODEL), main_idx),
        out_shape=jax.ShapeDtypeStruct((nb, SEQ, D_MODEL), F32),
        scratch_shapes=[
            pltpu.VMEM((FFN_HALO + FFN_ROWS, D_MODEL), BF16),
            pltpu.VMEM((FFN_HALO + FFN_ROWS, FF_BLK), F32),
            pltpu.VMEM((FFN_HALO + FFN_ROWS, FF_BLK), F32),
            pltpu.VMEM((FFN_ROWS, FF_BLK), F32),
            pltpu.VMEM((FFN_ROWS, FF_BLK), F32),
            pltpu.VMEM((FFN_ROWS, D_FF), BF16),
        ],
        compiler_params=_cparams(("arbitrary", "arbitrary")),
        name="outproj_ffn",
    )(x, x, hp_meta, yax, yax, yam, ydx, ydx, ydm, wo, nw, wg, wu, cw, cb, wd)


def _rot_cols(w):
    half = QK_ROPE // 2
    return jnp.concatenate([-w[..., half:], w[..., :half]], axis=-1)


def _swap_halves(w):
    half = QK_ROPE // 2
    return jnp.concatenate([w[..., half:], w[..., :half]], axis=-1)


def _pad_lanes(v, n=128):
    return jnp.pad(v.astype(F32), (0, n - v.shape[0])).reshape(1, n)


def _layer(x, hp_meta, l, attn_norm_w, w_in, q_a_norm_w, w_q_b, kv_a_norm_w, w_kv_b, q_norm_w,
           k_norm_w, mla_out_norm_w, dn_conv_w, dn_A_log, dn_dt_bias, dn_out_norm_w, w_out,
           ffn_norm_w, w_gate, w_up, ffn_conv_w, ffn_conv_b, w_down):
    nb = x.shape[0]
    c1 = Q_LORA
    c2 = c1 + KV_LORA
    c3 = c2 + QK_ROPE
    c4 = c3 + 3 * DN_WIDTH
    c5 = c4 + DN_WIDTH
    wint = w_in[l].T.astype(BF16)
    k_pe_w = wint[c2:c3, :]
    half = QK_ROPE // 2
    w1 = jnp.concatenate(
        [wint[:c2, :], k_pe_w, -k_pe_w[half:, :], k_pe_w[:half, :], wint[c3:c5, :], wint[c5:, :],
         jnp.zeros((N_AB - 2 * DN_HEADS, D_MODEL), BF16)], axis=0).T

    wqb = w_q_b[l].reshape(Q_LORA, MLA_HEADS, QK_HEAD)
    wqt = jnp.concatenate([wqb[..., :QK_NOPE], wqb[..., QK_NOPE:], _rot_cols(wqb[..., QK_NOPE:])],
                          axis=-1).reshape(Q_LORA, MLA_HEADS * HEAD_SLOT).T.astype(BF16)
    wkvb = w_kv_b[l].reshape(KV_LORA, MLA_HEADS, QK_NOPE + V_HEAD)
    wk = wkvb[..., :QK_NOPE].reshape(KV_LORA, MLA_HEADS * QK_NOPE).astype(BF16)
    wvt = wkvb[..., QK_NOPE:].reshape(KV_LORA, MLA_HEADS * V_HEAD).T.astype(BF16)
    qn = q_norm_w[l].astype(F32)
    kn = k_norm_w[l].astype(F32)
    qnn = qn[:QK_NOPE].reshape(QK_NOPE, 1)
    knn = kn[:QK_NOPE].reshape(1, 128)
    qrw = jnp.concatenate([qn[QK_NOPE:], _swap_halves(qn[QK_NOPE:])]).reshape(2 * QK_ROPE, 1)
    krw = jnp.concatenate([kn[QK_NOPE:], _swap_halves(kn[QK_NOPE:])]).reshape(1, 128)

    half = QK_ROPE // 2
    inv_freq = ROPE_THETA ** (-jnp.arange(half, dtype=F32) / half)
    pos = (jnp.arange(ATT_ROWS, dtype=jnp.int32) - ATT_META_VALID).astype(F32)
    ang = pos[:, None] * inv_freq[None, :]
    cs = jnp.concatenate([jnp.cos(ang), jnp.cos(ang), jnp.sin(ang), jnp.sin(ang)], axis=1)
    cst = cs.T

    nw1 = attn_norm_w[l].astype(F32).reshape(1, D_MODEL)
    qaw = q_a_norm_w[l].astype(F32).reshape(1, Q_LORA)
    kvaw = kv_a_norm_w[l].astype(F32).reshape(1, KV_LORA)
    cw = dn_conv_w[l].astype(F32)
    no_hist = jnp.zeros((CONV_HIST, 3 * DN_WIDTH), F32)
    latm, dnm, abm, meta_tail = _inproj(hp_meta, nw1, w1, qaw, kvaw, cw, no_hist, META_ROWS, 1)
    latx, dnx, abx, _ = _inproj(x.reshape(nb * SEQ, D_MODEL), nw1, w1, qaw, kvaw, cw, meta_tail,
                                INPROJ_ROWS, SEQ // INPROJ_ROWS)
    latx = latx.reshape(nb, SEQ, N_LAT)
    dnx = dnx.reshape(nb, SEQ, N_DN)
    abx = abx.reshape(nb, SEQ, N_AB)

    yax, yam, wo, wg, wu, wd = _mla(
        latx, latm, cs, cst, wqt, wk, wvt,
        qnn, qrw, knn, krw, mla_out_norm_w[l].astype(F32).reshape(V_HEAD, 1),
        [w_out[l].astype(F32), w_gate[l].astype(F32), w_up[l].astype(F32), w_down[l].astype(F32)])
    ydx, ydm = _deltanet(dnx, dnm, abx, abm, _pad_lanes(dn_A_log[l]), _pad_lanes(dn_dt_bias[l]),
                         dn_out_norm_w[l].astype(F32).reshape(1, DN_DIM))
    return _ffn(x, hp_meta, yax, yam, ydx, ydm, wo,
                ffn_norm_w[l].astype(F32).reshape(1, D_MODEL), wg, wu, ffn_conv_w[l].astype(F32),
                ffn_conv_b[l].astype(F32).reshape(1, D_FF), wd)


def kernel(x, meta_tokens, attn_norm_w, w_in, q_a_norm_w, w_q_b, kv_a_norm_w, w_kv_b, q_norm_w, k_norm_w, mla_out_norm_w, dn_conv_w, dn_A_log, dn_dt_bias, dn_out_norm_w, w_out, ffn_norm_w, w_gate, w_up, ffn_conv_w, ffn_conv_b, w_down):
    assert x.shape[1:] == (SEQ, D_MODEL) and w_in.shape[0] == 1
    hp_meta = jnp.concatenate([jnp.zeros((META_PAD, D_MODEL), x.dtype), meta_tokens.astype(x.dtype)], axis=0)
    return _layer(x, hp_meta, 0, attn_norm_w, w_in, q_a_norm_w, w_q_b, kv_a_norm_w, w_kv_b, q_norm_w,
                  k_norm_w, mla_out_norm_w, dn_conv_w, dn_A_log, dn_dt_bias, dn_out_norm_w, w_out,
                  ffn_norm_w, w_gate, w_up, ffn_conv_w, ffn_conv_b, w_down)
```

```python
import functools
import math

import jax
import jax.numpy as jnp
from jax import lax
from jax.experimental import pallas as pl
from jax.experimental.pallas import tpu as pltpu

F32 = jnp.float32
BF16 = jnp.bfloat16

D_MODEL = 1024
SEQ = 2048
N_META = 16
META_ROWS = 64
META_PAD = META_ROWS - N_META

MLA_HEADS = 4
QK_NOPE = 128
QK_ROPE = 64
QK_HEAD = QK_NOPE + QK_ROPE
V_HEAD = 128
Q_LORA = 256
KV_LORA = 256
ROPE_THETA = 10000.0
HEAD_SLOT = 256

DN_HEADS = 4
DN_DIM = 128
DN_WIDTH = DN_HEADS * DN_DIM
DN_CHUNK = 64
DN_STACK = DN_HEADS * DN_CHUNK
DN_CONV = 4

D_FF = 2816
FF_BLK = 256
NORM_EPS = 1e-6

N_LAT = Q_LORA + KV_LORA + 2 * QK_ROPE
C_DN = N_LAT
N_DN = 4 * DN_WIDTH
C_AB = C_DN + N_DN
N_AB = 128
N_PROJ = C_AB + N_AB

VMEM_LIMIT = 56 * 1024 * 1024


def _cparams(sem):
    return pltpu.CompilerParams(dimension_semantics=sem, vmem_limit_bytes=VMEM_LIMIT)


def _rms(x, w):
    return x * lax.rsqrt(jnp.mean(x * x, axis=-1, keepdims=True) + NORM_EPS) * w


def _dot(a, b):
    return jnp.dot(a, b, preferred_element_type=F32)


def _dot_nt(a, b):
    return lax.dot_general(a, b, (((1,), (1,)), ((), ())), preferred_element_type=F32)


def _dot_tn(a, b):
    return lax.dot_general(a, b, (((0,), (0,)), ((), ())), preferred_element_type=F32)


def _silu(x):
    return x * jax.nn.sigmoid(x)


assert DN_CONV == 4
CONV_HIST = 16
INPROJ_ROWS = 1024
INPROJ_SUB = 256


def _inproj_kernel(tiles_per_seq, x_ref, nw_ref, w_ref, qaw_ref, kvaw_ref, cw_ref, hist_in_ref,
                   lat_ref, dn_ref, ab_ref, tail_ref, hist_s, p0_s, p1_s):
    n = x_ref.shape[0]
    sub = min(INPROJ_SUB, n)
    p_bufs = (p0_s, p1_s)

    @pl.when(pl.program_id(0) % tiles_per_seq == 0)
    def _():
        hist_s[...] = hist_in_ref[...]

    def project(r):
        u = _rms(x_ref[r * sub:(r + 1) * sub, :], nw_ref[...]).astype(BF16)
        p_bufs[r % 2][0:sub, :] = _dot(u, w_ref[...])

    def post(r):
        p = p_bufs[r % 2]
        rows = slice(r * sub, (r + 1) * sub)
        lat_ref[rows, 0:Q_LORA] = _rms(p[0:sub, 0:Q_LORA], qaw_ref[...]).astype(BF16)
        lat_ref[rows, Q_LORA:Q_LORA + KV_LORA] = _rms(p[0:sub, Q_LORA:Q_LORA + KV_LORA],
                                                      kvaw_ref[...]).astype(BF16)
        lat_ref[rows, Q_LORA + KV_LORA:N_LAT] = p[0:sub, Q_LORA + KV_LORA:N_LAT].astype(BF16)
        ab_ref[rows, :] = p[0:sub, C_AB:C_AB + N_AB]
        pre = p[0:sub, C_DN:C_DN + 3 * DN_WIDTH]
        full = jnp.concatenate([hist_s[...], pre], axis=0)
        hist_s[...] = pre[sub - CONV_HIST:, :]
        full1 = pltpu.roll(full, 1, 0)
        near = cw_ref[3:4, :] * full + cw_ref[2:3, :] * full1
        far = cw_ref[1:2, :] * full + cw_ref[0:1, :] * full1
        act = _silu((near + pltpu.roll(far, 2, 0))[CONV_HIST:, :])
        for h in range(2 * DN_HEADS):
            a = act[:, h * DN_DIM:(h + 1) * DN_DIM]
            a = a * lax.rsqrt(jnp.sum(a * a, axis=-1, keepdims=True) + NORM_EPS)
            if h < DN_HEADS:
                a = a * (1.0 / math.sqrt(DN_DIM))
            dn_ref[rows, h * DN_DIM:(h + 1) * DN_DIM] = a.astype(BF16)
        dn_ref[rows, 2 * DN_WIDTH:3 * DN_WIDTH] = act[:, 2 * DN_WIDTH:].astype(BF16)
        dn_ref[rows, 3 * DN_WIDTH:] = _silu(p[0:sub, C_DN + 3 * DN_WIDTH:C_DN + N_DN]).astype(BF16)

    project(0)
    for r in range(n // sub):
        if r + 1 < n // sub:
            project(r + 1)
        post(r)
    tail_ref[...] = hist_s[...]


def _inproj(x2d, nw, w, qaw, kvaw, cw, hist_in, row_tile, tiles_per_seq):
    rows = x2d.shape[0]
    grid = (rows // row_tile,)
    const = lambda shape: pl.BlockSpec(shape, lambda i: (0,) * len(shape), pipeline_mode=pl.Buffered(1))
    return pl.pallas_call(
        functools.partial(_inproj_kernel, tiles_per_seq),
        grid=grid,
        in_specs=[
            pl.BlockSpec((row_tile, D_MODEL), lambda i: (i, 0)),
            const((1, D_MODEL)),
            const((D_MODEL, N_PROJ)),
            const((1, Q_LORA)),
            const((1, KV_LORA)),
            const((DN_CONV, 3 * DN_WIDTH)),
            const((CONV_HIST, 3 * DN_WIDTH)),
        ],
        out_specs=[
            pl.BlockSpec((row_tile, N_LAT), lambda i: (i, 0)),
            pl.BlockSpec((row_tile, N_DN), lambda i: (i, 0)),
            pl.BlockSpec((row_tile, N_AB), lambda i: (i, 0)),
            pl.BlockSpec((CONV_HIST, 3 * DN_WIDTH), lambda i: (i, 0)),
        ],
        out_shape=[
            jax.ShapeDtypeStruct((rows, N_LAT), BF16),
            jax.ShapeDtypeStruct((rows, N_DN), BF16),
            jax.ShapeDtypeStruct((rows, N_AB), F32),
            jax.ShapeDtypeStruct((grid[0] * CONV_HIST, 3 * DN_WIDTH), F32),
        ],
        scratch_shapes=[pltpu.VMEM((CONV_HIST, 3 * DN_WIDTH), F32),
                        pltpu.VMEM((min(INPROJ_SUB, row_tile), N_PROJ), F32),
                        pltpu.VMEM((min(INPROJ_SUB, row_tile), N_PROJ), F32)],
        compiler_params=_cparams(("arbitrary",)),
        name="inproj",
    )(x2d, nw, w, qaw, kvaw, cw, hist_in)


ATT_TQ = 256
V_SLOT = V_HEAD + 16
ATT_PROJ_ROWS = 512
ATT_META = 2 * META_ROWS
ATT_META_VALID = ATT_META - N_META
ATT_ROWS = ATT_META + SEQ
NEG_INF = float("-inf")


def _mla_kernel(n_w, latx_ref, latm_ref, cs_ref, cst_ref, wqt_ref, wk_ref, wvt_ref,
                qnn_ref, qrw_ref, knn_ref, krw_ref, onw_ref, *refs):
    w_f32_refs, (yx_ref, ym_ref), w_bf16_refs = refs[:n_w], refs[n_w:n_w + 2], refs[n_w + 2:2 * n_w + 2]
    km_s, qm_s, vm_s, kx_s, qx_s, vx_s, acc_s, stm_s, st0_s, st1_s, p_s = refs[2 * n_w + 2:]
    for w_f32, w_bf16 in zip(w_f32_refs, w_bf16_refs):
        w_bf16[...] = w_f32[...].astype(BF16)
    low = lax.broadcasted_iota(jnp.int32, (1, 128), 1) < QK_ROPE
    scale = 1.0 / math.sqrt(QK_HEAD)

    def project(lat, cs, cst):
        nrows = lat.shape[0]
        qn = lat[:, 0:Q_LORA]
        kvn = lat[:, Q_LORA:Q_LORA + KV_LORA]
        pe = lat[:, Q_LORA + KV_LORA:N_LAT]
        qt = _dot(wqt_ref[...], qn.T.astype(BF16))
        vt = _dot(wvt_ref[...], kvn.T.astype(BF16)).astype(BF16)
        kn = _dot(kvn.astype(BF16), wk_ref[...])
        a = pe * (cs * krw_ref[...])
        k_rope = jnp.where(low, a + pltpu.roll(a, QK_ROPE, 1), 0.0)
        pe_ss = jnp.sum(jnp.where(low, pe * pe, 0.0), axis=-1, keepdims=True)
        cos_t = cst[0:QK_ROPE, :]
        sin_t = cst[QK_ROPE:2 * QK_ROPE, :]
        k_parts = []
        q_parts = []
        for h in range(MLA_HEADS):
            nope = kn[:, h * QK_NOPE:(h + 1) * QK_NOPE]
            rs = lax.rsqrt((jnp.sum(nope * nope, axis=-1, keepdims=True) + pe_ss) * (1.0 / QK_HEAD) + NORM_EPS)
            k_parts += [(nope * rs * knn_ref[...]).astype(BF16), (k_rope * rs).astype(BF16)]
            r0 = h * HEAD_SLOT
            qnope = qt[r0:r0 + QK_NOPE, :]
            qrope = qt[r0 + QK_NOPE:r0 + QK_HEAD, :]
            qrot = qt[r0 + QK_HEAD:r0 + HEAD_SLOT, :]
            ssq = (jnp.sum(qnope * qnope, axis=0, keepdims=True)
                   + jnp.sum(qrope * qrope, axis=0, keepdims=True))
            rsq = lax.rsqrt(ssq * (1.0 / QK_HEAD) + NORM_EPS) * scale
            roped = (qrope * (qrw_ref[0:QK_ROPE, :] * cos_t)
                     + qrot * (qrw_ref[QK_ROPE:2 * QK_ROPE, :] * sin_t))
            q_parts += [(qnope * qnn_ref[...] * rsq).astype(BF16), (roped * rsq).astype(BF16),
                        jnp.zeros((HEAD_SLOT - QK_HEAD, nrows), BF16)]
        ones_row = (lax.broadcasted_iota(jnp.int32, (V_SLOT - V_HEAD, nrows), 0) == 0).astype(BF16)
        v_parts = []
        for h in range(MLA_HEADS):
            v_parts += [vt[h * V_HEAD:(h + 1) * V_HEAD, :], ones_row]
        return jnp.concatenate(k_parts, axis=1), jnp.concatenate(q_parts, axis=0), jnp.concatenate(v_parts, axis=0)

    latm = jnp.concatenate([jnp.zeros((META_ROWS, N_LAT), F32), latm_ref[...].astype(F32)], axis=0)
    km_s[...], qm_s[...], vm_s[...] = project(latm, cs_ref[0:ATT_META, :], cst_ref[:, 0:ATT_META])
    for c in range(SEQ // ATT_PROJ_ROWS):
        r0 = c * ATT_PROJ_ROWS
        k, qt, vt = project(latx_ref[0, r0:r0 + ATT_PROJ_ROWS, :].astype(F32),
                            cs_ref[ATT_META + r0:ATT_META + r0 + ATT_PROJ_ROWS, :],
                            cst_ref[:, ATT_META + r0:ATT_META + r0 + ATT_PROJ_ROWS])
        kx_s[r0:r0 + ATT_PROJ_ROWS, :] = k
        for t in range(ATT_PROJ_ROWS // ATT_TQ):
            qx_s[c * (ATT_PROJ_ROWS // ATT_TQ) + t] = qt[:, t * ATT_TQ:(t + 1) * ATT_TQ]
            vx_s[c * (ATT_PROJ_ROWS // ATT_TQ) + t] = vt[:, t * ATT_TQ:(t + 1) * ATT_TQ]

    def next_block(st, m):
        m_new = jnp.maximum(m, jnp.max(st, axis=0, keepdims=True))
        alpha = jnp.exp(m - m_new)
        p = jnp.exp((st - m_new).astype(BF16))
        return p, m_new, alpha

    def finish(acc_ext):
        o = acc_ext[0:V_HEAD, :] * (1.0 / acc_ext[V_HEAD:V_HEAD + 1, :])
        o = o * lax.rsqrt(jnp.mean(o * o, axis=0, keepdims=True) + NORM_EPS) * onw_ref[...]
        return o.T.astype(BF16)

    hs = lambda h: slice(h * HEAD_SLOT, (h + 1) * HEAD_SLOT)
    vs = lambda h: slice(h * V_SLOT, (h + 1) * V_SLOT)
    outs = lambda h: slice(h * V_HEAD, (h + 1) * V_HEAD)

    mkey = lax.broadcasted_iota(jnp.int32, (ATT_META, ATT_META), 0)
    mqry = lax.broadcasted_iota(jnp.int32, (ATT_META, ATT_META), 1)
    meta_mask = (mkey <= mqry) & ((mkey >= ATT_META_VALID) | (mkey == mqry))
    for h in range(MLA_HEADS):
        st = jnp.where(meta_mask, _dot(km_s[:, hs(h)], qm_s[hs(h), :]), NEG_INF)
        p, _, _ = next_block(st, jnp.full((1, ATT_META), NEG_INF, F32))
        o = finish(_dot(vm_s[vs(h), :], p))
        ym_ref[0, :, outs(h)] = o[META_ROWS:, :]

    meta_key_ok = lax.broadcasted_iota(jnp.int32, (ATT_META, ATT_TQ), 0) >= ATT_META_VALID
    diag_mask = (lax.broadcasted_iota(jnp.int32, (ATT_TQ, ATT_TQ), 0)
                 <= lax.broadcasted_iota(jnp.int32, (ATT_TQ, ATT_TQ), 1))

    def k_blk(j):
        k0 = j * ATT_TQ
        if not isinstance(j, int):
            k0 = pl.multiple_of(k0, ATT_TQ)
        return lambda h: kx_s[pl.ds(k0, ATT_TQ), hs(h)]

    v_blk = lambda j: (lambda h: vx_s[j, vs(h), :])

    acc_t, p_t, stm_t, st_t = acc_s, p_s, stm_s, (st0_s, st1_s)

    for i in range(SEQ // ATT_TQ):
        def scores(buf, k_of, nk):
            for h in range(MLA_HEADS):
                buf[h, 0:nk, :] = _dot(k_of(h), qx_s[i, hs(h), :])

        def absorb(buf, v_of, nk, mask, ms):
            ms2, alphas = [], []
            for h in range(MLA_HEADS):
                st = buf[h, 0:nk, :]
                if mask is not None:
                    st = jnp.where(mask, st, NEG_INF)
                p, m, alpha = next_block(st, ms[h])
                p_t[h, 0:nk, :] = p
                ms2.append(m)
                alphas.append(alpha)
            for h in range(MLA_HEADS):
                acc_t[h] = acc_t[h] * alphas[h] + _dot(v_of(h), p_t[h, 0:nk, :])
            return tuple(ms2)

        acc_t[...] = jnp.zeros_like(acc_t)
        scores(stm_t, lambda h: km_s[:, hs(h)], ATT_META)
        scores(st_t[0], k_blk(0), ATT_TQ)
        ms = (jnp.full((1, ATT_TQ), NEG_INF, F32),) * MLA_HEADS
        ms = absorb(stm_t, lambda h: vm_s[vs(h), :], ATT_META, meta_key_ok, ms)

        def pair(t, ms):
            scores(st_t[1], k_blk(2 * t + 1), ATT_TQ)
            ms = absorb(st_t[0], v_blk(2 * t), ATT_TQ, None, ms)
            scores(st_t[0], k_blk(2 * t + 2), ATT_TQ)
            return absorb(st_t[1], v_blk(2 * t + 1), ATT_TQ, None, ms)

        if i // 2 > 0:
            ms = lax.fori_loop(0, i // 2, pair, ms)
        if i % 2 == 1:
            scores(st_t[1], k_blk(i), ATT_TQ)
            ms = absorb(st_t[0], v_blk(i - 1), ATT_TQ, None, ms)
        absorb(st_t[i % 2], v_blk(i), ATT_TQ, diag_mask, ms)
        for h in range(MLA_HEADS):
            yx_ref[0, i * ATT_TQ:(i + 1) * ATT_TQ, outs(h)] = finish(acc_t[h])


def _mla(latx, latm, cs, cst, wqt, wk, wvt, qnn, qrw, knn, krw, onw, cast_weights):
    nb = latx.shape[0]
    const = lambda shape: pl.BlockSpec(shape, lambda b: (0,) * len(shape), pipeline_mode=pl.Buffered(1))
    row_slice = lambda w: pl.BlockSpec((w.shape[0] // nb, w.shape[1]), lambda b: (b, 0))
    nq = SEQ // ATT_TQ
    return pl.pallas_call(
        functools.partial(_mla_kernel, len(cast_weights)),
        grid=(nb,),
        in_specs=[
            pl.BlockSpec((1, SEQ, N_LAT), lambda b: (b, 0, 0)),
            const((META_ROWS, N_LAT)),
            const((ATT_ROWS, 128)),
            const((128, ATT_ROWS)),
            const((MLA_HEADS * HEAD_SLOT, Q_LORA)),
            const((KV_LORA, MLA_HEADS * QK_NOPE)),
            const((MLA_HEADS * V_HEAD, KV_LORA)),
            const((QK_NOPE, 1)), const((2 * QK_ROPE, 1)), const((1, 128)), const((1, 128)),
            const((V_HEAD, 1)),
        ] + [row_slice(w) for w in cast_weights],
        out_specs=[
            pl.BlockSpec((1, SEQ, MLA_HEADS * V_HEAD), lambda b: (b, 0, 0)),
            pl.BlockSpec((1, META_ROWS, MLA_HEADS * V_HEAD), lambda b: (b, 0, 0)),
        ] + [row_slice(w) for w in cast_weights],
        out_shape=[
            jax.ShapeDtypeStruct((nb, SEQ, MLA_HEADS * V_HEAD), BF16),
            jax.ShapeDtypeStruct((nb, META_ROWS, MLA_HEADS * V_HEAD), BF16),
        ] + [jax.ShapeDtypeStruct(w.shape, BF16) for w in cast_weights],
        scratch_shapes=[
            pltpu.VMEM((ATT_META, MLA_HEADS * HEAD_SLOT), BF16),
            pltpu.VMEM((MLA_HEADS * HEAD_SLOT, ATT_META), BF16),
            pltpu.VMEM((MLA_HEADS * V_SLOT, ATT_META), BF16),
            pltpu.VMEM((SEQ, MLA_HEADS * HEAD_SLOT), BF16),
            pltpu.VMEM((nq, MLA_HEADS * HEAD_SLOT, ATT_TQ), BF16),
            pltpu.VMEM((nq, MLA_HEADS * V_SLOT, ATT_TQ), BF16),
        ] + [
            pltpu.VMEM((MLA_HEADS, V_SLOT, ATT_TQ), F32),
            pltpu.VMEM((MLA_HEADS, ATT_META, ATT_TQ), F32),
            pltpu.VMEM((MLA_HEADS, ATT_TQ, ATT_TQ), F32),
            pltpu.VMEM((MLA_HEADS, ATT_TQ, ATT_TQ), F32),
            pltpu.VMEM((MLA_HEADS, ATT_TQ, ATT_TQ), BF16),
        ],
        compiler_params=_cparams(("arbitrary",)),
        name="mla",
    )(latx, latm, cs, cst, wqt, wk, wvt, qnn, qrw, knn, krw, onw, *cast_weights)


DN_GROUP = 4
DN_GROUP_ROWS = DN_GROUP * DN_CHUNK


def _deltanet_kernel(dnx_ref, dnm_ref, abx_ref, abm_ref, alog_ref, dtb_ref, onw_ref,
                     yx_ref, ym_ref, s_s, af_s, t_s, pa_s, rhs_s,
                     uw0_s, qk0_s, qg0_s, kd0_s, el0_s, uw1_s, qk1_s, qg1_s, kd1_s, el1_s,
                     smeta_s, ymeta_s):
    C = DN_CHUNK
    R = DN_STACK
    row = lax.broadcasted_iota(jnp.int32, (R, R), 0)
    col = lax.broadcasted_iota(jnp.int32, (R, R), 1)
    same = lambda sh: jnp.right_shift(row, sh) == jnp.right_shift(col, sh)
    m_incl = same(6) & (col <= row)
    m_strict = same(6) & (col < row)
    m_d2 = m_strict & same(1)
    m_merges = [m_strict & same(sh + 1) & jnp.logical_not(same(sh)) for sh in range(1, 6)]
    eye = (row == col).astype(F32)
    neg_a = -jnp.exp(alog_ref[...])

    s_s[...] = jnp.zeros_like(s_s)

    def stack(x):
        return jnp.concatenate([x[:, h * DN_DIM:(h + 1) * DN_DIM] for h in range(DN_HEADS)], axis=0)

    def stack_col(x, c0):
        return jnp.concatenate(
            [jnp.broadcast_to(x[:, c0 + h:c0 + h + 1], (C, DN_DIM)) for h in range(DN_HEADS)], axis=0)

    def gates(ab, row_ok):
        xa = ab + dtb_ref[...]
        softplus = jnp.maximum(xa, 0.0) + jnp.log(1.0 + jnp.exp(-jnp.abs(xa)))
        g = neg_a * softplus
        beta = jax.nn.sigmoid(ab)
        if row_ok is not None:
            g = jnp.where(row_ok, g, 0.0)
            beta = jnp.where(row_ok, beta, 0.0)
        pos = lax.broadcasted_iota(jnp.int32, g.shape, 0) & (C - 1)
        gc = g
        for s in (1, 2, 4, 8, 16, 32):
            gc = gc + jnp.where(pos >= s, pltpu.roll(gc, s, 0), 0.0)
        return gc, beta

    def phase_a(acts, gcl, betal, buf):
        uw_b, qk_b, qg_b, kd_b, el_b = buf
        n = len(acts)
        for c in range(n):
            act, gc, beta = acts[c], gcl[c], betal[c]
            q = stack(act[:, 0:DN_WIDTH])
            k = stack(act[:, DN_WIDTH:2 * DN_WIDTH])
            v = stack(act[:, 2 * DN_WIDTH:3 * DN_WIDTH])
            gcs = stack_col(gc, 0)
            gls = stack_col(jnp.broadcast_to(gc[C - 1:C, :], (C, 128)), 0)
            bs = stack_col(beta, DN_HEADS)
            grow = gcs.T[0:1, :]
            dec = jnp.exp(jnp.where(m_incl, gcs[:, 0:1] - grow, NEG_INF))
            kb = k * bs
            kbf = k.astype(BF16)
            eg = jnp.exp(gcs)
            af_s[c] = _dot_nt(kb.astype(BF16), kbf) * dec
            qk_b[c] = jnp.where(m_incl, _dot_nt(q.astype(BF16), kbf) * dec, 0.0).astype(BF16)
            rhs_s[c] = jnp.concatenate([v * bs, kb * eg], axis=1).astype(BF16)
            qg_b[c] = (q * eg).astype(BF16)
            kd_b[c] = (k * jnp.exp(gls - gcs)).astype(BF16)
            el_b[c] = jnp.exp(gls)
            yield

        for c in range(n):
            t_s[c] = eye - jnp.where(m_d2, af_s[c], 0.0)
        for m_low in m_merges:
            for c in range(n):
                pa_s[c] = _dot(jnp.where(m_low, af_s[c], 0.0).astype(BF16), t_s[c].astype(BF16)).astype(BF16)
            yield
            for c in range(n):
                t_s[c] = t_s[c] - _dot(t_s[c].astype(BF16), pa_s[c])
            yield
        for c in range(n):
            uw_b[c] = _dot(t_s[c].astype(BF16), rhs_s[c])

    def phase_b(zs, buf, outs):
        uw_b, qk_b, qg_b, kd_b, el_b = buf
        for c in range(len(zs)):
            vnew = []
            o_inter = []
            for h in range(DN_HEADS):
                r0 = h * C
                s_h = s_s[h]
                sb = s_h.astype(BF16)
                vn = uw_b[c, r0:r0 + C, 0:DN_DIM] - _dot(uw_b[c, r0:r0 + C, DN_DIM:].astype(BF16), sb)
                yield
                o_inter.append(_dot(qg_b[c, r0:r0 + C, :], sb))
                yield
                s_s[h] = s_h * el_b[c, r0:r0 + 1, :] + _dot_tn(kd_b[c, r0:r0 + C, :], vn.astype(BF16))
                vnew.append(vn)
                yield
            o = jnp.concatenate(o_inter, axis=0) + _dot(qk_b[c], jnp.concatenate(vnew, axis=0).astype(BF16))
            o = _rms(o, onw_ref[...])
            out = jnp.concatenate([o[h * C:(h + 1) * C, :] for h in range(DN_HEADS)], axis=1) * zs[c]
            outs.append(out.astype(BF16))
            yield

    def run(*gens_and_steps):
        gens = [g for g, _ in gens_and_steps]
        lens = [s for _, s in gens_and_steps]
        done = [0] * len(gens)
        alive = [True] * len(gens)
        while any(alive):
            i = min((j for j in range(len(gens)) if alive[j]), key=lambda j: (done[j] + 0.5) / lens[j])
            try:
                next(gens[i])
                done[i] += 1
            except StopIteration:
                alive[i] = False

    bufs = ((uw0_s, qk0_s, qg0_s, kd0_s, el0_s), (uw1_s, qk1_s, qg1_s, kd1_s, el1_s))

    @pl.when(pl.program_id(0) == 0)
    def _():
        meta_ok = lax.broadcasted_iota(jnp.int32, (C, 128), 0) >= META_PAD
        dn0 = dnm_ref[...]
        act0 = dn0[:, 0:3 * DN_WIDTH].astype(F32)
        gc0, beta0 = gates(abm_ref[...], meta_ok)
        run((phase_a([act0], [gc0], [beta0], bufs[1]), 1))
        out0 = []
        run((phase_b([dn0[:, 3 * DN_WIDTH:].astype(F32)], bufs[1], out0), 1))
        ymeta_s[...] = out0[0]
        smeta_s[...] = s_s[...]

    s_s[...] = smeta_s[...]
    ym_ref[0] = ymeta_s[...]

    rows = [slice(c * C, (c + 1) * C) for c in range(DN_GROUP)]
    a_steps = DN_GROUP + 2 + 2 * len(m_merges)
    b_steps = (3 * DN_HEADS + 1) * DN_GROUP + 1

    def group_a(p, buf):
        r0 = p * DN_GROUP_ROWS
        if not isinstance(p, int):
            r0 = pl.multiple_of(r0, DN_GROUP_ROWS)
        act = dnx_ref[0, pl.ds(r0, DN_GROUP_ROWS), 0:3 * DN_WIDTH].astype(F32)
        gc, beta = gates(abx_ref[0, pl.ds(r0, DN_GROUP_ROWS), :], None)
        yield
        yield from phase_a([act[r, :] for r in rows], [gc[r, :] for r in rows], [beta[r, :] for r in rows], buf)

    def group_b(p, buf):
        r0 = p * DN_GROUP_ROWS
        if not isinstance(p, int):
            r0 = pl.multiple_of(r0, DN_GROUP_ROWS)
        z = dnx_ref[0, pl.ds(r0, DN_GROUP_ROWS), 3 * DN_WIDTH:].astype(F32)
        outs = []
        yield from phase_b([z[r, :] for r in rows], buf, outs)
        yx_ref[0, pl.ds(r0, DN_GROUP_ROWS), :] = jnp.concatenate(outs, axis=0)

    n_groups = SEQ // DN_GROUP_ROWS
    run((group_a(0, bufs[0]), a_steps))

    def body(kk, carry):
        p = 2 * kk
        run((group_a(p + 1, bufs[1]), a_steps), (group_b(p, bufs[0]), b_steps))
        run((group_a(p + 2, bufs[0]), a_steps), (group_b(p + 1, bufs[1]), b_steps))
        return carry

    lax.fori_loop(0, n_groups // 2 - 1, body, 0)
    run((group_a(n_groups - 1, bufs[1]), a_steps), (group_b(n_groups - 2, bufs[0]), b_steps))
    run((group_b(n_groups - 1, bufs[1]), b_steps))


def _deltanet(dnx, dnm, abx, abm, alog, dtb, onw):
    nb = dnx.shape[0]
    const = lambda shape: pl.BlockSpec(shape, lambda b: (0,) * len(shape), pipeline_mode=pl.Buffered(1))
    return pl.pallas_call(
        _deltanet_kernel,
        grid=(nb,),
        in_specs=[
            pl.BlockSpec((1, SEQ, N_DN), lambda b: (b, 0, 0)),
            const((META_ROWS, N_DN)),
            pl.BlockSpec((1, SEQ, N_AB), lambda b: (b, 0, 0)),
            const((META_ROWS, N_AB)),
            const((1, 128)), const((1, 128)), const((1, 128)),
        ],
        out_specs=[
            pl.BlockSpec((1, SEQ, DN_WIDTH), lambda b: (b, 0, 0)),
            pl.BlockSpec((1, META_ROWS, DN_WIDTH), lambda b: (b, 0, 0)),
        ],
        out_shape=[
            jax.ShapeDtypeStruct((nb, SEQ, DN_WIDTH), BF16),
            jax.ShapeDtypeStruct((nb, META_ROWS, DN_WIDTH), BF16),
        ],
        scratch_shapes=[
            pltpu.VMEM((DN_HEADS, DN_DIM, DN_DIM), F32),
            pltpu.VMEM((DN_GROUP, DN_STACK, DN_STACK), F32),
            pltpu.VMEM((DN_GROUP, DN_STACK, DN_STACK), F32),
            pltpu.VMEM((DN_GROUP, DN_STACK, DN_STACK), BF16),
            pltpu.VMEM((DN_GROUP, DN_STACK, 2 * DN_DIM), BF16),
        ] + 2 * [
            pltpu.VMEM((DN_GROUP, DN_STACK, 2 * DN_DIM), F32),
            pltpu.VMEM((DN_GROUP, DN_STACK, DN_STACK), BF16),
            pltpu.VMEM((DN_GROUP, DN_STACK, DN_DIM), BF16),
            pltpu.VMEM((DN_GROUP, DN_STACK, DN_DIM), BF16),
            pltpu.VMEM((DN_GROUP, DN_STACK, DN_DIM), F32),
        ] + [
            pltpu.VMEM((DN_HEADS, DN_DIM, DN_DIM), F32),
            pltpu.VMEM((META_ROWS, DN_WIDTH), BF16),
        ],
        compiler_params=_cparams(("arbitrary",)),
        name="deltanet",
    )(dnx, dnm, abx, abm, alog, dtb, onw)


FFN_ROWS = 1024
FFN_HALO = 16
N_FF_BLK = D_FF // FF_BLK


def _ffn_kernel(x_ref, xh_ref, mh_ref, ya_ref, yah_ref, yam_ref, yd_ref, ydh_ref, ydm_ref,
                wo_ref, nw_ref, wg_ref, wu_ref, cw_ref, cb_ref, wd_ref,
                o_ref, u_s, g0_s, g1_s, up0_s, up1_s, act_s):
    r = pl.program_id(1)
    mixed = jnp.concatenate([ya_ref[0], yd_ref[0]], axis=1)
    h_mid = x_ref[0] + _dot(mixed, wo_ref[...])
    o_ref[0] = h_mid
    u_s[FFN_HALO:, :] = _rms(h_mid, nw_ref[...]).astype(BF16)
    first = r == 0
    mixed_h = jnp.concatenate([jnp.where(first, yam_ref[0], yah_ref[0]),
                               jnp.where(first, ydm_ref[0], ydh_ref[0])], axis=1)
    h_halo = jnp.where(first, mh_ref[...], xh_ref[0]) + _dot(mixed_h, wo_ref[...])
    u_s[0:FFN_HALO, :] = _rms(h_halo, nw_ref[...]).astype(BF16)

    g_bufs = (g0_s, g1_s)
    up_bufs = (up0_s, up1_s)

    def project(f):
        cols = slice(f * FF_BLK, (f + 1) * FF_BLK)
        g_bufs[f % 2][...] = _dot(u_s[...], wg_ref[:, cols])
        up_bufs[f % 2][...] = _dot(u_s[FFN_HALO:, :], wu_ref[:, cols])

    project(0)
    for f in range(N_FF_BLK):
        if f + 1 < N_FF_BLK:
            project(f + 1)
        g_s = g_bufs[f % 2]
        cols = slice(f * FF_BLK, (f + 1) * FF_BLK)
        gate = (cw_ref[2:3, cols] * g_s[FFN_HALO:, :]
                + cw_ref[1:2, cols] * g_s[FFN_HALO - 1:FFN_HALO - 1 + FFN_ROWS, :]
                + cw_ref[0:1, cols] * g_s[FFN_HALO - 2:FFN_HALO - 2 + FFN_ROWS, :]
                + cb_ref[:, cols])
        act_s[:, cols] = (_silu(gate) * up_bufs[f % 2][...]).astype(BF16)
    o_ref[0] += _dot(act_s[...], wd_ref[...])


def _ffn(x, hp_meta, yax, yam, ydx, ydm, wo, nw, wg, wu, cw, cb, wd):
    nb = x.shape[0]
    nr = SEQ // FFN_ROWS
    hb = FFN_ROWS // FFN_HALO
    halo_idx = lambda b, r: (b, jnp.maximum(r * hb - 1, 0), 0)
    meta_idx = lambda b, r: (b, META_ROWS // FFN_HALO - 1, 0)
    main_idx = lambda b, r: (b, r, 0)
    resident = lambda shape: pl.BlockSpec(shape, lambda b, r: (0,) * len(shape),
                                          pipeline_mode=pl.Buffered(1))
    return pl.pallas_call(
        _ffn_kernel,
        grid=(nb, nr),
        in_specs=[
            pl.BlockSpec((1, FFN_ROWS, D_MODEL), main_idx),
            pl.BlockSpec((1, FFN_HALO, D_MODEL), halo_idx),
            pl.BlockSpec((FFN_HALO, D_MODEL), lambda b, r: (META_ROWS // FFN_HALO - 1, 0)),
            pl.BlockSpec((1, FFN_ROWS, MLA_HEADS * V_HEAD), main_idx),
            pl.BlockSpec((1, FFN_HALO, MLA_HEADS * V_HEAD), halo_idx),
            pl.BlockSpec((1, FFN_HALO, MLA_HEADS * V_HEAD), meta_idx),
            pl.BlockSpec((1, FFN_ROWS, DN_WIDTH), main_idx),
            pl.BlockSpec((1, FFN_HALO, DN_WIDTH), halo_idx),
            pl.BlockSpec((1, FFN_HALO, DN_WIDTH), meta_idx),
            resident((D_MODEL, D_MODEL)),
            resident((1, D_MODEL)),
            resident((D_MODEL, D_FF)),
            resident((D_MODEL, D_FF)),
            resident((3, D_FF)),
            resident((1, D_FF)),
            resident((D_FF, D_MODEL)),
        ],
        out_specs=pl.BlockSpec((1, FFN_ROWS, D_MODEL), main_idx),
        out_shape=jax.ShapeDtypeStruct((nb, SEQ, D_MODEL), F32),
        scratch_shapes=[
            pltpu.VMEM((FFN_HALO + FFN_ROWS, D_MODEL), BF16),
            pltpu.VMEM((FFN_HALO + FFN_ROWS, FF_BLK), F32),
            pltpu.VMEM((FFN_HALO + FFN_ROWS, FF_BLK), F32),
            pltpu.VMEM((FFN_ROWS, FF_BLK), F32),
            pltpu.VMEM((FFN_ROWS, FF_BLK), F32),
            pltpu.VMEM((FFN_ROWS, D_FF), BF16),
        ],
        compiler_params=_cparams(("arbitrary", "arbitrary")),
        name="outproj_ffn",
    )(x, x, hp_meta, yax, yax, yam, ydx, ydx, ydm, wo, nw, wg, wu, cw, cb, wd)


def _rot_cols(w):
    half = QK_ROPE // 2
    return jnp.concatenate([-w[..., half:], w[..., :half]], axis=-1)


def _swap_halves(w):
    half = QK_ROPE // 2
    return jnp.concatenate([w[..., half:], w[..., :half]], axis=-1)


def _pad_lanes(v, n=128):
    return jnp.pad(v.astype(F32), (0, n - v.shape[0])).reshape(1, n)


def _layer(x, hp_meta, l, attn_norm_w, w_in, q_a_norm_w, w_q_b, kv_a_norm_w, w_kv_b, q_norm_w,
           k_norm_w, mla_out_norm_w, dn_conv_w, dn_A_log, dn_dt_bias, dn_out_norm_w, w_out,
           ffn_norm_w, w_gate, w_up, ffn_conv_w, ffn_conv_b, w_down):
    nb = x.shape[0]
    c1 = Q_LORA
    c2 = c1 + KV_LORA
    c3 = c2 + QK_ROPE
    c4 = c3 + 3 * DN_WIDTH
    c5 = c4 + DN_WIDTH
    wint = w_in[l].T.astype(BF16)
    k_pe_w = wint[c2:c3, :]
    half = QK_ROPE // 2
    w1 = jnp.concatenate(
        [wint[:c2, :], k_pe_w, -k_pe_w[half:, :], k_pe_w[:half, :], wint[c3:c5, :], wint[c5:, :],
         jnp.zeros((N_AB - 2 * DN_HEADS, D_MODEL), BF16)], axis=0).T

    wqb = w_q_b[l].reshape(Q_LORA, MLA_HEADS, QK_HEAD)
    wqt = jnp.concatenate([wqb[..., :QK_NOPE], wqb[..., QK_NOPE:], _rot_cols(wqb[..., QK_NOPE:])],
                          axis=-1).reshape(Q_LORA, MLA_HEADS * HEAD_SLOT).T.astype(BF16)
    wkvb = w_kv_b[l].reshape(KV_LORA, MLA_HEADS, QK_NOPE + V_HEAD)
    wk = wkvb[..., :QK_NOPE].reshape(KV_LORA, MLA_HEADS * QK_NOPE).astype(BF16)
    wvt = wkvb[..., QK_NOPE:].reshape(KV_LORA, MLA_HEADS * V_HEAD).T.astype(BF16)
    qn = q_norm_w[l].astype(F32)
    kn = k_norm_w[l].astype(F32)
    qnn = qn[:QK_NOPE].reshape(QK_NOPE, 1)
    knn = kn[:QK_NOPE].reshape(1, 128)
    qrw = jnp.concatenate([qn[QK_NOPE:], _swap_halves(qn[QK_NOPE:])]).reshape(2 * QK_ROPE, 1)
    krw = jnp.concatenate([kn[QK_NOPE:], _swap_halves(kn[QK_NOPE:])]).reshape(1, 128)

    half = QK_ROPE // 2
    inv_freq = ROPE_THETA ** (-jnp.arange(half, dtype=F32) / half)
    pos = (jnp.arange(ATT_ROWS, dtype=jnp.int32) - ATT_META_VALID).astype(F32)
    ang = pos[:, None] * inv_freq[None, :]
    cs = jnp.concatenate([jnp.cos(ang), jnp.cos(ang), jnp.sin(ang), jnp.sin(ang)], axis=1)
    cst = cs.T

    nw1 = attn_norm_w[l].astype(F32).reshape(1, D_MODEL)
    qaw = q_a_norm_w[l].astype(F32).reshape(1, Q_LORA)
    kvaw = kv_a_norm_w[l].astype(F32).reshape(1, KV_LORA)
    cw = dn_conv_w[l].astype(F32)
    no_hist = jnp.zeros((CONV_HIST, 3 * DN_WIDTH), F32)
    latm, dnm, abm, meta_tail = _inproj(hp_meta, nw1, w1, qaw, kvaw, cw, no_hist, META_ROWS, 1)
    latx, dnx, abx, _ = _inproj(x.reshape(nb * SEQ, D_MODEL), nw1, w1, qaw, kvaw, cw, meta_tail,
                                INPROJ_ROWS, SEQ // INPROJ_ROWS)
    latx = latx.reshape(nb, SEQ, N_LAT)
    dnx = dnx.reshape(nb, SEQ, N_DN)
    abx = abx.reshape(nb, SEQ, N_AB)

    yax, yam, wo, wg, wu, wd = _mla(
        latx, latm, cs, cst, wqt, wk, wvt,
        qnn, qrw, knn, krw, mla_out_norm_w[l].astype(F32).reshape(V_HEAD, 1),
        [w_out[l].astype(F32), w_gate[l].astype(F32), w_up[l].astype(F32), w_down[l].astype(F32)])
    ydx, ydm = _deltanet(dnx, dnm, abx, abm, _pad_lanes(dn_A_log[l]), _pad_lanes(dn_dt_bias[l]),
                         dn_out_norm_w[l].astype(F32).reshape(1, DN_DIM))
    return _ffn(x, hp_meta, yax, yam, ydx, ydm, wo,
                ffn_norm_w[l].astype(F32).reshape(1, D_MODEL), wg, wu, ffn_conv_w[l].astype(F32),
                ffn_conv_b[l].astype(F32).reshape(1, D_FF), wd)


def kernel(x, meta_tokens, attn_norm_w, w_in, q_a_norm_w, w_q_b, kv_a_norm_w, w_kv_b, q_norm_w, k_norm_w, mla_out_norm_w, dn_conv_w, dn_A_log, dn_dt_bias, dn_out_norm_w, w_out, ffn_norm_w, w_gate, w_up, ffn_conv_w, ffn_conv_b, w_down):
    assert x.shape[1:] == (SEQ, D_MODEL) and w_in.shape[0] == 1
    hp_meta = jnp.concatenate([jnp.zeros((META_PAD, D_MODEL), x.dtype), meta_tokens.astype(x.dtype)], axis=0)
    return _layer(x, hp_meta, 0, attn_norm_w, w_in, q_a_norm_w, w_q_b, kv_a_norm_w, w_kv_b, q_norm_w,
                  k_norm_w, mla_out_norm_w, dn_conv_w, dn_A_log, dn_dt_bias, dn_out_norm_w, w_out,
                  ffn_norm_w, w_gate, w_up, ffn_conv_w, ffn_conv_b, w_down)
```
